```python
import math
import jax, jax.numpy as jnp
from jax import lax
import numpy as np

D_MODEL = 1024
BATCH = 8
SEQ = 2048
DEPTH = 1

NSA_HEADS = 8
NSA_KV_HEADS = 2
NSA_GQA = NSA_HEADS // NSA_KV_HEADS
NSA_HEAD_DIM = 64
NSA_WIDTH = NSA_HEADS * NSA_HEAD_DIM
NSA_KV_W = NSA_KV_HEADS * NSA_HEAD_DIM
CMP_BLOCK = 32
CMP_STRIDE = 16
SEL_BLOCK = 64
SEL_TOP = 16
WINDOW = 512
Q_BLOCK = 128
SEL_Q_CHUNK = 32
SSD_HEADS = 8
SSD_HEAD_DIM = 64
SSD_WIDTH = SSD_HEADS * SSD_HEAD_DIM
SSD_GROUPS = 2
SSD_STATE = 128
SSD_CONV = 4
SSD_CHUNK = 128
SSD_CONV_DIM = SSD_WIDTH + 2 * SSD_GROUPS * SSD_STATE
MIX_WIDTH = NSA_WIDTH + SSD_WIDTH
NSA_COLS = NSA_WIDTH + 6 * NSA_KV_W + 3 * NSA_HEADS
SSD_COLS = SSD_WIDTH + SSD_CONV_DIM + SSD_HEADS
IN_COLS = NSA_COLS + SSD_COLS
N_BUCKETS = 32
MAX_DISTANCE = 128
N_EXPERTS = 32
TOP_K = 4
D_FF = 1024
SWIGLU_LIMIT = 7.0
SWIGLU_ALPHA = 1.702

EPS = 1e-6
NEG_INF = -1e30
FORCED_SCORE = 1e4

kernel_name = "hymba_nsa_ssd_moe_block"


def rms_norm(x, g):
    xf = x.astype(jnp.float32)
    y = xf * lax.rsqrt(jnp.mean(xf * xf, axis=-1, keepdims=True) + EPS)
    return (y * g.astype(jnp.float32)).astype(x.dtype)


def masked_softmax(logits, mask):
    logits = jnp.where(mask, logits.astype(jnp.float32), NEG_INF)
    m = jnp.max(logits, axis=-1, keepdims=True)
    p = jnp.where(mask, jnp.exp(logits - m), 0.0)
    return p / jnp.maximum(jnp.sum(p, axis=-1, keepdims=True), 1e-30)


def t5_bucket(dist):
    n = jnp.maximum(dist, 0)
    max_exact = N_BUCKETS // 2
    nf = jnp.maximum(n, max_exact).astype(jnp.float32)
    large = max_exact + (jnp.log(nf / max_exact) / math.log(MAX_DISTANCE / max_exact)
                         * (N_BUCKETS - max_exact)).astype(jnp.int32)
    large = jnp.minimum(large, N_BUCKETS - 1)
    return jnp.where(n < max_exact, n, large)


def nsa_mixer(q, k_cmp, v_cmp, k_slc, v_slc, k_win, v_win, gates, rel_bias, cmp_pos, cmp_w1, cmp_w2):
    B, S = q.shape[:2]
    HKV, G, Dh = NSA_KV_HEADS, NSA_GQA, NSA_HEAD_DIM
    scale = Dh ** -0.5
    pos = jnp.arange(S)
    f32 = jnp.float32

    n_cmp = (S - CMP_BLOCK) // CMP_STRIDE + 1
    blk_idx = np.arange(n_cmp)[:, None] * CMP_STRIDE + np.arange(CMP_BLOCK)[None, :]

    def compress(t, j):
        blocks = t[:, blk_idx] + cmp_pos[j][None, None, :, None, :]
        blocks = blocks.transpose(0, 1, 3, 2, 4).reshape(B, n_cmp, HKV, CMP_BLOCK * Dh)
        return jax.nn.silu(blocks @ cmp_w1[j]) @ cmp_w2[j]

    kc = compress(k_cmp, 0)
    vc = compress(v_cmp, 1)
    cmp_end = np.arange(n_cmp) * CMP_STRIDE + CMP_BLOCK - 1
    dist_c = pos[:, None] - cmp_end[None, :]
    bias_c = rel_bias[t5_bucket(dist_c)].reshape(S, n_cmp, HKV, G).transpose(2, 3, 0, 1)
    logits_c = jnp.einsum('bskgd,bckd->bkgsc', q, kc, preferred_element_type=f32) * scale + bias_c
    p_c = masked_softmax(logits_c, dist_c >= 0)
    o_cmp = jnp.einsum('bkgsc,bckd->bskgd', p_c, vc).astype(q.dtype)

    n_sel = S // SEL_BLOCK
    c_start = np.arange(n_cmp) * CMP_STRIDE
    s_start = np.arange(n_sel) * SEL_BLOCK
    overlap = ((c_start[:, None] < s_start[None, :] + SEL_BLOCK)
               & (c_start[:, None] + CMP_BLOCK > s_start[None, :])).astype(np.float32)
    imp = jnp.einsum('bkgsc,cn->bksn', p_c, jnp.asarray(overlap))
    blk_of_t = pos // SEL_BLOCK
    sblk = jnp.arange(n_sel)
    forced = (sblk[None, :] == 0) | (sblk[None, :] == blk_of_t[:, None]) | (sblk[None, :] == blk_of_t[:, None] - 1)
    causal_blk = sblk[None, :] <= blk_of_t[:, None]
    score = jnp.where(forced, FORCED_SCORE, jnp.where(causal_blk, imp, -1.0))
    k_eff = min(SEL_TOP, n_sel)
    _, sel_idx = lax.top_k(score, k_eff)

    ks = k_slc.reshape(B, n_sel, SEL_BLOCK, HKV, Dh).transpose(0, 3, 1, 2, 4)
    vs = v_slc.reshape(B, n_sel, SEL_BLOCK, HKV, Dh).transpose(0, 3, 1, 2, 4)
    n_chunk = S // SEL_Q_CHUNK
    q_ch = q.reshape(B, n_chunk, SEL_Q_CHUNK, HKV, G, Dh).transpose(1, 0, 2, 3, 4, 5)
    idx_ch = sel_idx.reshape(B, HKV, n_chunk, SEL_Q_CHUNK, k_eff).transpose(2, 0, 1, 3, 4)
    b_ar = jnp.arange(B)[:, None, None, None]
    h_ar = jnp.arange(HKV)[None, :, None, None]
    h_ar5 = jnp.arange(HKV)[None, :, None, None, None]
    rel_kg = rel_bias.reshape(N_BUCKETS, HKV, G).transpose(1, 0, 2)

    def sel_chunk(args):
        c, qc, ic = args
        t = c * SEL_Q_CHUNK + jnp.arange(SEL_Q_CHUNK)
        kg = ks[b_ar, h_ar, ic]
        vg = vs[b_ar, h_ar, ic]
        key_pos = ic[..., None] * SEL_BLOCK + jnp.arange(SEL_BLOCK)
        dist = t[None, None, :, None, None] - key_pos
        bias = rel_kg[h_ar5, t5_bucket(dist)]
        bias = bias.transpose(0, 1, 5, 2, 3, 4)
        logits = jnp.einsum('bqkgd,bkqnjd->bkgqnj', qc, kg, preferred_element_type=f32) * scale + bias
        n_keys = k_eff * SEL_BLOCK
        logits = logits.reshape(B, HKV, G, SEL_Q_CHUNK, n_keys)
        mask = (dist >= 0).reshape(B, HKV, 1, SEL_Q_CHUNK, n_keys)
        p = masked_softmax(logits, mask)
        return jnp.einsum('bkgqm,bkqmd->bqkgd', p, vg.reshape(B, HKV, SEL_Q_CHUNK, n_keys, Dh)).astype(q.dtype)

    o_slc = lax.map(sel_chunk, (jnp.arange(n_chunk), q_ch, idx_ch))
    o_slc = o_slc.transpose(1, 0, 2, 3, 4, 5).reshape(B, S, HKV, G, Dh)

    n_qb = S // Q_BLOCK
    n_band = WINDOW // Q_BLOCK + 1
    band = n_band * Q_BLOCK
    kw = jnp.pad(k_win, ((0, 0), (WINDOW, 0), (0, 0), (0, 0))).reshape(B, n_qb + WINDOW // Q_BLOCK, Q_BLOCK, HKV, Dh)
    vw = jnp.pad(v_win, ((0, 0), (WINDOW, 0), (0, 0), (0, 0))).reshape(B, n_qb + WINDOW // Q_BLOCK, Q_BLOCK, HKV, Dh)
    q_b = q.reshape(B, n_qb, Q_BLOCK, HKV, G, Dh).transpose(1, 0, 2, 3, 4, 5)
    m_loc = np.arange(band)
    dist_w = np.arange(Q_BLOCK)[:, None] + WINDOW - m_loc[None, :]
    bias_w = rel_bias[t5_bucket(jnp.asarray(dist_w))].reshape(Q_BLOCK, band, HKV, G).transpose(2, 3, 0, 1)
    in_window = jnp.asarray((dist_w >= 0) & (dist_w < WINDOW))

    def win_block(args):
        c, qc = args
        kb = lax.dynamic_slice_in_dim(kw, c, n_band, axis=1).reshape(B, band, HKV, Dh)
        vb = lax.dynamic_slice_in_dim(vw, c, n_band, axis=1).reshape(B, band, HKV, Dh)
        key_pos = c * Q_BLOCK - WINDOW + jnp.arange(band)
        mask = in_window & (key_pos >= 0)[None, :]
        logits = jnp.einsum('bqkgd,bmkd->bkgqm', qc, kb, preferred_element_type=f32) * scale + bias_w
        p = masked_softmax(logits, mask)
        return jnp.einsum('bkgqm,bmkd->bqkgd', p, vb).astype(q.dtype)

    o_win = lax.map(win_block, (jnp.arange(n_qb), q_b))
    o_win = o_win.transpose(1, 0, 2, 3, 4, 5).reshape(B, S, HKV, G, Dh)

    o = (gates[:, :, 0, ..., None] * o_cmp + gates[:, :, 1, ..., None] * o_slc
         + gates[:, :, 2, ..., None] * o_win)
    return o.reshape(B, S, NSA_WIDTH)


def ssd_mixer(z, xbc, dt_raw, conv_w, conv_b, dt_bias, a_log, d_skip, norm_g):
    B, S, _ = xbc.shape
    H, P, N, Gr, L = SSD_HEADS, SSD_HEAD_DIM, SSD_STATE, SSD_GROUPS, SSD_CHUNK
    f32 = jnp.float32
    xpad = jnp.pad(xbc, ((0, 0), (SSD_CONV - 1, 0), (0, 0)))
    conv = conv_b
    for k in range(SSD_CONV):
        conv = conv + xpad[:, k:k + S] * conv_w[k]
    xbc = jax.nn.silu(conv)
    xs = xbc[..., :SSD_WIDTH].reshape(B, S, H, P)
    Bm = xbc[..., SSD_WIDTH:SSD_WIDTH + Gr * N].reshape(B, S, Gr, N)
    Cm = xbc[..., SSD_WIDTH + Gr * N:].reshape(B, S, Gr, N)
    Bh = jnp.repeat(Bm, H // Gr, axis=2).astype(f32)
    Ch = jnp.repeat(Cm, H // Gr, axis=2).astype(f32)
    dt = jax.nn.softplus((dt_raw + dt_bias).astype(f32))
    A = -jnp.exp(a_log.astype(f32))
    nc = S // L
    xc = (xs.astype(f32) * dt[..., None]).reshape(B, nc, L, H, P)
    Bc = Bh.reshape(B, nc, L, H, N)
    Cc = Ch.reshape(B, nc, L, H, N)
    a_cs = jnp.cumsum((dt * A).reshape(B, nc, L, H).transpose(0, 3, 1, 2), axis=-1)
    causal = np.tril(np.ones((L, L), dtype=bool))
    decay_ls = jnp.exp(jnp.where(causal, a_cs[..., :, None] - a_cs[..., None, :], NEG_INF))
    scores = jnp.einsum('bclhn,bcshn->bhcls', Cc, Bc) * decay_ls
    y_diag = jnp.einsum('bhcls,bcshp->bclhp', scores, xc)
    decay_states = jnp.exp(a_cs[..., -1:] - a_cs).transpose(0, 2, 3, 1)
    states = jnp.einsum('bclhn,bclhp->bchpn', Bc * decay_states[..., None], xc)
    chunk_decay = jnp.exp(a_cs[..., -1])

    def step(h, inp):
        s_c, d_c = inp
        return d_c[..., None, None] * h + s_c, h

    h0 = jnp.zeros((B, H, P, N), f32)
    _, prev = lax.scan(step, h0, (states.transpose(1, 0, 2, 3, 4), chunk_decay.transpose(2, 0, 1)))
    prev = prev.transpose(1, 0, 2, 3, 4)
    decay_in = jnp.exp(a_cs).transpose(0, 2, 3, 1)
    y_off = jnp.einsum('bclhn,bchpn->bclhp', Cc * decay_in[..., None], prev)
    y = (y_diag + y_off).reshape(B, S, H, P) + xs.astype(f32) * d_skip.astype(f32)[:, None]
    y = y.reshape(B, S, SSD_WIDTH) * jax.nn.silu(z.astype(f32))
    y = rms_norm(y.reshape(B, S, Gr, SSD_WIDTH // Gr), norm_g.reshape(Gr, SSD_WIDTH // Gr))
    return y.reshape(B, S, SSD_WIDTH).astype(z.dtype)


def moe_ffn(h, router_w, router_b, w_gate_up, b_gate_up, w_down, b_down):
    B, S, D = h.shape
    xt = h.reshape(B * S, D)
    logits = (xt @ router_w + router_b).astype(jnp.float32)
    top_v, top_i = lax.top_k(logits, TOP_K)
    top_w = jax.nn.softmax(top_v, axis=-1)
    combine = jnp.sum(jax.nn.one_hot(top_i, N_EXPERTS, dtype=jnp.float32) * top_w[..., None], axis=1)
    out = jnp.zeros((B * S, D), jnp.float32)
    for e in range(N_EXPERTS):
        gu = xt @ w_gate_up[e] + b_gate_up[e]
        gate = jnp.minimum(gu[:, :D_FF], SWIGLU_LIMIT)
        up = jnp.clip(gu[:, D_FF:], -SWIGLU_LIMIT, SWIGLU_LIMIT)
        act = (up + 1.0) * gate * jax.nn.sigmoid(SWIGLU_ALPHA * gate)
        out = out + combine[:, e:e + 1] * (act @ w_down[e] + b_down[e])
    return out.astype(h.dtype).reshape(B, S, D)


def setup_inputs(seed: int = 0) -> dict:
    key = jax.random.key(seed)
    ks = jax.random.split(key, 24)
    f32 = jnp.float32
    nrm = lambda k, shape, s: jax.random.normal(k, shape, f32) * s
    gain = lambda k, shape: 1.0 + 0.01 * jax.random.normal(k, shape, f32)
    dt0 = jnp.exp(jax.random.uniform(ks[10], (DEPTH, SSD_HEADS), f32, math.log(1e-3), math.log(1e-1)))
    return {
        "x": nrm(ks[0], (BATCH, SEQ, D_MODEL), 1.0),
        "attn_norm": gain(ks[1], (DEPTH, D_MODEL)),
        "w_in": nrm(ks[2], (DEPTH, D_MODEL, IN_COLS), D_MODEL ** -0.5),
        "rel_bias": nrm(ks[3], (N_BUCKETS, NSA_HEADS), 0.2),
        "cmp_pos": nrm(ks[4], (DEPTH, 2, CMP_BLOCK, NSA_HEAD_DIM), 0.1),
        "cmp_w1": nrm(ks[5], (DEPTH, 2, CMP_BLOCK * NSA_HEAD_DIM, NSA_HEAD_DIM), (CMP_BLOCK * NSA_HEAD_DIM) ** -0.5),
        "cmp_w2": nrm(ks[6], (DEPTH, 2, NSA_HEAD_DIM, NSA_HEAD_DIM), NSA_HEAD_DIM ** -0.5),
        "attn_out_norm": gain(ks[7], (DEPTH, NSA_WIDTH)),
        "conv_w": nrm(ks[8], (DEPTH, SSD_CONV, SSD_CONV_DIM), SSD_CONV ** -0.5),
        "conv_b": nrm(ks[9], (DEPTH, SSD_CONV_DIM), 0.01),
        "dt_bias": dt0 + jnp.log(-jnp.expm1(-dt0)),
        "a_log": jnp.log(jax.random.uniform(ks[11], (DEPTH, SSD_HEADS), f32, 1.0, 16.0)),
        "d_skip": gain(ks[12], (DEPTH, SSD_HEADS)),
        "ssm_out_norm": gain(ks[13], (DEPTH, SSD_WIDTH)),
        "w_out": nrm(ks[14], (DEPTH, MIX_WIDTH, D_MODEL), MIX_WIDTH ** -0.5),
        "ffn_norm": gain(ks[15], (DEPTH, D_MODEL)),
        "router_w": nrm(ks[16], (DEPTH, D_MODEL, N_EXPERTS), D_MODEL ** -0.5),
        "router_b": nrm(ks[17], (DEPTH, N_EXPERTS), 0.01),
        "w_gate_up": nrm(ks[18], (DEPTH, N_EXPERTS, D_MODEL, 2 * D_FF), D_MODEL ** -0.5),
        "b_gate_up": nrm(ks[19], (DEPTH, N_EXPERTS, 2 * D_FF), 0.01),
        "w_down": nrm(ks[20], (DEPTH, N_EXPERTS, D_FF, D_MODEL), D_FF ** -0.5),
        "b_down": nrm(ks[21], (DEPTH, N_EXPERTS, D_MODEL), 0.01),
        "final_norm": gain(ks[22], (D_MODEL,)),
    }


def reference(x, attn_norm, w_in, rel_bias, cmp_pos, cmp_w1, cmp_w2, attn_out_norm, conv_w, conv_b,
              dt_bias, a_log, d_skip, ssm_out_norm, w_out, ffn_norm, router_w, router_b,
              w_gate_up, b_gate_up, w_down, b_down, final_norm):
    B, S, _ = x.shape
    HKV, G, Dh = NSA_KV_HEADS, NSA_GQA, NSA_HEAD_DIM
    h = x
    for l in range(DEPTH):
        xn = rms_norm(h, attn_norm[l])
        proj = xn @ w_in[l]
        nsa_p = proj[..., :NSA_COLS]
        ssd_p = proj[..., NSA_COLS:]
        q = nsa_p[..., :NSA_WIDTH].reshape(B, S, HKV, G, Dh)
        kv = nsa_p[..., NSA_WIDTH:NSA_WIDTH + 6 * NSA_KV_W].reshape(B, S, 6, HKV, Dh)
        gates = jax.nn.sigmoid(nsa_p[..., NSA_WIDTH + 6 * NSA_KV_W:].reshape(B, S, 3, HKV, G))
        o_nsa = nsa_mixer(q, kv[:, :, 0], kv[:, :, 1], kv[:, :, 2], kv[:, :, 3], kv[:, :, 4], kv[:, :, 5],
                          gates, rel_bias, cmp_pos[l], cmp_w1[l], cmp_w2[l])
        o_nsa = rms_norm(o_nsa, attn_out_norm[l])
        z = ssd_p[..., :SSD_WIDTH]
        xbc = ssd_p[..., SSD_WIDTH:SSD_WIDTH + SSD_CONV_DIM]
        dt_raw = ssd_p[..., SSD_WIDTH + SSD_CONV_DIM:]
        o_ssd = ssd_mixer(z, xbc, dt_raw, conv_w[l], conv_b[l], dt_bias[l], a_log[l], d_skip[l], ssm_out_norm[l])
        h = h + jnp.concatenate([o_nsa, o_ssd], axis=-1) @ w_out[l]
        h = h + moe_ffn(rms_norm(h, ffn_norm[l]), router_w[l], router_b[l], w_gate_up[l], b_gate_up[l],
                        w_down[l], b_down[l])
    return rms_norm(h, final_norm)
```

```python
import functools
import math

import numpy as np
import jax
import jax.numpy as jnp
from jax import lax
from jax.experimental import pallas as pl
from jax.experimental.pallas import tpu as pltpu

F32 = jnp.float32
BF16 = jnp.bfloat16

D_MODEL = 1024
NSA_HEADS = 8
HKV = 2
GQA = NSA_HEADS // HKV
DH = 64
NSA_WIDTH = NSA_HEADS * DH
CMP_BLOCK = 32
CMP_STRIDE = 16
SEL_BLOCK = 64
SEL_TOP = 16
WINDOW = 512
TQ = 128
SSD_HEADS = 8
SSD_P = 64
SSD_WIDTH = SSD_HEADS * SSD_P
SSD_GROUPS = 2
SSD_N = 128
SSD_CONV = 4
SSD_L = 128
SSD_CONV_DIM = SSD_WIDTH + 2 * SSD_GROUPS * SSD_N
N_BUCKETS = 32
MAX_DISTANCE = 128
N_EXPERTS = 32
TOP_K = 4
D_FF = 1024
SWIGLU_LIMIT = 7.0
SWIGLU_ALPHA = 1.702

EPS = 1e-6
NEG = -1e30
FORCED_SCORE = 1e4
LOG2E = 1.4426950408889634
LANES = 128
VMEM_LIMIT = 56 * 1024 * 1024


def _cparams(*sem):
    return pltpu.CompilerParams(dimension_semantics=sem, vmem_limit_bytes=VMEM_LIMIT)


def _silu(v):
    return v / (1.0 + jnp.exp(-v))


_Q0, _Q1 = 0, NSA_HEADS * LANES
_KV0 = _Q1
_G0 = _KV0 + 6 * LANES
_Z0 = _G0 + LANES
_X0 = _Z0 + SSD_WIDTH
_DT0 = _X0 + SSD_CONV_DIM
_WCOLS = _DT0 + LANES


def _pad_w_in(w_in):
    d = w_in.shape[0]
    nsa_cols = NSA_WIDTH + 6 * HKV * DH + 3 * NSA_HEADS
    wq = w_in[:, :NSA_WIDTH].reshape(d, HKV, GQA, DH)
    zq = jnp.zeros_like(wq)
    q0 = jnp.concatenate([wq[:, 0], zq[:, 0]], axis=-1)
    q1 = jnp.concatenate([zq[:, 1], wq[:, 1]], axis=-1)
    wq_pad = jnp.stack([q0, q1], axis=1).reshape(d, NSA_HEADS * LANES)
    wkv = w_in[:, NSA_WIDTH:NSA_WIDTH + 6 * HKV * DH]
    wg = w_in[:, NSA_WIDTH + 6 * HKV * DH:nsa_cols]
    wg = jnp.pad(wg, ((0, 0), (0, LANES - wg.shape[1])))
    wz = w_in[:, nsa_cols:nsa_cols + SSD_WIDTH]
    wx = w_in[:, nsa_cols + SSD_WIDTH:nsa_cols + SSD_WIDTH + SSD_CONV_DIM]
    wdt = w_in[:, nsa_cols + SSD_WIDTH + SSD_CONV_DIM:]
    wdt = jnp.pad(wdt, ((0, 0), (0, LANES - wdt.shape[1])))
    return jnp.concatenate([wq_pad, wkv, wg, wz, wx, wdt], axis=1).astype(BF16)


def _in_proj_kernel(x_ref, g_ref, w_ref, q_ref, kc_ref, vc_ref, ks_ref, vs_ref, kw_ref, vw_ref,
                    gate_ref, z_ref, xbc_ref, dt_ref):
    x = x_ref[...]
    ms = jnp.mean(x * x, axis=-1, keepdims=True)
    xn = (x * lax.rsqrt(ms + EPS) * g_ref[...]).astype(BF16)

    def seg(lo, hi):
        return jnp.dot(xn, w_ref[:, lo:hi], preferred_element_type=F32)

    q_ref[...] = (seg(_Q0, _Q1) * (DH ** -0.5 * LOG2E)).astype(BF16)
    for j, ref in enumerate((kc_ref, vc_ref, ks_ref, vs_ref, kw_ref, vw_ref)):
        ref[...] = seg(_KV0 + j * LANES, _KV0 + (j + 1) * LANES).astype(BF16)
    gate_ref[...] = 1.0 / (1.0 + jnp.exp(-seg(_G0, _Z0)))
    z_ref[...] = seg(_Z0, _X0)
    xbc_ref[...] = seg(_X0, _DT0)
    dt_ref[...] = seg(_DT0, _WCOLS)


def _in_proj(x2, attn_norm, w_pad, tm=512):
    t = x2.shape[0]
    row = lambda w: pl.BlockSpec((tm, w), lambda i: (i, 0))
    full = lambda a: pl.BlockSpec(a.shape, lambda i: (0,) * a.ndim)
    g = attn_norm.reshape(1, D_MODEL)
    outs = ([jax.ShapeDtypeStruct((t, NSA_HEADS * LANES), BF16)]
            + [jax.ShapeDtypeStruct((t, LANES), BF16)] * 6
            + [jax.ShapeDtypeStruct((t, LANES), F32),
               jax.ShapeDtypeStruct((t, SSD_WIDTH), F32),
               jax.ShapeDtypeStruct((t, SSD_CONV_DIM), F32),
               jax.ShapeDtypeStruct((t, LANES), F32)])
    return pl.pallas_call(
        _in_proj_kernel,
        out_shape=outs,
        grid=(t // tm,),
        in_specs=[row(D_MODEL), full(g), full(w_pad)],
        out_specs=[row(s.shape[1]) for s in outs],
        compiler_params=_cparams("parallel"),
        name="in_proj",
    )(x2, g, w_pad)


def _compress_weights(cmp_pos, cmp_w1, cmp_w2):
    half = CMP_BLOCK // 2
    eye = jnp.eye(HKV, dtype=F32)
    w1 = cmp_w1.reshape(2, CMP_BLOCK, DH, DH)
    w1big = jnp.einsum('jlde,hk->jlhdke', w1, eye)
    w1lo = w1big[:, :half].reshape(2, half * HKV * DH, HKV * DH).astype(BF16)
    w1hi = w1big[:, half:].reshape(2, half * HKV * DH, HKV * DH).astype(BF16)
    pos = jnp.broadcast_to(cmp_pos[:, :, None, :], (2, CMP_BLOCK, HKV, DH))
    poslo = pos[:, :half].reshape(2, 1, half * HKV * DH)
    poshi = pos[:, half:].reshape(2, 1, half * HKV * DH)
    w2big = jnp.einsum('jde,hk->jhdke', cmp_w2, eye).reshape(2, HKV * DH, HKV * DH).astype(BF16)
    return w1lo, w1hi, poslo, poshi, w2big


def _compress_kernel(kr_ref, vr_ref, w1lo_ref, w1hi_ref, poslo_ref, poshi_ref, w2_ref, kc_ref, vc_ref):
    for j, (src, dst) in enumerate(((kr_ref, kc_ref), (vr_ref, vc_ref))):
        r = src[0].astype(F32)
        a = jnp.dot((r + poslo_ref[j]).astype(BF16), w1lo_ref[j], preferred_element_type=F32)
        b = jnp.dot((r + poshi_ref[j]).astype(BF16), w1hi_ref[j], preferred_element_type=F32)
        hid = a + pltpu.roll(b, b.shape[0] - 1, 0)
        dst[0] = jnp.dot(_silu(hid).astype(BF16), w2_ref[j], preferred_element_type=F32).astype(BF16)


def _compress(kr, vr, cw):
    b, ng, width = kr.shape
    w1lo, w1hi, poslo, poshi, w2big = cw
    full = lambda a: pl.BlockSpec(a.shape, lambda i: (0,) * a.ndim)
    bspec = pl.BlockSpec((1, ng, width), lambda i: (i, 0, 0))
    ospec = pl.BlockSpec((1, ng, HKV * DH), lambda i: (i, 0, 0))
    out = jax.ShapeDtypeStruct((b, ng, HKV * DH), BF16)
    return pl.pallas_call(
        _compress_kernel,
        out_shape=[out, out],
        grid=(b,),
        in_specs=[bspec, bspec, full(w1lo), full(w1hi), full(poslo), full(poshi), full(w2big)],
        out_specs=[ospec, ospec],
        compiler_params=_cparams("parallel"),
        name="nsa_compress",
    )(kr, vr, w1lo, w1hi, poslo, poshi, w2big)


def _t5_bucket(dist):
    n = jnp.maximum(dist, 0)
    max_exact = N_BUCKETS // 2
    nf = jnp.maximum(n, max_exact).astype(F32)
    large = max_exact + (jnp.log(nf / max_exact) / math.log(MAX_DISTANCE / max_exact)
                         * (N_BUCKETS - max_exact)).astype(jnp.int32)
    large = jnp.minimum(large, N_BUCKETS - 1)
    return jnp.where(n < max_exact, n, large)


def _bias_tables(rel_bias, s):
    n_cmp = (s - CMP_BLOCK) // CMP_STRIDE + 1
    assert n_cmp < LANES and TQ >= MAX_DISTANCE
    tbl = rel_bias[_t5_bucket(jnp.arange(s))] * LOG2E
    far = tbl[s - 1]

    def lookup(dist, valid, shift):
        v = tbl[jnp.clip(dist, 0, s - 1)] - shift
        v = jnp.where(valid[..., None], v, NEG)
        r, c = dist.shape
        return v.transpose(2, 0, 1).reshape(HKV, GQA, r, c)

    pos = jnp.arange(s)[:, None]
    cidx = jnp.arange(LANES)[None, :]
    dist_c = pos - (cidx * CMP_STRIDE + CMP_BLOCK - 1)
    bias_c = lookup(dist_c, (dist_c >= 0) & (cidx < n_cmp), 0.0)
    qi = jnp.arange(TQ)[:, None]
    band = WINDOW + TQ
    dist_w = qi + WINDOW - jnp.arange(band)[None, :]
    bias_w = lookup(dist_w, (dist_w >= 0) & (dist_w < WINDOW), 0.0).reshape(HKV, GQA * TQ, band)
    dist_s = qi + TQ - jnp.arange(2 * TQ)[None, :]
    bias_s = lookup(dist_s, dist_s >= 0, far).reshape(HKV, GQA * TQ, 2 * TQ)
    return bias_c, bias_w, bias_s


def _sel_tables(s):
    n_cmp = (s - CMP_BLOCK) // CMP_STRIDE + 1
    n_sel = s // SEL_BLOCK
    c_start = np.arange(LANES) * CMP_STRIDE
    s_start = np.arange(n_sel) * SEL_BLOCK
    ovl = ((c_start[None, :] < s_start[:, None] + SEL_BLOCK)
           & (c_start[None, :] + CMP_BLOCK > s_start[:, None])
           & (np.arange(LANES)[None, :] < n_cmp)).astype(np.float32)
    key_blk = np.arange(s) // SEL_BLOCK
    expand = np.where(key_blk[None, :] == np.arange(LANES)[:, None], NEG, 0.0).astype(np.float32)
    expand = expand.reshape(LANES, s // TQ, TQ).transpose(1, 0, 2)
    return jnp.asarray(ovl, BF16), jnp.asarray(expand, BF16)


def _nsa_kernel(q_ref, kc_ref, vc_ref, ks_ref, vs_ref, kw_ref, vw_ref, gate_ref,
                bc_ref, bw_ref, bs_ref, ovl_ref, exp_ref, gn_ref, o_ref,
                s_ref, m_ref, l_ref, acc_ref, ob_ref):
    i = pl.program_id(1)
    rows = GQA * TQ
    n_sel = ovl_ref.shape[0]
    nt_dims = (((1,), (1,)), ((), ()))

    def finish(acc, lsum):
        l = jnp.sum(lsum, axis=-1, keepdims=True)
        return acc / jnp.maximum(l, 1e-30)

    def second_pass(n_chunks, v_chunk, m):
        l_ref[...] = jnp.zeros(l_ref.shape, F32)
        acc_ref[...] = jnp.zeros(acc_ref.shape, F32)

        def body(c, carry):
            p = jnp.exp2(s_ref[c] - m)
            l_ref[...] += p
            acc_ref[...] += jnp.dot(p.astype(BF16), v_chunk(c), preferred_element_type=F32)
            return carry

        lax.fori_loop(0, n_chunks, body, 0)
        return finish(acc_ref[...], l_ref[...])

    for h in range(HKV):
        qs = jnp.concatenate([q_ref[0, :, (h * GQA + g) * LANES:(h * GQA + g + 1) * LANES]
                              for g in range(GQA)], axis=0)

        bias_c = bc_ref[h].reshape(rows, LANES)
        sc = lax.dot_general(qs, kc_ref[0], nt_dims, preferred_element_type=F32) + bias_c
        mc = jnp.max(sc, axis=-1, keepdims=True)
        pc = jnp.where(bias_c > 0.5 * NEG, jnp.exp2(sc - mc), 0.0)
        pc = pc / jnp.maximum(jnp.sum(pc, axis=-1, keepdims=True), 1e-30)
        o_cmp = jnp.dot(pc.astype(BF16), vc_ref[0], preferred_element_type=F32)

        psum = jnp.sum(pc.reshape(GQA, TQ, LANES), axis=0)
        p_hi = psum.astype(BF16)
        p_lo = (psum - p_hi.astype(F32)).astype(BF16)
        imp = (lax.dot_general(ovl_ref[...], p_hi, nt_dims, preferred_element_type=F32)
               + lax.dot_general(ovl_ref[...], p_lo, nt_dims, preferred_element_type=F32))
        blk = lax.broadcasted_iota(jnp.int32, (n_sel, TQ), 0)
        tok = lax.broadcasted_iota(jnp.int32, (n_sel, TQ), 1) + i * TQ
        blk_of_t = tok // SEL_BLOCK
        forced = (blk == 0) | (blk == blk_of_t) | (blk == blk_of_t - 1)
        score = jnp.where(forced, FORCED_SCORE, jnp.where(blk <= blk_of_t, imp, -1.0))
        rank = jnp.zeros((n_sel, TQ), F32)
        for mm in range(n_sel):
            sm = score[mm:mm + 1, :]
            ahead = (sm > score) | ((sm == score) & (blk > mm))
            rank = rank + jnp.where(ahead, 1.0, 0.0)
        unsel_t = jnp.where(rank < min(SEL_TOP, n_sel), 0.0, 1.0)
        unsel_t = jnp.concatenate([unsel_t, jnp.zeros((LANES - n_sel, TQ), F32)], axis=0).astype(BF16)
        eye = (lax.broadcasted_iota(jnp.int32, (TQ, TQ), 0)
               == lax.broadcasted_iota(jnp.int32, (TQ, TQ), 1)).astype(BF16)
        unsel = lax.dot_general(eye, unsel_t, nt_dims, preferred_element_type=F32).astype(BF16)

        def sel_scores(c, bias):
            k = ks_ref[0, pl.ds(pl.multiple_of(c * TQ, TQ), TQ), :]
            madd = jnp.dot(unsel, exp_ref[c], preferred_element_type=F32)
            s = lax.dot_general(qs, k, nt_dims, preferred_element_type=F32)
            s = (s.reshape(GQA, TQ, TQ) + madd[None]).reshape(rows, TQ)
            if bias is not None:
                s = s + bias
            s_ref[c] = s
            m_ref[...] = jnp.maximum(m_ref[...], s)

        m_ref[...] = jnp.full(m_ref.shape, NEG, F32)

        def far_body(c, carry):
            sel_scores(c, None)
            return carry

        lax.fori_loop(0, jnp.maximum(i - 1, 0), far_body, 0)

        @pl.when(i >= 1)
        def _():
            sel_scores(i - 1, bs_ref[h, :, 0:TQ])

        sel_scores(i, bs_ref[h, :, TQ:2 * TQ])
        m = jnp.max(m_ref[...], axis=-1, keepdims=True)
        o_slc = second_pass(i + 1, lambda c: vs_ref[0, pl.ds(pl.multiple_of(c * TQ, TQ), TQ), :], m)

        n_band = WINDOW // TQ + 1
        m_ref[...] = jnp.full(m_ref.shape, NEG, F32)
        for jj in range(n_band):
            c = i - (n_band - 1) + jj
            cc = jnp.maximum(c, 0)
            k = kw_ref[0, pl.ds(pl.multiple_of(cc * TQ, TQ), TQ), :]
            s = lax.dot_general(qs, k, nt_dims, preferred_element_type=F32)
            s = s + bw_ref[h, :, jj * TQ:(jj + 1) * TQ] + jnp.where(c >= 0, 0.0, NEG)
            s_ref[jj] = s
            m_ref[...] = jnp.maximum(m_ref[...], s)
        m = jnp.max(m_ref[...], axis=-1, keepdims=True)

        def v_win(jj):
            cc = jnp.maximum(i - (n_band - 1) + jj, 0)
            return vw_ref[0, pl.ds(pl.multiple_of(cc * TQ, TQ), TQ), :]

        o_win = second_pass(n_band, v_win, m)

        for g in range(GQA):
            hd = h * GQA + g
            r0 = g * TQ
            gates = gate_ref[0]
            o = (gates[:, hd:hd + 1] * o_cmp[r0:r0 + TQ]
                 + gates[:, NSA_HEADS + hd:NSA_HEADS + hd + 1] * o_slc[r0:r0 + TQ]
                 + gates[:, 2 * NSA_HEADS + hd:2 * NSA_HEADS + hd + 1] * o_win[r0:r0 + TQ])
            ob_ref[:, hd * DH:(hd + 1) * DH] = o[:, h * DH:(h + 1) * DH]

    o = ob_ref[...]
    ms = jnp.mean(o * o, axis=-1, keepdims=True)
    o_ref[0] = (o * lax.rsqrt(ms + EPS) * gn_ref[...]).astype(BF16)


def _nsa(q, kc, vc, ks, vs, kw, vw, gates, tables, attn_out_norm):
    b, s, _ = q.shape
    bias_c, bias_w, bias_s, ovl, expand = tables
    gn = attn_out_norm.reshape(1, NSA_WIDTH)
    full = lambda a: pl.BlockSpec(a.shape, lambda bi, i: (0,) * a.ndim)
    seq = pl.BlockSpec((1, s, LANES), lambda bi, i: (bi, 0, 0))
    cmp = pl.BlockSpec((1,) + kc.shape[1:], lambda bi, i: (bi, 0, 0))
    rows = GQA * TQ
    return pl.pallas_call(
        _nsa_kernel,
        out_shape=jax.ShapeDtypeStruct((b, s, NSA_WIDTH), BF16),
        grid=(b, s // TQ),
        in_specs=[pl.BlockSpec((1, TQ, NSA_HEADS * LANES), lambda bi, i: (bi, i, 0)),
                  cmp, cmp, seq, seq, seq, seq,
                  pl.BlockSpec((1, TQ, LANES), lambda bi, i: (bi, i, 0)),
                  pl.BlockSpec((HKV, GQA, TQ, LANES), lambda bi, i: (0, 0, i, 0)),
                  full(bias_w), full(bias_s), full(ovl), full(expand), full(gn)],
        out_specs=pl.BlockSpec((1, TQ, NSA_WIDTH), lambda bi, i: (bi, i, 0)),
        scratch_shapes=[pltpu.VMEM((s // TQ, rows, TQ), F32),
                        pltpu.VMEM((rows, TQ), F32),
                        pltpu.VMEM((rows, TQ), F32),
                        pltpu.VMEM((rows, LANES), F32),
                        pltpu.VMEM((TQ, NSA_WIDTH), F32)],
        compiler_params=_cparams("parallel", "arbitrary"),
        name="nsa_attention",
    )(q, kc, vc, ks, vs, kw, vw, gates, bias_c, bias_w, bias_s, ovl, expand, gn)


def _ssd_kernel(xbc_ref, z_ref, dt_ref, cw_ref, cb_ref, dtb_ref, alog_ref, dsk_ref, ng_ref, o_ref,
                xbuf_ref, state_ref, y_ref):
    c = pl.program_id(1)
    L, P, N = SSD_L, SSD_P, SSD_N
    hpg = SSD_HEADS // SSD_GROUPS
    pad = 8

    @pl.when(c == 0)
    def _():
        xbuf_ref[0:pad, :] = jnp.zeros((pad, SSD_CONV_DIM), F32)
        state_ref[...] = jnp.zeros(state_ref.shape, F32)

    xbuf_ref[pad:pad + L, :] = xbc_ref[0]
    conv = cb_ref[...]
    for k in range(SSD_CONV):
        shift = SSD_CONV - 1 - k
        conv = conv + xbuf_ref[pad - shift:pad - shift + L, :] * cw_ref[k:k + 1, :]
    xbuf_ref[0:pad, :] = xbuf_ref[L:L + pad, :]
    xa = _silu(conv)
    xs = xa[:, :SSD_WIDTH]
    bm = xa[:, SSD_WIDTH:SSD_WIDTH + SSD_GROUPS * N]
    cm = xa[:, SSD_WIDTH + SSD_GROUPS * N:]

    dtv = dt_ref[0] + dtb_ref[...]
    dt = jnp.maximum(dtv, 0.0) + jnp.log1p(jnp.exp(-jnp.abs(dtv)))
    a = dt * (-jnp.exp(alog_ref[...]))
    ri = lax.broadcasted_iota(jnp.int32, (L, L), 0)
    ci = lax.broadcasted_iota(jnp.int32, (L, L), 1)
    causal = ri >= ci
    cs = jnp.dot(causal.astype(F32), a, preferred_element_type=F32,
                 precision=lax.Precision.HIGHEST)
    cs_t = cs.T
    nt_dims = (((1,), (1,)), ((), ()))

    for gr in range(SSD_GROUPS):
        b_g = bm[:, gr * N:(gr + 1) * N]
        c_g = cm[:, gr * N:(gr + 1) * N]
        scores = lax.dot_general(c_g.astype(BF16), b_g.astype(BF16), nt_dims, preferred_element_type=F32)
        b_gt = b_g.T
        for hh in range(hpg):
            h = gr * hpg + hh
            cs_col = cs[:, h:h + 1]
            cs_row = cs_t[h:h + 1, :]
            cs_last = cs[L - 1:L, h:h + 1]
            decay = jnp.exp(jnp.where(causal, cs_col - cs_row, NEG))
            xs_h = xs[:, h * P:(h + 1) * P]
            xc = (xs_h * dt[:, h:h + 1]).astype(BF16)
            y = jnp.dot((scores * decay).astype(BF16), xc, preferred_element_type=F32)
            prev = state_ref[h]
            y = y + jnp.dot((c_g * jnp.exp(cs_col)).astype(BF16), prev.astype(BF16),
                            preferred_element_type=F32)
            contrib = jnp.dot((b_gt * jnp.exp(cs_last - cs_row)).astype(BF16), xc,
                              preferred_element_type=F32)
            state_ref[h] = jnp.exp(cs_last) * prev + contrib
            y_ref[:, h * P:(h + 1) * P] = y + xs_h * dsk_ref[:, h * P:(h + 1) * P]

    z = z_ref[0]
    y = y_ref[...] * _silu(z)
    gw = SSD_WIDTH // SSD_GROUPS
    for gr in range(SSD_GROUPS):
        yg = y[:, gr * gw:(gr + 1) * gw]
        ms = jnp.mean(yg * yg, axis=-1, keepdims=True)
        o_ref[0, :, gr * gw:(gr + 1) * gw] = (yg * lax.rsqrt(ms + EPS)
                                              * ng_ref[:, gr * gw:(gr + 1) * gw]).astype(BF16)


def _ssd(xbc, z, dt, conv_w, conv_b, dt_bias, a_log, d_skip, norm_g):
    b, s, _ = xbc.shape
    padl = lambda v: jnp.pad(v, (0, LANES - v.shape[0])).reshape(1, LANES)
    args = (conv_w, conv_b.reshape(1, SSD_CONV_DIM), padl(dt_bias), padl(a_log),
            jnp.repeat(d_skip, SSD_P).reshape(1, SSD_WIDTH), norm_g.reshape(1, SSD_WIDTH))
    full = lambda a: pl.BlockSpec(a.shape, lambda bi, c: (0,) * a.ndim)
    blk = lambda w: pl.BlockSpec((1, SSD_L, w), lambda bi, c: (bi, c, 0))
    return pl.pallas_call(
        _ssd_kernel,
        out_shape=jax.ShapeDtypeStruct((b, s, SSD_WIDTH), BF16),
        grid=(b, s // SSD_L),
        in_specs=[blk(SSD_CONV_DIM), blk(SSD_WIDTH), blk(LANES)] + [full(a) for a in args],
        out_specs=blk(SSD_WIDTH),
        scratch_shapes=[pltpu.VMEM((SSD_L + 8, SSD_CONV_DIM), F32),
                        pltpu.VMEM((SSD_HEADS, SSD_N, SSD_P), F32),
                        pltpu.VMEM((SSD_L, SSD_WIDTH), F32)],
        compiler_params=_cparams("parallel", "arbitrary"),
        name="ssd_mixer",
    )(xbc, z, dt, *args)


def _out_proj_kernel(x_ref, on_ref, os_ref, wo_ref, fg_ref, rw_ref, rb_ref, h_ref, hn_ref, comb_ref):
    h = (x_ref[...]
         + jnp.dot(on_ref[...], wo_ref[0:NSA_WIDTH, :], preferred_element_type=F32)
         + jnp.dot(os_ref[...], wo_ref[NSA_WIDTH:, :], preferred_element_type=F32))
    h_ref[...] = h
    ms = jnp.mean(h * h, axis=-1, keepdims=True)
    hn = h * lax.rsqrt(ms + EPS) * fg_ref[...]
    hn_ref[...] = hn.astype(BF16)
    logits = jnp.dot(hn, rw_ref[...], preferred_element_type=F32,
                     precision=lax.Precision.HIGHEST) + rb_ref[...]
    lane = lax.broadcasted_iota(jnp.int32, logits.shape, 1)
    work = logits
    picks = []
    for _ in range(TOP_K):
        v = jnp.max(work, axis=-1, keepdims=True)
        idx = jnp.min(jnp.where(work == v, lane, LANES), axis=-1, keepdims=True)
        hit = lane == idx
        picks.append((v, hit))
        work = jnp.where(hit, -3e38, work)
    v0 = picks[0][0]
    es = [jnp.exp(v - v0) for v, _ in picks]
    den = es[0] + es[1] + es[2] + es[3]
    comb = jnp.zeros_like(logits)
    for e, (_, hit) in zip(es, picks):
        comb = comb + jnp.where(hit, e / den, 0.0)
    comb_ref[...] = comb


def _out_proj(x2, o_nsa, o_ssd, w_out, ffn_norm, router_w, router_b, tm=512):
    t = x2.shape[0]
    row = lambda w: pl.BlockSpec((tm, w), lambda i: (i, 0))
    full = lambda a: pl.BlockSpec(a.shape, lambda i: (0,) * a.ndim)
    wo = w_out.astype(BF16)
    fg = ffn_norm.reshape(1, D_MODEL)
    rw = jnp.pad(router_w, ((0, 0), (0, LANES - N_EXPERTS)))
    rb = jnp.pad(router_b, (0, LANES - N_EXPERTS), constant_values=NEG).reshape(1, LANES)
    return pl.pallas_call(
        _out_proj_kernel,
        out_shape=[jax.ShapeDtypeStruct((t, D_MODEL), F32),
                   jax.ShapeDtypeStruct((t, D_MODEL), BF16),
                   jax.ShapeDtypeStruct((t, LANES), F32)],
        grid=(t // tm,),
        in_specs=[row(D_MODEL), row(NSA_WIDTH), row(SSD_WIDTH), full(wo), full(fg), full(rw), full(rb)],
        out_specs=[row(D_MODEL), row(D_MODEL), row(LANES)],
        compiler_params=_cparams("parallel"),
        name="out_proj_router",
    )(x2, o_nsa, o_ssd, wo, fg, rw, rb)


def _moe_kernel(hn_ref, comb_ref, h_ref, wgu_ref, bgu_ref, wd_ref, bd_ref, fn_ref, o_ref, acc_ref):
    e = pl.program_id(1)

    @pl.when(e == 0)
    def _():
        acc_ref[...] = h_ref[...]

    gu = jnp.dot(hn_ref[...], wgu_ref[0], preferred_element_type=F32) + bgu_ref[0]
    gate = jnp.minimum(gu[:, :D_FF], SWIGLU_LIMIT)
    up = jnp.clip(gu[:, D_FF:], -SWIGLU_LIMIT, SWIGLU_LIMIT)
    act = (up + 1.0) * gate / (1.0 + jnp.exp(-SWIGLU_ALPHA * gate))
    y = jnp.dot(act.astype(BF16), wd_ref[0], preferred_element_type=F32) + bd_ref[0]
    comb = comb_ref[...]
    lane = lax.broadcasted_iota(jnp.int32, comb.shape, 1)
    ce = jnp.sum(jnp.where(lane == e, comb, 0.0), axis=-1, keepdims=True)
    acc_ref[...] += ce * y

    @pl.when(e == pl.num_programs(1) - 1)
    def _():
        h = acc_ref[...]
        ms = jnp.mean(h * h, axis=-1, keepdims=True)
        o_ref[...] = h * lax.rsqrt(ms + EPS) * fn_ref[...]


def _moe(hn, comb, h1, w_gate_up, b_gate_up, w_down, b_down, final_norm, tm=512):
    t = hn.shape[0]
    wgu = w_gate_up.astype(BF16)
    wd = w_down.astype(BF16)
    bgu = b_gate_up.reshape(N_EXPERTS, 1, 2 * D_FF)
    bd = b_down.reshape(N_EXPERTS, 1, D_MODEL)
    fn = final_norm.reshape(1, D_MODEL)
    row = lambda w: pl.BlockSpec((tm, w), lambda i, e: (i, 0))
    exp = lambda a: pl.BlockSpec((1,) + a.shape[1:], lambda i, e: (e, 0, 0))
    return pl.pallas_call(
        _moe_kernel,
        out_shape=jax.ShapeDtypeStruct((t, D_MODEL), F32),
        grid=(t // tm, N_EXPERTS),
        in_specs=[row(D_MODEL), row(LANES), row(D_MODEL), exp(wgu), exp(bgu), exp(wd), exp(bd),
                  pl.BlockSpec(fn.shape, lambda i, e: (0, 0))],
        out_specs=row(D_MODEL),
        scratch_shapes=[pltpu.VMEM((tm, D_MODEL), F32)],
        compiler_params=_cparams("parallel", "arbitrary"),
        name="moe_experts",
    )(hn, comb, h1, wgu, bgu, wd, bd, fn)


def kernel(x, attn_norm, w_in, rel_bias, cmp_pos, cmp_w1, cmp_w2, attn_out_norm, conv_w, conv_b,
           dt_bias, a_log, d_skip, ssm_out_norm, w_out, ffn_norm, router_w, router_b,
           w_gate_up, b_gate_up, w_down, b_down, final_norm):
    b, s, d = x.shape
    t = b * s
    depth = w_in.shape[0]
    tables = _bias_tables(rel_bias, s) + _sel_tables(s)
    h = x.reshape(t, d)
    for l in range(depth):
        (q, kc_raw, vc_raw, ks, vs, kw, vw, gates, z, xbc, dt) = _in_proj(h, attn_norm[l], _pad_w_in(w_in[l]))
        grp = CMP_STRIDE * HKV * DH
        kc, vc = _compress(kc_raw.reshape(b, s // CMP_STRIDE, grp), vc_raw.reshape(b, s // CMP_STRIDE, grp),
                           _compress_weights(cmp_pos[l], cmp_w1[l], cmp_w2[l]))
        seq = lambda a: a.reshape(b, s, a.shape[-1])
        o_nsa = _nsa(seq(q), kc, vc, seq(ks), seq(vs), seq(kw), seq(vw), seq(gates), tables, attn_out_norm[l])
        o_ssd = _ssd(seq(xbc), seq(z), seq(dt), conv_w[l], conv_b[l], dt_bias[l], a_log[l], d_skip[l],
                     ssm_out_norm[l])
        h1, hn, comb = _out_proj(h, o_nsa.reshape(t, NSA_WIDTH), o_ssd.reshape(t, SSD_WIDTH), w_out[l],
                                 ffn_norm[l], router_w[l], router_b[l])
        assert depth == 1
        h = _moe(hn, comb, h1, w_gate_up[l], b_gate_up[l], w_down[l], b_down[l], final_norm)
    return h.reshape(b, s, d)
```

```python
import functools
import math

import numpy as np
import jax
import jax.numpy as jnp
from jax import lax
from jax.experimental import pallas as pl
from jax.experimental.pallas import tpu as pltpu

F32 = jnp.float32
BF16 = jnp.bfloat16

D_MODEL = 1024
NSA_HEADS = 8
HKV = 2
GQA = NSA_HEADS // HKV
DH = 64
NSA_WIDTH = NSA_HEADS * DH
CMP_BLOCK = 32
CMP_STRIDE = 16
SEL_BLOCK = 64
SEL_TOP = 16
WINDOW = 512
TQ = 128
SSD_HEADS = 8
SSD_P = 64
SSD_WIDTH = SSD_HEADS * SSD_P
SSD_GROUPS = 2
SSD_N = 128
SSD_CONV = 4
SSD_L = 128
SSD_CONV_DIM = SSD_WIDTH + 2 * SSD_GROUPS * SSD_N
N_BUCKETS = 32
MAX_DISTANCE = 128
N_EXPERTS = 32
TOP_K = 4
D_FF = 1024
SWIGLU_LIMIT = 7.0
SWIGLU_ALPHA = 1.702

EPS = 1e-6
NEG = -1e30
FORCED_SCORE = 1e4
LOG2E = 1.4426950408889634
LANES = 128
VMEM_LIMIT = 56 * 1024 * 1024


def _cparams(*sem):
    return pltpu.CompilerParams(dimension_semantics=sem, vmem_limit_bytes=VMEM_LIMIT)


def _silu(v):
    return v / (1.0 + jnp.exp(-v))


_Q0, _Q1 = 0, NSA_HEADS * LANES
_KV0 = _Q1
_G0 = _KV0 + 6 * LANES
_Z0 = _G0 + LANES
_X0 = _Z0 + SSD_WIDTH
_DT0 = _X0 + SSD_CONV_DIM
_WCOLS = _DT0 + LANES


def _pad_w_in(w_in):
    d = w_in.shape[0]
    nsa_cols = NSA_WIDTH + 6 * HKV * DH + 3 * NSA_HEADS
    wq = w_in[:, :NSA_WIDTH].reshape(d, HKV, GQA, DH)
    zq = jnp.zeros_like(wq)
    q0 = jnp.concatenate([wq[:, 0], zq[:, 0]], axis=-1)
    q1 = jnp.concatenate([zq[:, 1], wq[:, 1]], axis=-1)
    wq_pad = jnp.stack([q0, q1], axis=1).reshape(d, NSA_HEADS * LANES)
    wkv = w_in[:, NSA_WIDTH:NSA_WIDTH + 6 * HKV * DH]
    wg = w_in[:, NSA_WIDTH + 6 * HKV * DH:nsa_cols]
    wg = jnp.pad(wg, ((0, 0), (0, LANES - wg.shape[1])))
    wz = w_in[:, nsa_cols:nsa_cols + SSD_WIDTH]
    wx = w_in[:, nsa_cols + SSD_WIDTH:nsa_cols + SSD_WIDTH + SSD_CONV_DIM]
    wdt = w_in[:, nsa_cols + SSD_WIDTH + SSD_CONV_DIM:]
    wdt = jnp.pad(wdt, ((0, 0), (0, LANES - wdt.shape[1])))
    return jnp.concatenate([wq_pad, wkv, wg, wz, wx, wdt], axis=1).astype(BF16)


def _in_proj_kernel(x_ref, g_ref, w_ref, q_ref, kc_ref, vc_ref, ks_ref, vs_ref, kw_ref, vw_ref,
                    gate_ref, z_ref, xbc_ref, dt_ref):
    x = x_ref[...]
    ms = jnp.mean(x * x, axis=-1, keepdims=True)
    xn = (x * lax.rsqrt(ms + EPS) * g_ref[...]).astype(BF16)

    def seg(lo, hi):
        return jnp.dot(xn, w_ref[:, lo:hi], preferred_element_type=F32)

    q_ref[...] = (seg(_Q0, _Q1) * (DH ** -0.5 * LOG2E)).astype(BF16)
    for j, ref in enumerate((kc_ref, vc_ref, ks_ref, vs_ref, kw_ref, vw_ref)):
        ref[...] = seg(_KV0 + j * LANES, _KV0 + (j + 1) * LANES).astype(BF16)
    gate_ref[...] = 1.0 / (1.0 + jnp.exp(-seg(_G0, _Z0)))
    z_ref[...] = seg(_Z0, _X0)
    xbc_ref[...] = seg(_X0, _DT0)
    dt_ref[...] = seg(_DT0, _WCOLS)


def _in_proj(x2, attn_norm, w_pad, tm=512):
    t = x2.shape[0]
    row = lambda w: pl.BlockSpec((tm, w), lambda i: (i, 0))
    full = lambda a: pl.BlockSpec(a.shape, lambda i: (0,) * a.ndim)
    g = attn_norm.reshape(1, D_MODEL)
    outs = ([jax.ShapeDtypeStruct((t, NSA_HEADS * LANES), BF16)]
            + [jax.ShapeDtypeStruct((t, LANES), BF16)] * 6
            + [jax.ShapeDtypeStruct((t, LANES), F32),
               jax.ShapeDtypeStruct((t, SSD_WIDTH), F32),
               jax.ShapeDtypeStruct((t, SSD_CONV_DIM), F32),
               jax.ShapeDtypeStruct((t, LANES), F32)])
    return pl.pallas_call(
        _in_proj_kernel,
        out_shape=outs,
        grid=(t // tm,),
        in_specs=[row(D_MODEL), full(g), full(w_pad)],
        out_specs=[row(s.shape[1]) for s in outs],
        compiler_params=_cparams("parallel"),
        name="in_proj",
    )(x2, g, w_pad)


def _compress_weights(cmp_pos, cmp_w1, cmp_w2):
    half = CMP_BLOCK // 2
    eye = jnp.eye(HKV, dtype=F32)
    w1 = cmp_w1.reshape(2, CMP_BLOCK, DH, DH)
    w1big = jnp.einsum('jlde,hk->jlhdke', w1, eye)
    w1lo = w1big[:, :half].reshape(2, half * HKV * DH, HKV * DH).astype(BF16)
    w1hi = w1big[:, half:].reshape(2, half * HKV * DH, HKV * DH).astype(BF16)
    pos = jnp.broadcast_to(cmp_pos[:, :, None, :], (2, CMP_BLOCK, HKV, DH))
    poslo = pos[:, :half].reshape(2, 1, half * HKV * DH)
    poshi = pos[:, half:].reshape(2, 1, half * HKV * DH)
    w2big = jnp.einsum('jde,hk->jhdke', cmp_w2, eye).reshape(2, HKV * DH, HKV * DH).astype(BF16)
    return w1lo, w1hi, poslo, poshi, w2big


def _compress_kernel(kr_ref, vr_ref, w1lo_ref, w1hi_ref, poslo_ref, poshi_ref, w2_ref, kc_ref, vc_ref):
    for j, (src, dst) in enumerate(((kr_ref, kc_ref), (vr_ref, vc_ref))):
        r = src[0].astype(F32)
        a = jnp.dot((r + poslo_ref[j]).astype(BF16), w1lo_ref[j], preferred_element_type=F32)
        b = jnp.dot((r + poshi_ref[j]).astype(BF16), w1hi_ref[j], preferred_element_type=F32)
        hid = a + pltpu.roll(b, b.shape[0] - 1, 0)
        dst[0] = jnp.dot(_silu(hid).astype(BF16), w2_ref[j], preferred_element_type=F32).astype(BF16)


def _compress(kr, vr, cw):
    b, ng, width = kr.shape
    w1lo, w1hi, poslo, poshi, w2big = cw
    full = lambda a: pl.BlockSpec(a.shape, lambda i: (0,) * a.ndim)
    bspec = pl.BlockSpec((1, ng, width), lambda i: (i, 0, 0))
    ospec = pl.BlockSpec((1, ng, HKV * DH), lambda i: (i, 0, 0))
    out = jax.ShapeDtypeStruct((b, ng, HKV * DH), BF16)
    return pl.pallas_call(
        _compress_kernel,
        out_shape=[out, out],
        grid=(b,),
        in_specs=[bspec, bspec, full(w1lo), full(w1hi), full(poslo), full(poshi), full(w2big)],
        out_specs=[ospec, ospec],
        compiler_params=_cparams("parallel"),
        name="nsa_compress",
    )(kr, vr, w1lo, w1hi, poslo, poshi, w2big)


def _bucket_thresholds():
    d = np.arange(MAX_DISTANCE + 1)
    max_exact = N_BUCKETS // 2
    nf = np.maximum(d, max_exact).astype(np.float32)
    large = max_exact + (np.log(nf / np.float32(max_exact)) / np.float32(math.log(MAX_DISTANCE / max_exact))
                         * np.float32(N_BUCKETS - max_exact)).astype(np.int32)
    bucket = np.where(d < max_exact, d, np.minimum(large, N_BUCKETS - 1))
    assert np.all(np.diff(bucket) >= 0) and bucket[MAX_DISTANCE] == N_BUCKETS - 1
    return [int(np.argmax(bucket >= k)) for k in range(N_BUCKETS)]


def _bias_kernel(rb_ref, bc_ref, bw_ref, bs_ref, *, n_cmp):
    i = pl.program_id(0)
    thr = _bucket_thresholds()

    def table(dist, valid, hd, shift):
        v = jnp.full(dist.shape, rb_ref[0, hd], F32)
        for k in range(1, N_BUCKETS):
            v = jnp.where(dist >= thr[k], rb_ref[k, hd], v)
        return jnp.where(valid, (v - shift) * LOG2E, NEG)

    row = lax.broadcasted_iota(jnp.int32, (TQ, LANES), 0)
    col = lax.broadcasted_iota(jnp.int32, (TQ, LANES), 1)
    dist_c = i * TQ + row - (col * CMP_STRIDE + CMP_BLOCK - 1)
    valid_c = (dist_c >= 0) & (col < n_cmp)
    for hd in range(NSA_HEADS):
        bc_ref[hd // GQA, hd % GQA] = table(dist_c, valid_c, hd, 0.0)

    @pl.when(i == 0)
    def _():
        band = WINDOW + TQ
        qi_w = lax.broadcasted_iota(jnp.int32, (TQ, band), 0)
        dist_w = qi_w + WINDOW - lax.broadcasted_iota(jnp.int32, (TQ, band), 1)
        valid_w = (dist_w >= 0) & (dist_w < WINDOW)
        qi_s = lax.broadcasted_iota(jnp.int32, (TQ, 2 * TQ), 0)
        dist_s = qi_s + TQ - lax.broadcasted_iota(jnp.int32, (TQ, 2 * TQ), 1)
        for hd in range(NSA_HEADS):
            k, g = hd // GQA, hd % GQA
            bw_ref[k, g * TQ:(g + 1) * TQ, :] = table(dist_w, valid_w, hd, 0.0)
            bs_ref[k, g * TQ:(g + 1) * TQ, :] = table(dist_s, dist_s >= 0, hd, rb_ref[N_BUCKETS - 1, hd])


def _bias_tables(rel_bias, s):
    n_cmp = (s - CMP_BLOCK) // CMP_STRIDE + 1
    assert n_cmp < LANES and TQ >= MAX_DISTANCE
    band = WINDOW + TQ
    return pl.pallas_call(
        functools.partial(_bias_kernel, n_cmp=n_cmp),
        out_shape=[jax.ShapeDtypeStruct((HKV, GQA, s, LANES), F32),
                   jax.ShapeDtypeStruct((HKV, GQA * TQ, band), F32),
                   jax.ShapeDtypeStruct((HKV, GQA * TQ, 2 * TQ), F32)],
        grid=(s // TQ,),
        in_specs=[pl.BlockSpec(memory_space=pltpu.SMEM)],
        out_specs=[pl.BlockSpec((HKV, GQA, TQ, LANES), lambda i: (0, 0, i, 0)),
                   pl.BlockSpec((HKV, GQA * TQ, band), lambda i: (0, 0, 0)),
                   pl.BlockSpec((HKV, GQA * TQ, 2 * TQ), lambda i: (0, 0, 0))],
        compiler_params=_cparams("arbitrary"),
        name="nsa_bias_tables",
    )(rel_bias)


def _sel_tables(s):
    n_cmp = (s - CMP_BLOCK) // CMP_STRIDE + 1
    n_sel = s // SEL_BLOCK
    c_start = np.arange(LANES) * CMP_STRIDE
    s_start = np.arange(n_sel) * SEL_BLOCK
    ovl = ((c_start[None, :] < s_start[:, None] + SEL_BLOCK)
           & (c_start[None, :] + CMP_BLOCK > s_start[:, None])
           & (np.arange(LANES)[None, :] < n_cmp)).astype(np.float32)
    key_blk = np.arange(s) // SEL_BLOCK
    expand = np.where(key_blk[None, :] == np.arange(LANES)[:, None], NEG, 0.0).astype(np.float32)
    expand = expand.reshape(LANES, s // TQ, TQ).transpose(1, 0, 2)
    return jnp.asarray(ovl, BF16), jnp.asarray(expand, BF16)


def _nsa_kernel(q_ref, kc_ref, vc_ref, ks_ref, vs_ref, kw_ref, vw_ref, gate_ref,
                bc_ref, bw_ref, bs_ref, ovl_ref, exp_ref, gn_ref, o_ref,
                qs_ref, s_ref, m_ref, l_ref, acc_ref, uns_ref, oc_ref, os_ref, ob_ref):
    i = pl.program_id(1)
    rows = GQA * TQ
    n_sel = ovl_ref.shape[0]
    n_band = WINDOW // TQ + 1
    heads = range(HKV)
    nt_dims = (((1,), (1,)), ((), ()))

    for hd in range(NSA_HEADS):
        qs_ref[hd // GQA, (hd % GQA) * TQ:(hd % GQA + 1) * TQ, :] = q_ref[0, :, hd * LANES:(hd + 1) * LANES]

    def qs(h):
        return qs_ref[h]

    def chunk_rows(c):
        return pl.ds(pl.multiple_of(c * TQ, TQ), TQ)

    def paired_loop(n, body):
        def pair(j, carry):
            body(2 * j)
            body(2 * j + 1)
            return carry

        lax.fori_loop(0, n // 2, pair, 0)

        @pl.when(n % 2 == 1)
        def _():
            body(n - 1)

    def row_max_to_lanes():
        for h in heads:
            m_ref[h] = jnp.broadcast_to(jnp.max(m_ref[h], axis=-1, keepdims=True), (rows, TQ))

    def normalise(acc, lsum):
        return acc / jnp.maximum(jnp.sum(lsum, axis=-1, keepdims=True), 1e-30)

    for h in heads:
        bias_c = bc_ref[h].reshape(rows, LANES)
        sc = lax.dot_general(qs(h), kc_ref[0], nt_dims, preferred_element_type=F32) + bias_c
        mc = jnp.max(sc, axis=-1, keepdims=True)
        pc = jnp.where(bias_c > 0.5 * NEG, jnp.exp2(sc - mc), 0.0)
        pc = pc / jnp.maximum(jnp.sum(pc, axis=-1, keepdims=True), 1e-30)
        oc_ref[h] = jnp.dot(pc.astype(BF16), vc_ref[0], preferred_element_type=F32)

        psum = jnp.sum(pc.reshape(GQA, TQ, LANES), axis=0)
        p_hi = psum.astype(BF16)
        p_lo = (psum - p_hi.astype(F32)).astype(BF16)
        imp = (lax.dot_general(ovl_ref[...], p_hi, nt_dims, preferred_element_type=F32)
               + lax.dot_general(ovl_ref[...], p_lo, nt_dims, preferred_element_type=F32))
        blk = lax.broadcasted_iota(jnp.int32, (n_sel, TQ), 0)
        tok = lax.broadcasted_iota(jnp.int32, (n_sel, TQ), 1) + i * TQ
        blk_of_t = tok // SEL_BLOCK
        forced = (blk == 0) | (blk == blk_of_t) | (blk == blk_of_t - 1)
        score = jnp.where(forced, FORCED_SCORE, jnp.where(blk <= blk_of_t, imp, -1.0))
        rank = jnp.zeros((n_sel, TQ), F32)
        for mm in range(n_sel):
            sm = score[mm:mm + 1, :]
            ahead = (sm > score) | ((sm == score) & (blk > mm))
            rank = rank + jnp.where(ahead, 1.0, 0.0)
        unsel_t = jnp.where(rank < min(SEL_TOP, n_sel), 0.0, 1.0)
        unsel_t = jnp.concatenate([unsel_t, jnp.zeros((LANES - n_sel, TQ), F32)], axis=0).astype(BF16)
        eye = (lax.broadcasted_iota(jnp.int32, (TQ, TQ), 0)
               == lax.broadcasted_iota(jnp.int32, (TQ, TQ), 1)).astype(BF16)
        uns_ref[h] = lax.dot_general(eye, unsel_t, nt_dims, preferred_element_type=F32).astype(BF16)

    def sel_scores(c, bias_cols):
        k = ks_ref[0, chunk_rows(c), :]
        for h in heads:
            madd = jnp.dot(uns_ref[h], exp_ref[c], preferred_element_type=F32)
            s = lax.dot_general(qs(h), k, nt_dims, preferred_element_type=F32)
            s = (s.reshape(GQA, TQ, TQ) + madd[None]).reshape(rows, TQ)
            if bias_cols is not None:
                s = s + bs_ref[h, :, bias_cols:bias_cols + TQ]
            s_ref[h, c] = s
            m_ref[h] = jnp.maximum(m_ref[h], s)

    def softmax_pv(c, v):
        for h in heads:
            p = jnp.exp2(s_ref[h, c] - m_ref[h])
            l_ref[h] += p
            acc_ref[h] += jnp.dot(p.astype(BF16), v, preferred_element_type=F32)

    m_ref[...] = jnp.full(m_ref.shape, NEG, F32)
    paired_loop(jnp.maximum(i - 1, 0), lambda c: sel_scores(c, None))

    @pl.when(i >= 1)
    def _():
        sel_scores(i - 1, 0)

    sel_scores(i, TQ)
    row_max_to_lanes()
    l_ref[...] = jnp.zeros(l_ref.shape, F32)
    acc_ref[...] = jnp.zeros(acc_ref.shape, F32)
    paired_loop(i + 1, lambda c: softmax_pv(c, vs_ref[0, chunk_rows(c), :]))
    for h in heads:
        os_ref[h] = normalise(acc_ref[h], l_ref[h])

    m_ref[...] = jnp.full(m_ref.shape, NEG, F32)
    for jj in range(n_band):
        c = i - (n_band - 1) + jj
        k = kw_ref[0, chunk_rows(jnp.maximum(c, 0)), :]
        for h in heads:
            s = lax.dot_general(qs(h), k, nt_dims, preferred_element_type=F32)
            s = s + bw_ref[h, :, jj * TQ:(jj + 1) * TQ] + jnp.where(c >= 0, 0.0, NEG)
            s_ref[h, jj] = s
            m_ref[h] = jnp.maximum(m_ref[h], s)
    row_max_to_lanes()
    o_win = []
    for h in heads:
        acc = lsum = None
        for jj in range(n_band):
            v = vw_ref[0, chunk_rows(jnp.maximum(i - (n_band - 1) + jj, 0)), :]
            p = jnp.exp2(s_ref[h, jj] - m_ref[h])
            pv = jnp.dot(p.astype(BF16), v, preferred_element_type=F32)
            acc = pv if acc is None else acc + pv
            lsum = p if lsum is None else lsum + p
        o_win.append(normalise(acc, lsum))

    gates = gate_ref[0]
    for hd in range(NSA_HEADS):
        h, g = hd // GQA, hd % GQA
        r = slice(g * TQ, (g + 1) * TQ)
        o = (gates[:, hd:hd + 1] * oc_ref[h, r, :]
             + gates[:, NSA_HEADS + hd:NSA_HEADS + hd + 1] * os_ref[h, r, :]
             + gates[:, 2 * NSA_HEADS + hd:2 * NSA_HEADS + hd + 1] * o_win[h][r])
        ob_ref[:, hd * DH:(hd + 1) * DH] = o[:, h * DH:(h + 1) * DH]

    o = ob_ref[...]
    ms = jnp.mean(o * o, axis=-1, keepdims=True)
    o_ref[0] = (o * lax.rsqrt(ms + EPS) * gn_ref[...]).astype(BF16)


def _nsa(q, kc, vc, ks, vs, kw, vw, gates, tables, attn_out_norm):
    b, s, _ = q.shape
    bias_c, bias_w, bias_s, ovl, expand = tables
    gn = attn_out_norm.reshape(1, NSA_WIDTH)
    full = lambda a: pl.BlockSpec(a.shape, lambda bi, i: (0,) * a.ndim)
    seq = pl.BlockSpec((1, s, LANES), lambda bi, i: (bi, 0, 0))
    cmp = pl.BlockSpec((1,) + kc.shape[1:], lambda bi, i: (bi, 0, 0))
    rows = GQA * TQ
    return pl.pallas_call(
        _nsa_kernel,
        out_shape=jax.ShapeDtypeStruct((b, s, NSA_WIDTH), BF16),
        grid=(b, s // TQ),
        in_specs=[pl.BlockSpec((1, TQ, NSA_HEADS * LANES), lambda bi, i: (bi, i, 0)),
                  cmp, cmp, seq, seq, seq, seq,
                  pl.BlockSpec((1, TQ, LANES), lambda bi, i: (bi, i, 0)),
                  pl.BlockSpec((HKV, GQA, TQ, LANES), lambda bi, i: (0, 0, i, 0)),
                  full(bias_w), full(bias_s), full(ovl), full(expand), full(gn)],
        out_specs=pl.BlockSpec((1, TQ, NSA_WIDTH), lambda bi, i: (bi, i, 0)),
        scratch_shapes=[pltpu.VMEM((HKV, rows, LANES), BF16),
                        pltpu.VMEM((HKV, s // TQ, rows, TQ), F32),
                        pltpu.VMEM((HKV, rows, TQ), F32),
                        pltpu.VMEM((HKV, rows, TQ), F32),
                        pltpu.VMEM((HKV, rows, LANES), F32),
                        pltpu.VMEM((HKV, TQ, LANES), BF16),
                        pltpu.VMEM((HKV, rows, LANES), F32),
                        pltpu.VMEM((HKV, rows, LANES), F32),
                        pltpu.VMEM((TQ, NSA_WIDTH), F32)],
        compiler_params=_cparams("parallel", "arbitrary"),
        name="nsa_attention",
    )(q, kc, vc, ks, vs, kw, vw, gates, bias_c, bias_w, bias_s, ovl, expand, gn)


def _ssd_kernel(xbc_ref, z_ref, dt_ref, cw_ref, cb_ref, dtb_ref, alog_ref, dsk_ref, ng_ref, o_ref,
                xbuf_ref, state_ref, y_ref):
    c = pl.program_id(1)
    L, P, N = SSD_L, SSD_P, SSD_N
    hpg = SSD_HEADS // SSD_GROUPS
    pad = 8

    @pl.when(c == 0)
    def _():
        xbuf_ref[0:pad, :] = jnp.zeros((pad, SSD_CONV_DIM), F32)
        state_ref[...] = jnp.zeros(state_ref.shape, F32)

    xbuf_ref[pad:pad + L, :] = xbc_ref[0]
    conv = cb_ref[...]
    for k in range(SSD_CONV):
        shift = SSD_CONV - 1 - k
        conv = conv + xbuf_ref[pad - shift:pad - shift + L, :] * cw_ref[k:k + 1, :]
    xbuf_ref[0:pad, :] = xbuf_ref[L:L + pad, :]
    xa = _silu(conv)
    xs = xa[:, :SSD_WIDTH]
    bm = xa[:, SSD_WIDTH:SSD_WIDTH + SSD_GROUPS * N]
    cm = xa[:, SSD_WIDTH + SSD_GROUPS * N:]

    dtv = dt_ref[0] + dtb_ref[...]
    dt = jnp.maximum(dtv, 0.0) + jnp.log1p(jnp.exp(-jnp.abs(dtv)))
    a = dt * (-jnp.exp(alog_ref[...]))
    ri = lax.broadcasted_iota(jnp.int32, (L, L), 0)
    ci = lax.broadcasted_iota(jnp.int32, (L, L), 1)
    causal = ri >= ci
    cs = jnp.dot(causal.astype(F32), a, preferred_element_type=F32,
                 precision=lax.Precision.HIGHEST)
    cs_t = cs.T
    nt_dims = (((1,), (1,)), ((), ()))

    for gr in range(SSD_GROUPS):
        b_g = bm[:, gr * N:(gr + 1) * N]
        c_g = cm[:, gr * N:(gr + 1) * N]
        scores = lax.dot_general(c_g.astype(BF16), b_g.astype(BF16), nt_dims, preferred_element_type=F32)
        b_gt = b_g.T
        for hh in range(hpg):
            h = gr * hpg + hh
            cs_col = cs[:, h:h + 1]
            cs_row = cs_t[h:h + 1, :]
            cs_last = cs[L - 1:L, h:h + 1]
            decay = jnp.exp(jnp.where(causal, cs_col - cs_row, NEG))
            xs_h = xs[:, h * P:(h + 1) * P]
            xc = (xs_h * dt[:, h:h + 1]).astype(BF16)
            y = jnp.dot((scores * decay).astype(BF16), xc, preferred_element_type=F32)
            prev = state_ref[h]
            y = y + jnp.dot((c_g * jnp.exp(cs_col)).astype(BF16), prev.astype(BF16),
                            preferred_element_type=F32)
            contrib = jnp.dot((b_gt * jnp.exp(cs_last - cs_row)).astype(BF16), xc,
                              preferred_element_type=F32)
            state_ref[h] = jnp.exp(cs_last) * prev + contrib
            y_ref[:, h * P:(h + 1) * P] = y + xs_h * dsk_ref[:, h * P:(h + 1) * P]

    z = z_ref[0]
    y = y_ref[...] * _silu(z)
    gw = SSD_WIDTH // SSD_GROUPS
    for gr in range(SSD_GROUPS):
        yg = y[:, gr * gw:(gr + 1) * gw]
        ms = jnp.mean(yg * yg, axis=-1, keepdims=True)
        o_ref[0, :, gr * gw:(gr + 1) * gw] = (yg * lax.rsqrt(ms + EPS)
                                              * ng_ref[:, gr * gw:(gr + 1) * gw]).astype(BF16)


def _ssd(xbc, z, dt, conv_w, conv_b, dt_bias, a_log, d_skip, norm_g):
    b, s, _ = xbc.shape
    padl = lambda v: jnp.pad(v, (0, LANES - v.shape[0])).reshape(1, LANES)
    args = (conv_w, conv_b.reshape(1, SSD_CONV_DIM), padl(dt_bias), padl(a_log),
            jnp.repeat(d_skip, SSD_P).reshape(1, SSD_WIDTH), norm_g.reshape(1, SSD_WIDTH))
    full = lambda a: pl.BlockSpec(a.shape, lambda bi, c: (0,) * a.ndim)
    blk = lambda w: pl.BlockSpec((1, SSD_L, w), lambda bi, c: (bi, c, 0))
    return pl.pallas_call(
        _ssd_kernel,
        out_shape=jax.ShapeDtypeStruct((b, s, SSD_WIDTH), BF16),
        grid=(b, s // SSD_L),
        in_specs=[blk(SSD_CONV_DIM), blk(SSD_WIDTH), blk(LANES)] + [full(a) for a in args],
        out_specs=blk(SSD_WIDTH),
        scratch_shapes=[pltpu.VMEM((SSD_L + 8, SSD_CONV_DIM), F32),
                        pltpu.VMEM((SSD_HEADS, SSD_N, SSD_P), F32),
                        pltpu.VMEM((SSD_L, SSD_WIDTH), F32)],
        compiler_params=_cparams("parallel", "arbitrary"),
        name="ssd_mixer",
    )(xbc, z, dt, *args)


def _out_proj_kernel(x_ref, on_ref, os_ref, wo_ref, fg_ref, rw_ref, rb_ref, h_ref, hn_ref, comb_ref):
    h = (x_ref[...]
         + jnp.dot(on_ref[...], wo_ref[0:NSA_WIDTH, :], preferred_element_type=F32)
         + jnp.dot(os_ref[...], wo_ref[NSA_WIDTH:, :], preferred_element_type=F32))
    h_ref[...] = h
    ms = jnp.mean(h * h, axis=-1, keepdims=True)
    hn = h * lax.rsqrt(ms + EPS) * fg_ref[...]
    hn_ref[...] = hn.astype(BF16)
    logits = jnp.dot(hn, rw_ref[...], preferred_element_type=F32,
                     precision=lax.Precision.HIGHEST) + rb_ref[...]
    lane = lax.broadcasted_iota(jnp.int32, logits.shape, 1)
    work = logits
    picks = []
    for _ in range(TOP_K):
        v = jnp.max(work, axis=-1, keepdims=True)
        idx = jnp.min(jnp.where(work == v, lane, LANES), axis=-1, keepdims=True)
        hit = lane == idx
        picks.append((v, hit))
        work = jnp.where(hit, -3e38, work)
    v0 = picks[0][0]
    es = [jnp.exp(v - v0) for v, _ in picks]
    den = es[0] + es[1] + es[2] + es[3]
    comb = jnp.zeros_like(logits)
    for e, (_, hit) in zip(es, picks):
        comb = comb + jnp.where(hit, e / den, 0.0)
    comb_ref[...] = comb


def _out_proj(x2, o_nsa, o_ssd, w_out, ffn_norm, router_w, router_b, tm=512):
    t = x2.shape[0]
    row = lambda w: pl.BlockSpec((tm, w), lambda i: (i, 0))
    full = lambda a: pl.BlockSpec(a.shape, lambda i: (0,) * a.ndim)
    wo = w_out.astype(BF16)
    fg = ffn_norm.reshape(1, D_MODEL)
    rw = jnp.pad(router_w, ((0, 0), (0, LANES - N_EXPERTS)))
    rb = jnp.pad(router_b, (0, LANES - N_EXPERTS), constant_values=NEG).reshape(1, LANES)
    return pl.pallas_call(
        _out_proj_kernel,
        out_shape=[jax.ShapeDtypeStruct((t, D_MODEL), F32),
                   jax.ShapeDtypeStruct((t, D_MODEL), BF16),
                   jax.ShapeDtypeStruct((t, LANES), F32)],
        grid=(t // tm,),
        in_specs=[row(D_MODEL), row(NSA_WIDTH), row(SSD_WIDTH), full(wo), full(fg), full(rw), full(rb)],
        out_specs=[row(D_MODEL), row(D_MODEL), row(LANES)],
        compiler_params=_cparams("parallel"),
        name="out_proj_router",
    )(x2, o_nsa, o_ssd, wo, fg, rw, rb)


def _moe_kernel(hn_ref, comb_ref, h_ref, wgu_ref, bgu_ref, wd_ref, bd_ref, fn_ref, o_ref, acc_ref):
    e = pl.program_id(1)

    @pl.when(e == 0)
    def _():
        acc_ref[...] = h_ref[...]

    gu = jnp.dot(hn_ref[...], wgu_ref[0], preferred_element_type=F32) + bgu_ref[0]
    gate = jnp.minimum(gu[:, :D_FF], SWIGLU_LIMIT)
    up = jnp.clip(gu[:, D_FF:], -SWIGLU_LIMIT, SWIGLU_LIMIT)
    act = (up + 1.0) * gate / (1.0 + jnp.exp(-SWIGLU_ALPHA * gate))
    y = jnp.dot(act.astype(BF16), wd_ref[0], preferred_element_type=F32) + bd_ref[0]
    comb = comb_ref[...]
    lane = lax.broadcasted_iota(jnp.int32, comb.shape, 1)
    ce = jnp.sum(jnp.where(lane == e, comb, 0.0), axis=-1, keepdims=True)
    acc_ref[...] += ce * y

    @pl.when(e == pl.num_programs(1) - 1)
    def _():
        h = acc_ref[...]
        ms = jnp.mean(h * h, axis=-1, keepdims=True)
        o_ref[...] = h * lax.rsqrt(ms + EPS) * fn_ref[...]


def _moe(hn, comb, h1, w_gate_up, b_gate_up, w_down, b_down, final_norm, tm=512):
    t = hn.shape[0]
    wgu = w_gate_up.astype(BF16)
    wd = w_down.astype(BF16)
    bgu = b_gate_up.reshape(N_EXPERTS, 1, 2 * D_FF)
    bd = b_down.reshape(N_EXPERTS, 1, D_MODEL)
    fn = final_norm.reshape(1, D_MODEL)
    row = lambda w: pl.BlockSpec((tm, w), lambda i, e: (i, 0))
    exp = lambda a: pl.BlockSpec((1,) + a.shape[1:], lambda i, e: (e, 0, 0))
    return pl.pallas_call(
        _moe_kernel,
        out_shape=jax.ShapeDtypeStruct((t, D_MODEL), F32),
        grid=(t // tm, N_EXPERTS),
        in_specs=[row(D_MODEL), row(LANES), row(D_MODEL), exp(wgu), exp(bgu), exp(wd), exp(bd),
                  pl.BlockSpec(fn.shape, lambda i, e: (0, 0))],
        out_specs=row(D_MODEL),
        scratch_shapes=[pltpu.VMEM((tm, D_MODEL), F32)],
        compiler_params=_cparams("parallel", "arbitrary"),
        name="moe_experts",
    )(hn, comb, h1, wgu, bgu, wd, bd, fn)


def kernel(x, attn_norm, w_in, rel_bias, cmp_pos, cmp_w1, cmp_w2, attn_out_norm, conv_w, conv_b,
           dt_bias, a_log, d_skip, ssm_out_norm, w_out, ffn_norm, router_w, router_b,
           w_gate_up, b_gate_up, w_down, b_down, final_norm):
    b, s, d = x.shape
    t = b * s
    depth = w_in.shape[0]
    tables = tuple(_bias_tables(rel_bias, s)) + _sel_tables(s)
    h = x.reshape(t, d)
    for l in range(depth):
        (q, kc_raw, vc_raw, ks, vs, kw, vw, gates, z, xbc, dt) = _in_proj(h, attn_norm[l], _pad_w_in(w_in[l]))
        grp = CMP_STRIDE * HKV * DH
        kc, vc = _compress(kc_raw.reshape(b, s // CMP_STRIDE, grp), vc_raw.reshape(b, s // CMP_STRIDE, grp),
                           _compress_weights(cmp_pos[l], cmp_w1[l], cmp_w2[l]))
        seq = lambda a: a.reshape(b, s, a.shape[-1])
        o_nsa = _nsa(seq(q), kc, vc, seq(ks), seq(vs), seq(kw), seq(vw), seq(gates), tables, attn_out_norm[l])
        o_ssd = _ssd(seq(xbc), seq(z), seq(dt), conv_w[l], conv_b[l], dt_bias[l], a_log[l], d_skip[l],
                     ssm_out_norm[l])
        h1, hn, comb = _out_proj(h, o_nsa.reshape(t, NSA_WIDTH), o_ssd.reshape(t, SSD_WIDTH), w_out[l],
                                 ffn_norm[l], router_w[l], router_b[l])
        assert depth == 1
        h = _moe(hn, comb, h1, w_gate_up[l], b_gate_up[l], w_down[l], b_down[l], final_norm)
    return h.reshape(b, s, d)
```

```python
import functools
import math

import numpy as np
import jax
import jax.numpy as jnp
from jax import lax
from jax.experimental import pallas as pl
from jax.experimental.pallas import tpu as pltpu

F32 = jnp.float32
BF16 = jnp.bfloat16

D_MODEL = 1024
NSA_HEADS = 8
HKV = 2
GQA = NSA_HEADS // HKV
DH = 64
NSA_WIDTH = NSA_HEADS * DH
CMP_BLOCK = 32
CMP_STRIDE = 16
SEL_BLOCK = 64
SEL_TOP = 16
WINDOW = 512
TQ = 128
SSD_HEADS = 8
SSD_P = 64
SSD_WIDTH = SSD_HEADS * SSD_P
SSD_GROUPS = 2
SSD_N = 128
SSD_CONV = 4
SSD_L = 128
SSD_CONV_DIM = SSD_WIDTH + 2 * SSD_GROUPS * SSD_N
N_BUCKETS = 32
MAX_DISTANCE = 128
N_EXPERTS = 32
TOP_K = 4
D_FF = 1024
SWIGLU_LIMIT = 7.0
SWIGLU_ALPHA = 1.702

EPS = 1e-6
NEG = -1e30
FORCED_SCORE = 1e4
LOG2E = 1.4426950408889634
LANES = 128
SUBLANES = 8
H_SLAB = D_MODEL // LANES
VMEM_LIMIT = 56 * 1024 * 1024
MOE_VMEM_LIMIT = (2 * 4096 * D_MODEL * 4 + 2 * 3 * D_MODEL * D_FF * 2 + 12 * 1024 * 1024)


def _cparams(*sem, vmem=VMEM_LIMIT):
    return pltpu.CompilerParams(dimension_semantics=sem, vmem_limit_bytes=vmem)


def _silu(v):
    return v / (1.0 + jnp.exp(-v))


_Q0, _Q1 = 0, NSA_HEADS * LANES
_KV0 = _Q1
_G0 = _KV0 + 6 * LANES
_Z0 = _G0 + LANES
_X0 = _Z0 + SSD_WIDTH
_DT0 = _X0 + SSD_CONV_DIM
_WCOLS = _DT0 + LANES


def _pad_w_in(w_in):
    d = w_in.shape[0]
    nsa_cols = NSA_WIDTH + 6 * HKV * DH + 3 * NSA_HEADS
    wq = w_in[:, :NSA_WIDTH].reshape(d, HKV, GQA, DH)
    zq = jnp.zeros_like(wq)
    q0 = jnp.concatenate([wq[:, 0], zq[:, 0]], axis=-1)
    q1 = jnp.concatenate([zq[:, 1], wq[:, 1]], axis=-1)
    wq_pad = jnp.stack([q0, q1], axis=1).reshape(d, NSA_HEADS * LANES)
    wkv = w_in[:, NSA_WIDTH:NSA_WIDTH + 6 * HKV * DH]
    wg = w_in[:, NSA_WIDTH + 6 * HKV * DH:nsa_cols]
    wg = jnp.pad(wg, ((0, 0), (0, LANES - wg.shape[1])))
    wz = w_in[:, nsa_cols:nsa_cols + SSD_WIDTH]
    wx = w_in[:, nsa_cols + SSD_WIDTH:nsa_cols + SSD_WIDTH + SSD_CONV_DIM]
    wdt = w_in[:, nsa_cols + SSD_WIDTH + SSD_CONV_DIM:]
    wdt = jnp.pad(wdt, ((0, 0), (0, LANES - wdt.shape[1])))
    return jnp.concatenate([wq_pad, wkv, wg, wz, wx, wdt], axis=1).astype(BF16)


def _in_proj_kernel(x_ref, g_ref, w_ref, q_ref, kc_ref, vc_ref, ks_ref, vs_ref, kw_ref, vw_ref,
                    gate_ref, z_ref, xbc_ref, dt_ref):
    x = x_ref[...]
    ms = jnp.mean(x * x, axis=-1, keepdims=True)
    xn = (x * lax.rsqrt(ms + EPS) * g_ref[...]).astype(BF16)

    def seg(lo, hi):
        return jnp.dot(xn, w_ref[:, lo:hi], preferred_element_type=F32)

    q_ref[...] = (seg(_Q0, _Q1) * (DH ** -0.5 * LOG2E)).astype(BF16)
    for j, ref in enumerate((kc_ref, vc_ref, ks_ref, vs_ref, kw_ref, vw_ref)):
        ref[...] = seg(_KV0 + j * LANES, _KV0 + (j + 1) * LANES).astype(BF16)
    gate_ref[...] = 1.0 / (1.0 + jnp.exp(-seg(_G0, _Z0)))
    z_ref[...] = seg(_Z0, _X0)
    xbc_ref[...] = seg(_X0, _DT0)
    dt_ref[...] = seg(_DT0, _WCOLS)


def _in_proj(x2, attn_norm, w_pad, tm=512):
    t = x2.shape[0]
    row = lambda w: pl.BlockSpec((tm, w), lambda i: (i, 0))
    full = lambda a: pl.BlockSpec(a.shape, lambda i: (0,) * a.ndim)
    g = attn_norm.reshape(1, D_MODEL)
    outs = ([jax.ShapeDtypeStruct((t, NSA_HEADS * LANES), BF16)]
            + [jax.ShapeDtypeStruct((t, LANES), BF16)] * 6
            + [jax.ShapeDtypeStruct((t, LANES), F32),
               jax.ShapeDtypeStruct((t, SSD_WIDTH), F32),
               jax.ShapeDtypeStruct((t, SSD_CONV_DIM), F32),
               jax.ShapeDtypeStruct((t, LANES), F32)])
    return pl.pallas_call(
        _in_proj_kernel,
        out_shape=outs,
        grid=(t // tm,),
        in_specs=[row(D_MODEL), full(g), full(w_pad)],
        out_specs=[row(s.shape[1]) for s in outs],
        compiler_params=_cparams("parallel"),
        name="in_proj",
    )(x2, g, w_pad)


def _compress_weights(cmp_pos, cmp_w1, cmp_w2):
    half = CMP_BLOCK // 2
    eye = jnp.eye(HKV, dtype=F32)
    w1 = cmp_w1.reshape(2, CMP_BLOCK, DH, DH)
    w1big = jnp.einsum('jlde,hk->jlhdke', w1, eye)
    w1lo = w1big[:, :half].reshape(2, half * HKV * DH, HKV * DH).astype(BF16)
    w1hi = w1big[:, half:].reshape(2, half * HKV * DH, HKV * DH).astype(BF16)
    pos = jnp.broadcast_to(cmp_pos[:, :, None, :], (2, CMP_BLOCK, HKV, DH))
    poslo = pos[:, :half].reshape(2, 1, half * HKV * DH)
    poshi = pos[:, half:].reshape(2, 1, half * HKV * DH)
    w2big = jnp.einsum('jde,hk->jhdke', cmp_w2, eye).reshape(2, HKV * DH, HKV * DH).astype(BF16)
    return w1lo, w1hi, poslo, poshi, w2big


def _compress_kernel(kr_ref, vr_ref, w1lo_ref, w1hi_ref, poslo_ref, poshi_ref, w2_ref, kc_ref, vc_ref):
    for j, (src, dst) in enumerate(((kr_ref, kc_ref), (vr_ref, vc_ref))):
        r = src[0].astype(F32)
        a = jnp.dot((r + poslo_ref[j]).astype(BF16), w1lo_ref[j], preferred_element_type=F32)
        b = jnp.dot((r + poshi_ref[j]).astype(BF16), w1hi_ref[j], preferred_element_type=F32)
        hid = a + pltpu.roll(b, b.shape[0] - 1, 0)
        dst[0] = jnp.dot(_silu(hid).astype(BF16), w2_ref[j], preferred_element_type=F32).astype(BF16)


def _compress(kr, vr, cw):
    b, ng, width = kr.shape
    w1lo, w1hi, poslo, poshi, w2big = cw
    full = lambda a: pl.BlockSpec(a.shape, lambda i: (0,) * a.ndim)
    bspec = pl.BlockSpec((1, ng, width), lambda i: (i, 0, 0))
    ospec = pl.BlockSpec((1, ng, HKV * DH), lambda i: (i, 0, 0))
    out = jax.ShapeDtypeStruct((b, ng, HKV * DH), BF16)
    return pl.pallas_call(
        _compress_kernel,
        out_shape=[out, out],
        grid=(b,),
        in_specs=[bspec, bspec, full(w1lo), full(w1hi), full(poslo), full(poshi), full(w2big)],
        out_specs=[ospec, ospec],
        compiler_params=_cparams("parallel"),
        name="nsa_compress",
    )(kr, vr, w1lo, w1hi, poslo, poshi, w2big)


def _bucket_thresholds():
    d = np.arange(MAX_DISTANCE + 1)
    max_exact = N_BUCKETS // 2
    nf = np.maximum(d, max_exact).astype(np.float32)
    large = max_exact + (np.log(nf / np.float32(max_exact)) / np.float32(math.log(MAX_DISTANCE / max_exact))
                         * np.float32(N_BUCKETS - max_exact)).astype(np.int32)
    bucket = np.where(d < max_exact, d, np.minimum(large, N_BUCKETS - 1))
    assert np.all(np.diff(bucket) >= 0) and bucket[MAX_DISTANCE] == N_BUCKETS - 1
    return [int(np.argmax(bucket >= k)) for k in range(N_BUCKETS)]


def _bias_kernel(rb_ref, bc_ref, bw_ref, bs_ref, *, n_cmp):
    i = pl.program_id(0)
    thr = _bucket_thresholds()

    def table(dist, valid, hd, shift):
        v = jnp.full(dist.shape, rb_ref[0, hd], F32)
        for k in range(1, N_BUCKETS):
            v = jnp.where(dist >= thr[k], rb_ref[k, hd], v)
        return jnp.where(valid, (v - shift) * LOG2E, NEG)

    row = lax.broadcasted_iota(jnp.int32, (TQ, LANES), 0)
    col = lax.broadcasted_iota(jnp.int32, (TQ, LANES), 1)
    dist_c = i * TQ + row - (col * CMP_STRIDE + CMP_BLOCK - 1)
    valid_c = (dist_c >= 0) & (col < n_cmp)
    for hd in range(NSA_HEADS):
        bc_ref[hd // GQA, hd % GQA] = table(dist_c, valid_c, hd, 0.0)

    @pl.when(i == 0)
    def _():
        band = WINDOW + TQ
        qi_w = lax.broadcasted_iota(jnp.int32, (TQ, band), 0)
        dist_w = qi_w + WINDOW - lax.broadcasted_iota(jnp.int32, (TQ, band), 1)
        valid_w = (dist_w >= 0) & (dist_w < WINDOW)
        qi_s = lax.broadcasted_iota(jnp.int32, (TQ, 2 * TQ), 0)
        dist_s = qi_s + TQ - lax.broadcasted_iota(jnp.int32, (TQ, 2 * TQ), 1)
        for hd in range(NSA_HEADS):
            k, g = hd // GQA, hd % GQA
            bw_ref[k, g * TQ:(g + 1) * TQ, :] = table(dist_w, valid_w, hd, 0.0)
            bs_ref[k, g * TQ:(g + 1) * TQ, :] = table(dist_s, dist_s >= 0, hd, rb_ref[N_BUCKETS - 1, hd])


def _bias_tables(rel_bias, s):
    n_cmp = (s - CMP_BLOCK) // CMP_STRIDE + 1
    assert n_cmp < LANES and TQ >= MAX_DISTANCE
    band = WINDOW + TQ
    return pl.pallas_call(
        functools.partial(_bias_kernel, n_cmp=n_cmp),
        out_shape=[jax.ShapeDtypeStruct((HKV, GQA, s, LANES), F32),
                   jax.ShapeDtypeStruct((HKV, GQA * TQ, band), F32),
                   jax.ShapeDtypeStruct((HKV, GQA * TQ, 2 * TQ), F32)],
        grid=(s // TQ,),
        in_specs=[pl.BlockSpec(memory_space=pltpu.SMEM)],
        out_specs=[pl.BlockSpec((HKV, GQA, TQ, LANES), lambda i: (0, 0, i, 0)),
                   pl.BlockSpec((HKV, GQA * TQ, band), lambda i: (0, 0, 0)),
                   pl.BlockSpec((HKV, GQA * TQ, 2 * TQ), lambda i: (0, 0, 0))],
        compiler_params=_cparams("arbitrary"),
        name="nsa_bias_tables",
    )(rel_bias)


def _sel_tables(s):
    n_cmp = (s - CMP_BLOCK) // CMP_STRIDE + 1
    n_sel = s // SEL_BLOCK
    c_start = np.arange(LANES) * CMP_STRIDE
    s_start = np.arange(n_sel) * SEL_BLOCK
    ovl = ((c_start[None, :] < s_start[:, None] + SEL_BLOCK)
           & (c_start[None, :] + CMP_BLOCK > s_start[:, None])
           & (np.arange(LANES)[None, :] < n_cmp)).astype(np.float32)
    key_blk = np.arange(s) // SEL_BLOCK
    expand = np.where(key_blk[None, :] == np.arange(LANES)[:, None], NEG, 0.0).astype(np.float32)
    expand = expand.reshape(LANES, s // TQ, TQ).transpose(1, 0, 2)
    return jnp.asarray(ovl, BF16), jnp.asarray(expand, BF16)


def _nsa_kernel(q_ref, kc_ref, vc_ref, ks_ref, vs_ref, kw_ref, vw_ref, gate_ref,
                bc_ref, bw_ref, bs_ref, ovl_ref, exp_ref, gn_ref, o_ref,
                qs_ref, s_ref, m_ref, l_ref, acc_ref, uns_ref, oc_ref, os_ref, ob_ref):
    i = pl.program_id(1)
    rows = GQA * TQ
    n_sel = ovl_ref.shape[0]
    n_band = WINDOW // TQ + 1
    heads = range(HKV)
    nt_dims = (((1,), (1,)), ((), ()))

    for hd in range(NSA_HEADS):
        qs_ref[hd // GQA, (hd % GQA) * TQ:(hd % GQA + 1) * TQ, :] = q_ref[0, :, hd * LANES:(hd + 1) * LANES]

    def qs(h):
        return qs_ref[h]

    def chunk_rows(c):
        return pl.ds(pl.multiple_of(c * TQ, TQ), TQ)

    def paired_loop(n, body):
        def pair(j, carry):
            body(2 * j)
            body(2 * j + 1)
            return carry

        lax.fori_loop(0, n // 2, pair, 0)

        @pl.when(n % 2 == 1)
        def _():
            body(n - 1)

    def row_max_to_lanes():
        for h in heads:
            m_ref[h] = jnp.broadcast_to(jnp.max(m_ref[h], axis=-1, keepdims=True), (rows, TQ))

    def normalise(acc, lsum):
        return acc / jnp.maximum(jnp.sum(lsum, axis=-1, keepdims=True), 1e-30)

    for h in heads:
        bias_c = bc_ref[h].reshape(rows, LANES)
        sc = lax.dot_general(qs(h), kc_ref[0], nt_dims, preferred_element_type=F32) + bias_c
        mc = jnp.max(sc, axis=-1, keepdims=True)
        pc = jnp.where(bias_c > 0.5 * NEG, jnp.exp2(sc - mc), 0.0)
        pc = pc / jnp.maximum(jnp.sum(pc, axis=-1, keepdims=True), 1e-30)
        oc_ref[h] = jnp.dot(pc.astype(BF16), vc_ref[0], preferred_element_type=F32)

        psum = jnp.sum(pc.reshape(GQA, TQ, LANES), axis=0)
        p_hi = psum.astype(BF16)
        p_lo = (psum - p_hi.astype(F32)).astype(BF16)
        imp = (lax.dot_general(ovl_ref[...], p_hi, nt_dims, preferred_element_type=F32)
               + lax.dot_general(ovl_ref[...], p_lo, nt_dims, preferred_element_type=F32))
        blk = lax.broadcasted_iota(jnp.int32, (n_sel, TQ), 0)
        tok = lax.broadcasted_iota(jnp.int32, (n_sel, TQ), 1) + i * TQ
        blk_of_t = tok // SEL_BLOCK
        forced = (blk == 0) | (blk == blk_of_t) | (blk == blk_of_t - 1)
        score = jnp.where(forced, FORCED_SCORE, jnp.where(blk <= blk_of_t, imp, -1.0))
        rank = jnp.zeros((n_sel, TQ), F32)
        for mm in range(n_sel):
            sm = score[mm:mm + 1, :]
            ahead = (sm > score) | ((sm == score) & (blk > mm))
            rank = rank + jnp.where(ahead, 1.0, 0.0)
        unsel_t = jnp.where(rank < min(SEL_TOP, n_sel), 0.0, 1.0)
        unsel_t = jnp.concatenate([unsel_t, jnp.zeros((LANES - n_sel, TQ), F32)], axis=0).astype(BF16)
        eye = (lax.broadcasted_iota(jnp.int32, (TQ, TQ), 0)
               == lax.broadcasted_iota(jnp.int32, (TQ, TQ), 1)).astype(BF16)
        uns_ref[h] = lax.dot_general(eye, unsel_t, nt_dims, preferred_element_type=F32).astype(BF16)

    def sel_scores(c, bias_cols):
        k = ks_ref[0, chunk_rows(c), :]
        for h in heads:
            madd = jnp.dot(uns_ref[h], exp_ref[c], preferred_element_type=F32)
            s = lax.dot_general(qs(h), k, nt_dims, preferred_element_type=F32)
            s = (s.reshape(GQA, TQ, TQ) + madd[None]).reshape(rows, TQ)
            if bias_cols is not None:
                s = s + bs_ref[h, :, bias_cols:bias_cols + TQ]
            s_ref[h, c] = s
            m_ref[h] = jnp.maximum(m_ref[h], s)

    def softmax_pv(c, v):
        for h in heads:
            p = jnp.exp2(s_ref[h, c] - m_ref[h])
            l_ref[h] += p
            acc_ref[h] += jnp.dot(p.astype(BF16), v, preferred_element_type=F32)

    m_ref[...] = jnp.full(m_ref.shape, NEG, F32)
    paired_loop(jnp.maximum(i - 1, 0), lambda c: sel_scores(c, None))

    @pl.when(i >= 1)
    def _():
        sel_scores(i - 1, 0)

    sel_scores(i, TQ)
    row_max_to_lanes()
    l_ref[...] = jnp.zeros(l_ref.shape, F32)
    acc_ref[...] = jnp.zeros(acc_ref.shape, F32)
    paired_loop(i + 1, lambda c: softmax_pv(c, vs_ref[0, chunk_rows(c), :]))
    for h in heads:
        os_ref[h] = normalise(acc_ref[h], l_ref[h])

    m_ref[...] = jnp.full(m_ref.shape, NEG, F32)
    for jj in range(n_band):
        c = i - (n_band - 1) + jj
        k = kw_ref[0, chunk_rows(jnp.maximum(c, 0)), :]
        for h in heads:
            s = lax.dot_general(qs(h), k, nt_dims, preferred_element_type=F32)
            s = s + bw_ref[h, :, jj * TQ:(jj + 1) * TQ] + jnp.where(c >= 0, 0.0, NEG)
            s_ref[h, jj] = s
            m_ref[h] = jnp.maximum(m_ref[h], s)
    row_max_to_lanes()
    o_win = []
    for h in heads:
        acc = lsum = None
        for jj in range(n_band):
            v = vw_ref[0, chunk_rows(jnp.maximum(i - (n_band - 1) + jj, 0)), :]
            p = jnp.exp2(s_ref[h, jj] - m_ref[h])
            pv = jnp.dot(p.astype(BF16), v, preferred_element_type=F32)
            acc = pv if acc is None else acc + pv
            lsum = p if lsum is None else lsum + p
        o_win.append(normalise(acc, lsum))

    gates = gate_ref[0]
    for hd in range(NSA_HEADS):
        h, g = hd // GQA, hd % GQA
        r = slice(g * TQ, (g + 1) * TQ)
        o = (gates[:, hd:hd + 1] * oc_ref[h, r, :]
             + gates[:, NSA_HEADS + hd:NSA_HEADS + hd + 1] * os_ref[h, r, :]
             + gates[:, 2 * NSA_HEADS + hd:2 * NSA_HEADS + hd + 1] * o_win[h][r])
        ob_ref[:, hd * DH:(hd + 1) * DH] = o[:, h * DH:(h + 1) * DH]

    o = ob_ref[...]
    ms = jnp.mean(o * o, axis=-1, keepdims=True)
    o_ref[0] = (o * lax.rsqrt(ms + EPS) * gn_ref[...]).astype(BF16)


def _nsa(q, kc, vc, ks, vs, kw, vw, gates, tables, attn_out_norm):
    b, s, _ = q.shape
    bias_c, bias_w, bias_s, ovl, expand = tables
    gn = attn_out_norm.reshape(1, NSA_WIDTH)
    full = lambda a: pl.BlockSpec(a.shape, lambda bi, i: (0,) * a.ndim)
    seq = pl.BlockSpec((1, s, LANES), lambda bi, i: (bi, 0, 0))
    cmp = pl.BlockSpec((1,) + kc.shape[1:], lambda bi, i: (bi, 0, 0))
    rows = GQA * TQ
    return pl.pallas_call(
        _nsa_kernel,
        out_shape=jax.ShapeDtypeStruct((b, s, NSA_WIDTH), BF16),
        grid=(b, s // TQ),
        in_specs=[pl.BlockSpec((1, TQ, NSA_HEADS * LANES), lambda bi, i: (bi, i, 0)),
                  cmp, cmp, seq, seq, seq, seq,
                  pl.BlockSpec((1, TQ, LANES), lambda bi, i: (bi, i, 0)),
                  pl.BlockSpec((HKV, GQA, TQ, LANES), lambda bi, i: (0, 0, i, 0)),
                  full(bias_w), full(bias_s), full(ovl), full(expand), full(gn)],
        out_specs=pl.BlockSpec((1, TQ, NSA_WIDTH), lambda bi, i: (bi, i, 0)),
        scratch_shapes=[pltpu.VMEM((HKV, rows, LANES), BF16),
                        pltpu.VMEM((HKV, s // TQ, rows, TQ), F32),
                        pltpu.VMEM((HKV, rows, TQ), F32),
                        pltpu.VMEM((HKV, rows, TQ), F32),
                        pltpu.VMEM((HKV, rows, LANES), F32),
                        pltpu.VMEM((HKV, TQ, LANES), BF16),
                        pltpu.VMEM((HKV, rows, LANES), F32),
                        pltpu.VMEM((HKV, rows, LANES), F32),
                        pltpu.VMEM((TQ, NSA_WIDTH), F32)],
        compiler_params=_cparams("parallel", "arbitrary"),
        name="nsa_attention",
    )(q, kc, vc, ks, vs, kw, vw, gates, bias_c, bias_w, bias_s, ovl, expand, gn)


def _ssd_kernel(xbc_ref, z_ref, dt_ref, cw_ref, cb_ref, dtb_ref, alog_ref, dsk_ref, ng_ref, o_ref,
                xbuf_ref, state_ref, y_ref):
    c = pl.program_id(1)
    L, P, N = SSD_L, SSD_P, SSD_N
    hpg = SSD_HEADS // SSD_GROUPS
    pad = 8

    @pl.when(c == 0)
    def _():
        xbuf_ref[0:pad, :] = jnp.zeros((pad, SSD_CONV_DIM), F32)
        state_ref[...] = jnp.zeros(state_ref.shape, F32)

    xbuf_ref[pad:pad + L, :] = xbc_ref[0]
    conv = cb_ref[...]
    for k in range(SSD_CONV):
        shift = SSD_CONV - 1 - k
        conv = conv + xbuf_ref[pad - shift:pad - shift + L, :] * cw_ref[k:k + 1, :]
    xbuf_ref[0:pad, :] = xbuf_ref[L:L + pad, :]
    xa = _silu(conv)
    xs = xa[:, :SSD_WIDTH]
    bm = xa[:, SSD_WIDTH:SSD_WIDTH + SSD_GROUPS * N]
    cm = xa[:, SSD_WIDTH + SSD_GROUPS * N:]

    dtv = dt_ref[0] + dtb_ref[...]
    dt = jnp.maximum(dtv, 0.0) + jnp.log1p(jnp.exp(-jnp.abs(dtv)))
    a = dt * (-jnp.exp(alog_ref[...]))
    ri = lax.broadcasted_iota(jnp.int32, (L, L), 0)
    ci = lax.broadcasted_iota(jnp.int32, (L, L), 1)
    causal = ri >= ci
    cs = jnp.dot(causal.astype(F32), a, preferred_element_type=F32,
                 precision=lax.Precision.HIGHEST)
    cs_t = cs.T
    nt_dims = (((1,), (1,)), ((), ()))

    for gr in range(SSD_GROUPS):
        b_g = bm[:, gr * N:(gr + 1) * N]
        c_g = cm[:, gr * N:(gr + 1) * N]
        scores = lax.dot_general(c_g.astype(BF16), b_g.astype(BF16), nt_dims, preferred_element_type=F32)
        b_gt = b_g.T
        for hh in range(hpg):
            h = gr * hpg + hh
            cs_col = cs[:, h:h + 1]
            cs_row = cs_t[h:h + 1, :]
            cs_last = cs[L - 1:L, h:h + 1]
            decay = jnp.exp(jnp.where(causal, cs_col - cs_row, NEG))
            xs_h = xs[:, h * P:(h + 1) * P]
            xc = (xs_h * dt[:, h:h + 1]).astype(BF16)
            y = jnp.dot((scores * decay).astype(BF16), xc, preferred_element_type=F32)
            prev = state_ref[h]
            y = y + jnp.dot((c_g * jnp.exp(cs_col)).astype(BF16), prev.astype(BF16),
                            preferred_element_type=F32)
            contrib = jnp.dot((b_gt * jnp.exp(cs_last - cs_row)).astype(BF16), xc,
                              preferred_element_type=F32)
            state_ref[h] = jnp.exp(cs_last) * prev + contrib
            y_ref[:, h * P:(h + 1) * P] = y + xs_h * dsk_ref[:, h * P:(h + 1) * P]

    z = z_ref[0]
    y = y_ref[...] * _silu(z)
    gw = SSD_WIDTH // SSD_GROUPS
    for gr in range(SSD_GROUPS):
        yg = y[:, gr * gw:(gr + 1) * gw]
        ms = jnp.mean(yg * yg, axis=-1, keepdims=True)
        o_ref[0, :, gr * gw:(gr + 1) * gw] = (yg * lax.rsqrt(ms + EPS)
                                              * ng_ref[:, gr * gw:(gr + 1) * gw]).astype(BF16)


def _ssd(xbc, z, dt, conv_w, conv_b, dt_bias, a_log, d_skip, norm_g):
    b, s, _ = xbc.shape
    padl = lambda v: jnp.pad(v, (0, LANES - v.shape[0])).reshape(1, LANES)
    args = (conv_w, conv_b.reshape(1, SSD_CONV_DIM), padl(dt_bias), padl(a_log),
            jnp.repeat(d_skip, SSD_P).reshape(1, SSD_WIDTH), norm_g.reshape(1, SSD_WIDTH))
    full = lambda a: pl.BlockSpec(a.shape, lambda bi, c: (0,) * a.ndim)
    blk = lambda w: pl.BlockSpec((1, SSD_L, w), lambda bi, c: (bi, c, 0))
    return pl.pallas_call(
        _ssd_kernel,
        out_shape=jax.ShapeDtypeStruct((b, s, SSD_WIDTH), BF16),
        grid=(b, s // SSD_L),
        in_specs=[blk(SSD_CONV_DIM), blk(SSD_WIDTH), blk(LANES)] + [full(a) for a in args],
        out_specs=blk(SSD_WIDTH),
        scratch_shapes=[pltpu.VMEM((SSD_L + 8, SSD_CONV_DIM), F32),
                        pltpu.VMEM((SSD_HEADS, SSD_N, SSD_P), F32),
                        pltpu.VMEM((SSD_L, SSD_WIDTH), F32)],
        compiler_params=_cparams("parallel", "arbitrary"),
        name="ssd_mixer",
    )(xbc, z, dt, *args)


def _out_proj_kernel(x_ref, on_ref, os_ref, wo_ref, fg_ref, rw_ref, rb_ref, h_ref, hn_ref, comb_ref, slot_ref):
    h = (x_ref[...]
         + jnp.dot(on_ref[...], wo_ref[0:NSA_WIDTH, :], preferred_element_type=F32)
         + jnp.dot(os_ref[...], wo_ref[NSA_WIDTH:, :], preferred_element_type=F32))
    tm = h.shape[0]
    for j in range(D_MODEL // LANES):
        h_ref[pl.ds(j, tm, stride=D_MODEL // LANES), :] = h[:, j * LANES:(j + 1) * LANES]
    ms = jnp.mean(h * h, axis=-1, keepdims=True)
    hn = h * lax.rsqrt(ms + EPS) * fg_ref[...]
    for j in range(D_MODEL // LANES):
        hn_ref[pl.ds(j, tm, stride=D_MODEL // LANES), :] = hn[:, j * LANES:(j + 1) * LANES]
    logits = jnp.dot(hn, rw_ref[...], preferred_element_type=F32,
                     precision=lax.Precision.HIGHEST) + rb_ref[...]
    lane = lax.broadcasted_iota(jnp.int32, logits.shape, 1)
    work = logits
    picks = []
    for _ in range(TOP_K):
        v = jnp.max(work, axis=-1, keepdims=True)
        idx = jnp.min(jnp.where(work == v, lane, LANES), axis=-1, keepdims=True)
        hit = lane == idx
        picks.append((v, hit))
        work = jnp.where(hit, -3e38, work)
    v0 = picks[0][0]
    es = [jnp.exp(v - v0) for v, _ in picks]
    den = es[0] + es[1] + es[2] + es[3]
    comb = jnp.zeros_like(logits)
    slot = jnp.zeros(logits.shape, jnp.int32)
    for k, (e, (_, hit)) in enumerate(zip(es, picks)):
        comb = comb + jnp.where(hit, e / den, 0.0)
        slot = jnp.where(hit, k + 1, slot)
    comb_ref[...] = comb
    slot_ref[...] = slot


def _out_proj(x2, o_nsa, o_ssd, w_out, ffn_norm, router_w, router_b, tm=512):
    t = x2.shape[0]
    row = lambda w: pl.BlockSpec((tm, w), lambda i: (i, 0))
    full = lambda a: pl.BlockSpec(a.shape, lambda i: (0,) * a.ndim)
    wo = w_out.astype(BF16)
    fg = ffn_norm.reshape(1, D_MODEL)
    rw = jnp.pad(router_w, ((0, 0), (0, LANES - N_EXPERTS)))
    rb = jnp.pad(router_b, (0, LANES - N_EXPERTS), constant_values=NEG).reshape(1, LANES)
    return pl.pallas_call(
        _out_proj_kernel,
        out_shape=[jax.ShapeDtypeStruct((t * H_SLAB, LANES), F32),
                   jax.ShapeDtypeStruct((t * H_SLAB, LANES), F32),
                   jax.ShapeDtypeStruct((t, LANES), F32),
                   jax.ShapeDtypeStruct((t, LANES), jnp.int32)],
        grid=(t // tm,),
        in_specs=[row(D_MODEL), row(NSA_WIDTH), row(SSD_WIDTH), full(wo), full(fg), full(rw), full(rb)],
        out_specs=[pl.BlockSpec((tm * H_SLAB, LANES), lambda i: (i, 0)),
                   pl.BlockSpec((tm * H_SLAB, LANES), lambda i: (i, 0)), row(LANES), row(LANES)],
        compiler_params=_cparams("parallel"),
        name="out_proj_router",
    )(x2, o_nsa, o_ssd, wo, fg, rw, rb)


MOE_CHUNK = 4096
MOE_TILE = 256
MOE_REM_TILE = 128
ROUTE_TILE = 256


def _route_kernel(comb_ref, slot_ref, dest_ref, wrow_ref, starts_ref, pos_ref):
    tc = comb_ref.shape[0]
    nt_dims = (((1,), (1,)), ((), ()))
    hi = lax.Precision.HIGHEST
    ri = lax.broadcasted_iota(jnp.int32, (ROUTE_TILE, ROUTE_TILE), 0)
    ci = lax.broadcasted_iota(jnp.int32, (ROUTE_TILE, ROUTE_TILE), 1)
    below = (ri > ci).astype(BF16)
    carry = jnp.zeros((1, LANES), F32)
    for j in range(tc // ROUTE_TILE):
        rows = slice(j * ROUTE_TILE, (j + 1) * ROUTE_TILE)
        sel = jnp.where(slot_ref[rows, :] > 0, 1.0, 0.0)
        pos_ref[rows, :] = jnp.dot(below, sel.astype(BF16), preferred_element_type=F32) + carry
        carry = carry + jnp.sum(sel, axis=0, keepdims=True)
    li = lax.broadcasted_iota(jnp.int32, (LANES, LANES), 0)
    lj = lax.broadcasted_iota(jnp.int32, (LANES, LANES), 1)
    before = (li < lj).astype(F32)
    counts = jnp.broadcast_to(carry, (8, LANES))
    starts = jnp.dot(counts, before, preferred_element_type=F32, precision=hi)
    starts_ref[0] = starts.astype(jnp.int32)
    dest = pos_ref[...] + starts[0:1, :]
    ones = jnp.ones((8, LANES), F32)
    slot = slot_ref[...]
    comb = comb_ref[...]
    for k in range(TOP_K):
        hit = slot == k + 1
        d = lax.dot_general(ones, jnp.where(hit, dest, 0.0), nt_dims, preferred_element_type=F32, precision=hi)
        w = lax.dot_general(ones, jnp.where(hit, comb, 0.0), nt_dims, preferred_element_type=F32, precision=hi)
        dest_ref[0, k:k + 1, :] = d[0:1, :].astype(jnp.int32)
        wrow_ref[0, k:k + 1, :] = w[0:1, :]


def _route(comb, slot, tc):
    t = comb.shape[0]
    nch = t // tc
    blk = pl.BlockSpec((tc, LANES), lambda c: (c, 0))
    return pl.pallas_call(
        _route_kernel,
        out_shape=[jax.ShapeDtypeStruct((nch, TOP_K, tc), jnp.int32),
                   jax.ShapeDtypeStruct((nch, TOP_K, tc), F32),
                   jax.ShapeDtypeStruct((nch, 8, LANES), jnp.int32)],
        grid=(nch,),
        in_specs=[blk, blk],
        out_specs=[pl.BlockSpec((1, TOP_K, tc), lambda c: (c, 0, 0)),
                   pl.BlockSpec((1, TOP_K, tc), lambda c: (c, 0, 0)),
                   pl.BlockSpec((1, 8, LANES), lambda c: (c, 0, 0))],
        scratch_shapes=[pltpu.VMEM((tc, LANES), F32)],
        compiler_params=_cparams("parallel"),
        name="moe_route",
    )(comb, slot)


def _moe_kernel(starts_ref, dest_hbm, wrow_hbm, hn_hbm, h_hbm, wgu_ref, bgu_ref, wd_ref, bd_ref, fn_ref, o_hbm,
                x_ref, acc_ref, xs_ref, y_ref, ob_ref, dest_s, wrow_s, tok_s, wt_s, sem, osem):
    c = pl.program_id(0)
    e = pl.program_id(1)
    tc = x_ref.shape[0] // H_SLAB
    n_rows = TOP_K * tc

    def slab(i, n, width):
        return pl.ds(pl.multiple_of(i * width, width), n * width)

    copies = lambda: (pltpu.make_async_copy(hn_hbm.at[slab(c * tc, tc, H_SLAB), :], x_ref, sem.at[0]),
                      pltpu.make_async_copy(h_hbm.at[slab(c * tc, tc, H_SLAB), :],
                                            acc_ref.at[pl.ds(0, tc * H_SLAB), :], sem.at[1]),
                      pltpu.make_async_copy(dest_hbm.at[c], dest_s, sem.at[2]),
                      pltpu.make_async_copy(wrow_hbm.at[c], wrow_s, sem.at[3]))

    @pl.when(e == 0)
    def _():
        for cp in copies():
            cp.start()
        for cp in copies():
            cp.wait()
        acc_ref[pl.ds(tc * H_SLAB, H_SLAB), :] = jnp.zeros((H_SLAB, LANES), F32)

        def invert(j, carry):
            r = dest_s[j]
            tok_s[r] = j & (tc - 1)
            wt_s[r] = wrow_s[j]
            return carry

        lax.fori_loop(0, n_rows, invert, 0, unroll=8)

    base = starts_ref[c * LANES + e]
    n_e = starts_ref[c * LANES + e + 1] - base

    def tile(r0, n_valid, size):
        def gather(g, carry2):
            for u in range(8):
                r = g * 8 + u
                tok = tok_s[jnp.minimum(r0 + r, n_rows - 1)]
                xs_ref[slab(r, 1, H_SLAB), :] = x_ref[slab(tok, 1, H_SLAB), :]
            return carry2

        lax.fori_loop(0, size // 8, gather, 0)
        x = jnp.concatenate([xs_ref[pl.ds(j, size, stride=H_SLAB), :].astype(BF16) for j in range(H_SLAB)],
                            axis=1)
        gu = jnp.dot(x, wgu_ref[0], preferred_element_type=F32) + bgu_ref[0]
        gate = jnp.minimum(gu[:, :D_FF], SWIGLU_LIMIT)
        up = jnp.clip(gu[:, D_FF:], -SWIGLU_LIMIT, SWIGLU_LIMIT)
        act = (up + 1.0) * gate / (1.0 + jnp.exp(-SWIGLU_ALPHA * gate))
        y = jnp.dot(act.astype(BF16), wd_ref[0], preferred_element_type=F32) + bd_ref[0]
        for j in range(H_SLAB):
            y_ref[pl.ds(j, size, stride=H_SLAB), :] = y[:, j * LANES:(j + 1) * LANES]

        def scatter(g, carry2):
            rows = []
            for u in range(8):
                r = g * 8 + u
                idx = jnp.minimum(r0 + r, n_rows - 1)
                tok = jnp.where(r < n_valid, tok_s[idx], tc)
                rows.append((tok, acc_ref[slab(tok, 1, H_SLAB), :] + wt_s[idx] * y_ref[slab(r, 1, H_SLAB), :]))
            for tok, val in rows:
                acc_ref[slab(tok, 1, H_SLAB), :] = val
            return carry2

        lax.fori_loop(0, (n_valid + 7) // 8, scatter, 0)

    n_full = n_e // MOE_TILE
    rem0 = base + n_full * MOE_TILE
    rem = n_e - n_full * MOE_TILE

    def full_tile(j, carry):
        tile(base + j * MOE_TILE, MOE_TILE, MOE_TILE)
        return carry

    def rem_tile(j, carry):
        tile(rem0 + j * MOE_REM_TILE, jnp.minimum(rem - j * MOE_REM_TILE, MOE_REM_TILE), MOE_REM_TILE)
        return carry

    lax.fori_loop(0, n_full, full_tile, 0)
    lax.fori_loop(0, (rem + MOE_REM_TILE - 1) // MOE_REM_TILE, rem_tile, 0)

    @pl.when(e == pl.num_programs(1) - 1)
    def _():
        n_groups = tc // MOE_TILE
        out_copy = lambda g, buf: pltpu.make_async_copy(
            ob_ref.at[buf], o_hbm.at[pl.ds(pl.multiple_of(c * tc + g * MOE_TILE, MOE_TILE), MOE_TILE), :],
            osem.at[buf])

        def norm(g, carry):
            buf = g % 2

            @pl.when(g >= 2)
            def _():
                out_copy(g - 2, buf).wait()

            first = pl.multiple_of(g * MOE_TILE * H_SLAB, MOE_TILE * H_SLAB)
            hs = [acc_ref[pl.ds(first + j, MOE_TILE, stride=H_SLAB), :] for j in range(H_SLAB)]
            ss = hs[0] * hs[0]
            for hj in hs[1:]:
                ss = ss + hj * hj
            inv = lax.rsqrt(jnp.sum(ss, axis=-1, keepdims=True) / D_MODEL + EPS)
            for j, hj in enumerate(hs):
                ob_ref[buf, :, j * LANES:(j + 1) * LANES] = hj * inv * fn_ref[:, j * LANES:(j + 1) * LANES]
            out_copy(g, buf).start()
            return carry

        lax.fori_loop(0, n_groups, norm, 0)
        for g in range(max(n_groups - 2, 0), n_groups):
            out_copy(g, g % 2).wait()


def _moe(hnp, comb, slot, h1, w_gate_up, b_gate_up, w_down, b_down, final_norm, tc=MOE_CHUNK):
    t = comb.shape[0]
    tc = min(tc, t)
    assert tc & (tc - 1) == 0 and tc % MOE_TILE == 0
    nch = t // tc
    dest, wrow, starts = _route(comb, slot, tc)
    dest = dest.reshape(nch, TOP_K * tc)
    wrow = wrow.reshape(nch, TOP_K * tc)
    starts = starts[:, 0, :].reshape(nch * LANES)
    wgu = w_gate_up.astype(BF16)
    wd = w_down.astype(BF16)
    bgu = b_gate_up.reshape(N_EXPERTS, 1, 2 * D_FF)
    bd = b_down.reshape(N_EXPERTS, 1, D_MODEL)
    fn = final_norm.reshape(1, D_MODEL)
    anyspace = pl.BlockSpec(memory_space=pl.ANY)
    exp = lambda a: pl.BlockSpec((1,) + a.shape[1:], lambda c, e, st: (e, 0, 0))
    return pl.pallas_call(
        _moe_kernel,
        out_shape=jax.ShapeDtypeStruct((t, D_MODEL), F32),
        grid_spec=pltpu.PrefetchScalarGridSpec(
            num_scalar_prefetch=1,
            grid=(nch, N_EXPERTS),
            in_specs=[anyspace, anyspace, anyspace, anyspace, exp(wgu), exp(bgu), exp(wd), exp(bd),
                      pl.BlockSpec(fn.shape, lambda c, e, st: (0, 0))],
            out_specs=anyspace,
            scratch_shapes=[pltpu.VMEM((tc * H_SLAB, LANES), F32),
                            pltpu.VMEM(((tc + 1) * H_SLAB, LANES), F32),
                            pltpu.VMEM((MOE_TILE * H_SLAB, LANES), F32),
                            pltpu.VMEM((MOE_TILE * H_SLAB, LANES), F32),
                            pltpu.VMEM((2, MOE_TILE, D_MODEL), F32),
                            pltpu.SMEM((TOP_K * tc,), jnp.int32),
                            pltpu.SMEM((TOP_K * tc,), F32),
                            pltpu.SMEM((TOP_K * tc,), jnp.int32),
                            pltpu.SMEM((TOP_K * tc,), F32),
                            pltpu.SemaphoreType.DMA((4,)),
                            pltpu.SemaphoreType.DMA((2,))]),
        compiler_params=_cparams("arbitrary", "arbitrary", vmem=MOE_VMEM_LIMIT),
        name="moe_experts",
    )(starts, dest, wrow, hnp, h1, wgu, bgu, wd, bd, fn)


def kernel(x, attn_norm, w_in, rel_bias, cmp_pos, cmp_w1, cmp_w2, attn_out_norm, conv_w, conv_b,
           dt_bias, a_log, d_skip, ssm_out_norm, w_out, ffn_norm, router_w, router_b,
           w_gate_up, b_gate_up, w_down, b_down, final_norm):
    b, s, d = x.shape
    t = b * s
    depth = w_in.shape[0]
    tables = tuple(_bias_tables(rel_bias, s)) + _sel_tables(s)
    h = x.reshape(t, d)
    for l in range(depth):
        (q, kc_raw, vc_raw, ks, vs, kw, vw, gates, z, xbc, dt) = _in_proj(h, attn_norm[l], _pad_w_in(w_in[l]))
        grp = CMP_STRIDE * HKV * DH
        kc, vc = _compress(kc_raw.reshape(b, s // CMP_STRIDE, grp), vc_raw.reshape(b, s // CMP_STRIDE, grp),
                           _compress_weights(cmp_pos[l], cmp_w1[l], cmp_w2[l]))
        seq = lambda a: a.reshape(b, s, a.shape[-1])
        o_nsa = _nsa(seq(q), kc, vc, seq(ks), seq(vs), seq(kw), seq(vw), seq(gates), tables, attn_out_norm[l])
        o_ssd = _ssd(seq(xbc), seq(z), seq(dt), conv_w[l], conv_b[l], dt_bias[l], a_log[l], d_skip[l],
                     ssm_out_norm[l])
        h1, hnp, comb, slot = _out_proj(h, o_nsa.reshape(t, NSA_WIDTH), o_ssd.reshape(t, SSD_WIDTH), w_out[l],
                                        ffn_norm[l], router_w[l], router_b[l])
        assert depth == 1
        h = _moe(hnp, comb, slot, h1, w_gate_up[l], b_gate_up[l], w_down[l], b_down[l], final_norm)
    return h.reshape(b, s, d)
```

```python
import functools
import math

import numpy as np
import jax
import jax.numpy as jnp
from jax import lax
from jax.experimental import pallas as pl
from jax.experimental.pallas import tpu as pltpu

F32 = jnp.float32
BF16 = jnp.bfloat16

D_MODEL = 1024
NSA_HEADS = 8
HKV = 2
GQA = NSA_HEADS // HKV
DH = 64
NSA_WIDTH = NSA_HEADS * DH
CMP_BLOCK = 32
CMP_STRIDE = 16
SEL_BLOCK = 64
SEL_TOP = 16
WINDOW = 512
TQ = 128
SSD_HEADS = 8
SSD_P = 64
SSD_WIDTH = SSD_HEADS * SSD_P
SSD_GROUPS = 2
SSD_N = 128
SSD_CONV = 4
SSD_L = 128
SSD_CONV_DIM = SSD_WIDTH + 2 * SSD_GROUPS * SSD_N
N_BUCKETS = 32
MAX_DISTANCE = 128
N_EXPERTS = 32
TOP_K = 4
D_FF = 1024
SWIGLU_LIMIT = 7.0
SWIGLU_ALPHA = 1.702

EPS = 1e-6
NEG = -1e30
FORCED_SCORE = 1e4
LOG2E = 1.4426950408889634
LANES = 128
SUBLANES = 8
H_SLAB = D_MODEL // LANES
VMEM_LIMIT = 56 * 1024 * 1024
MOE_VMEM_LIMIT = (2 * 4096 * D_MODEL * 4 + 2 * 3 * D_MODEL * D_FF * 2 + 12 * 1024 * 1024)


def _cparams(*sem, vmem=VMEM_LIMIT):
    return pltpu.CompilerParams(dimension_semantics=sem, vmem_limit_bytes=vmem)


def _silu(v):
    return v / (1.0 + jnp.exp(-v))


_Q0, _Q1 = 0, NSA_HEADS * LANES
_KV0 = _Q1
_G0 = _KV0 + 6 * LANES
_Z0 = _G0 + LANES
_X0 = _Z0 + SSD_WIDTH
_DT0 = _X0 + SSD_CONV_DIM
_WCOLS = _DT0 + LANES


def _pad_w_in(w_in):
    d = w_in.shape[0]
    nsa_cols = NSA_WIDTH + 6 * HKV * DH + 3 * NSA_HEADS
    wq = w_in[:, :NSA_WIDTH].reshape(d, HKV, GQA, DH)
    zq = jnp.zeros_like(wq)
    q0 = jnp.concatenate([wq[:, 0], zq[:, 0]], axis=-1)
    q1 = jnp.concatenate([zq[:, 1], wq[:, 1]], axis=-1)
    wq_pad = jnp.stack([q0, q1], axis=1).reshape(d, NSA_HEADS * LANES)
    wkv = w_in[:, NSA_WIDTH:NSA_WIDTH + 6 * HKV * DH]
    wg = w_in[:, NSA_WIDTH + 6 * HKV * DH:nsa_cols]
    wg = jnp.pad(wg, ((0, 0), (0, LANES - wg.shape[1])))
    wz = w_in[:, nsa_cols:nsa_cols + SSD_WIDTH]
    wx = w_in[:, nsa_cols + SSD_WIDTH:nsa_cols + SSD_WIDTH + SSD_CONV_DIM]
    wdt = w_in[:, nsa_cols + SSD_WIDTH + SSD_CONV_DIM:]
    wdt = jnp.pad(wdt, ((0, 0), (0, LANES - wdt.shape[1])))
    return jnp.concatenate([wq_pad, wkv, wg, wz, wx, wdt], axis=1).astype(BF16)


def _in_proj_kernel(x_ref, g_ref, w_ref, q_ref, kc_ref, vc_ref, ks_ref, vs_ref, kw_ref, vw_ref,
                    gate_ref, z_ref, xbc_ref, dt_ref):
    x = x_ref[...]
    ms = jnp.mean(x * x, axis=-1, keepdims=True)
    xn = (x * lax.rsqrt(ms + EPS) * g_ref[...]).astype(BF16)

    def seg(lo, hi):
        return jnp.dot(xn, w_ref[:, lo:hi], preferred_element_type=F32)

    q_ref[...] = (seg(_Q0, _Q1) * (DH ** -0.5 * LOG2E)).astype(BF16)
    for j, ref in enumerate((kc_ref, vc_ref, ks_ref, vs_ref, kw_ref, vw_ref)):
        ref[...] = seg(_KV0 + j * LANES, _KV0 + (j + 1) * LANES).astype(BF16)
    gate_ref[...] = 1.0 / (1.0 + jnp.exp(-seg(_G0, _Z0)))
    z_ref[...] = seg(_Z0, _X0)
    xbc_ref[...] = seg(_X0, _DT0)
    dt_ref[...] = seg(_DT0, _WCOLS)


def _in_proj(x2, attn_norm, w_pad, tm=512):
    t = x2.shape[0]
    row = lambda w: pl.BlockSpec((tm, w), lambda i: (i, 0))
    full = lambda a: pl.BlockSpec(a.shape, lambda i: (0,) * a.ndim)
    g = attn_norm.reshape(1, D_MODEL)
    outs = ([jax.ShapeDtypeStruct((t, NSA_HEADS * LANES), BF16)]
            + [jax.ShapeDtypeStruct((t, LANES), BF16)] * 6
            + [jax.ShapeDtypeStruct((t, LANES), F32),
               jax.ShapeDtypeStruct((t, SSD_WIDTH), F32),
               jax.ShapeDtypeStruct((t, SSD_CONV_DIM), F32),
               jax.ShapeDtypeStruct((t, LANES), F32)])
    return pl.pallas_call(
        _in_proj_kernel,
        out_shape=outs,
        grid=(t // tm,),
        in_specs=[row(D_MODEL), full(g), full(w_pad)],
        out_specs=[row(s.shape[1]) for s in outs],
        compiler_params=_cparams("parallel"),
        name="in_proj",
    )(x2, g, w_pad)


def _compress_weights(cmp_pos, cmp_w1, cmp_w2):
    half = CMP_BLOCK // 2
    eye = jnp.eye(HKV, dtype=F32)
    w1 = cmp_w1.reshape(2, CMP_BLOCK, DH, DH)
    w1big = jnp.einsum('jlde,hk->jlhdke', w1, eye)
    w1lo = w1big[:, :half].reshape(2, half * HKV * DH, HKV * DH).astype(BF16)
    w1hi = w1big[:, half:].reshape(2, half * HKV * DH, HKV * DH).astype(BF16)
    pos = jnp.broadcast_to(cmp_pos[:, :, None, :], (2, CMP_BLOCK, HKV, DH))
    poslo = pos[:, :half].reshape(2, 1, half * HKV * DH)
    poshi = pos[:, half:].reshape(2, 1, half * HKV * DH)
    w2big = jnp.einsum('jde,hk->jhdke', cmp_w2, eye).reshape(2, HKV * DH, HKV * DH).astype(BF16)
    return w1lo, w1hi, poslo, poshi, w2big


def _compress_kernel(kr_ref, vr_ref, w1lo_ref, w1hi_ref, poslo_ref, poshi_ref, w2_ref, kc_ref, vc_ref):
    for j, (src, dst) in enumerate(((kr_ref, kc_ref), (vr_ref, vc_ref))):
        r = src[0].astype(F32)
        a = jnp.dot((r + poslo_ref[j]).astype(BF16), w1lo_ref[j], preferred_element_type=F32)
        b = jnp.dot((r + poshi_ref[j]).astype(BF16), w1hi_ref[j], preferred_element_type=F32)
        hid = a + pltpu.roll(b, b.shape[0] - 1, 0)
        dst[0] = jnp.dot(_silu(hid).astype(BF16), w2_ref[j], preferred_element_type=F32).astype(BF16)


def _compress(kr, vr, cw):
    b, ng, width = kr.shape
    w1lo, w1hi, poslo, poshi, w2big = cw
    full = lambda a: pl.BlockSpec(a.shape, lambda i: (0,) * a.ndim)
    bspec = pl.BlockSpec((1, ng, width), lambda i: (i, 0, 0))
    ospec = pl.BlockSpec((1, ng, HKV * DH), lambda i: (i, 0, 0))
    out = jax.ShapeDtypeStruct((b, ng, HKV * DH), BF16)
    return pl.pallas_call(
        _compress_kernel,
        out_shape=[out, out],
        grid=(b,),
        in_specs=[bspec, bspec, full(w1lo), full(w1hi), full(poslo), full(poshi), full(w2big)],
        out_specs=[ospec, ospec],
        compiler_params=_cparams("parallel"),
        name="nsa_compress",
    )(kr, vr, w1lo, w1hi, poslo, poshi, w2big)


def _bucket_thresholds():
    d = np.arange(MAX_DISTANCE + 1)
    max_exact = N_BUCKETS // 2
    nf = np.maximum(d, max_exact).astype(np.float32)
    large = max_exact + (np.log(nf / np.float32(max_exact)) / np.float32(math.log(MAX_DISTANCE / max_exact))
                         * np.float32(N_BUCKETS - max_exact)).astype(np.int32)
    bucket = np.where(d < max_exact, d, np.minimum(large, N_BUCKETS - 1))
    assert np.all(np.diff(bucket) >= 0) and bucket[MAX_DISTANCE] == N_BUCKETS - 1
    return [int(np.argmax(bucket >= k)) for k in range(N_BUCKETS)]


def _bias_kernel(rb_ref, bc_ref, bw_ref, bs_ref, *, n_cmp):
    i = pl.program_id(0)
    thr = _bucket_thresholds()

    def table(dist, valid, hd, shift):
        v = jnp.full(dist.shape, rb_ref[0, hd], F32)
        for k in range(1, N_BUCKETS):
            v = jnp.where(dist >= thr[k], rb_ref[k, hd], v)
        return jnp.where(valid, (v - shift) * LOG2E, NEG)

    row = lax.broadcasted_iota(jnp.int32, (TQ, LANES), 0)
    col = lax.broadcasted_iota(jnp.int32, (TQ, LANES), 1)
    dist_c = i * TQ + row - (col * CMP_STRIDE + CMP_BLOCK - 1)
    valid_c = (dist_c >= 0) & (col < n_cmp)
    for hd in range(NSA_HEADS):
        bc_ref[hd // GQA, hd % GQA] = table(dist_c, valid_c, hd, 0.0)

    @pl.when(i == 0)
    def _():
        band = WINDOW + TQ
        qi_w = lax.broadcasted_iota(jnp.int32, (TQ, band), 0)
        dist_w = qi_w + WINDOW - lax.broadcasted_iota(jnp.int32, (TQ, band), 1)
        valid_w = (dist_w >= 0) & (dist_w < WINDOW)
        qi_s = lax.broadcasted_iota(jnp.int32, (TQ, 2 * TQ), 0)
        dist_s = qi_s + TQ - lax.broadcasted_iota(jnp.int32, (TQ, 2 * TQ), 1)
        for hd in range(NSA_HEADS):
            k, g = hd // GQA, hd % GQA
            bw_ref[k, g * TQ:(g + 1) * TQ, :] = table(dist_w, valid_w, hd, 0.0)
            bs_ref[k, g * TQ:(g + 1) * TQ, :] = table(dist_s, dist_s >= 0, hd, rb_ref[N_BUCKETS - 1, hd])


def _bias_tables(rel_bias, s):
    n_cmp = (s - CMP_BLOCK) // CMP_STRIDE + 1
    assert n_cmp < LANES and TQ >= MAX_DISTANCE
    band = WINDOW + TQ
    return pl.pallas_call(
        functools.partial(_bias_kernel, n_cmp=n_cmp),
        out_shape=[jax.ShapeDtypeStruct((HKV, GQA, s, LANES), F32),
                   jax.ShapeDtypeStruct((HKV, GQA * TQ, band), F32),
                   jax.ShapeDtypeStruct((HKV, GQA * TQ, 2 * TQ), F32)],
        grid=(s // TQ,),
        in_specs=[pl.BlockSpec(memory_space=pltpu.SMEM)],
        out_specs=[pl.BlockSpec((HKV, GQA, TQ, LANES), lambda i: (0, 0, i, 0)),
                   pl.BlockSpec((HKV, GQA * TQ, band), lambda i: (0, 0, 0)),
                   pl.BlockSpec((HKV, GQA * TQ, 2 * TQ), lambda i: (0, 0, 0))],
        compiler_params=_cparams("arbitrary"),
        name="nsa_bias_tables",
    )(rel_bias)


def _sel_tables(s):
    n_cmp = (s - CMP_BLOCK) // CMP_STRIDE + 1
    n_sel = s // SEL_BLOCK
    c_start = np.arange(LANES) * CMP_STRIDE
    s_start = np.arange(n_sel) * SEL_BLOCK
    ovl = ((c_start[None, :] < s_start[:, None] + SEL_BLOCK)
           & (c_start[None, :] + CMP_BLOCK > s_start[:, None])
           & (np.arange(LANES)[None, :] < n_cmp)).astype(np.float32)
    key_blk = np.arange(s) // SEL_BLOCK
    expand = np.where(key_blk[None, :] == np.arange(LANES)[:, None], NEG, 0.0).astype(np.float32)
    expand = expand.reshape(LANES, s // TQ, TQ).transpose(1, 0, 2)
    return jnp.asarray(ovl, BF16), jnp.asarray(expand, BF16)


def _nsa_kernel(q_ref, kc_ref, vc_ref, ks_ref, vs_ref, kw_ref, vw_ref, gate_ref,
                bc_ref, bw_ref, bs_ref, ovl_ref, exp_ref, gn_ref, o_ref,
                qs_ref, s_ref, m_ref, l_ref, acc_ref, uns_ref, oc_ref, os_ref, ob_ref):
    i = pl.program_id(1)
    rows = GQA * TQ
    n_sel = ovl_ref.shape[0]
    n_band = WINDOW // TQ + 1
    heads = range(HKV)
    nt_dims = (((1,), (1,)), ((), ()))

    for hd in range(NSA_HEADS):
        qs_ref[hd // GQA, (hd % GQA) * TQ:(hd % GQA + 1) * TQ, :] = q_ref[0, :, hd * LANES:(hd + 1) * LANES]

    def qs(h):
        return qs_ref[h]

    def chunk_rows(c):
        return pl.ds(pl.multiple_of(c * TQ, TQ), TQ)

    def paired_loop(n, body):
        def pair(j, carry):
            body(2 * j)
            body(2 * j + 1)
            return carry

        lax.fori_loop(0, n // 2, pair, 0)

        @pl.when(n % 2 == 1)
        def _():
            body(n - 1)

    def row_max_to_lanes():
        for h in heads:
            m_ref[h] = jnp.broadcast_to(jnp.max(m_ref[h], axis=-1, keepdims=True), (rows, TQ))

    def normalise(acc, lsum):
        return acc / jnp.maximum(jnp.sum(lsum, axis=-1, keepdims=True), 1e-30)

    for h in heads:
        bias_c = bc_ref[h].reshape(rows, LANES)
        sc = lax.dot_general(qs(h), kc_ref[0], nt_dims, preferred_element_type=F32) + bias_c
        mc = jnp.max(sc, axis=-1, keepdims=True)
        pc = jnp.where(bias_c > 0.5 * NEG, jnp.exp2(sc - mc), 0.0)
        pc = pc / jnp.maximum(jnp.sum(pc, axis=-1, keepdims=True), 1e-30)
        oc_ref[h] = jnp.dot(pc.astype(BF16), vc_ref[0], preferred_element_type=F32)

        psum = jnp.sum(pc.reshape(GQA, TQ, LANES), axis=0)
        p_hi = psum.astype(BF16)
        p_lo = (psum - p_hi.astype(F32)).astype(BF16)
        imp = (lax.dot_general(ovl_ref[...], p_hi, nt_dims, preferred_element_type=F32)
               + lax.dot_general(ovl_ref[...], p_lo, nt_dims, preferred_element_type=F32))
        blk = lax.broadcasted_iota(jnp.int32, (n_sel, TQ), 0)
        tok = lax.broadcasted_iota(jnp.int32, (n_sel, TQ), 1) + i * TQ
        blk_of_t = tok // SEL_BLOCK
        forced = (blk == 0) | (blk == blk_of_t) | (blk == blk_of_t - 1)
        score = jnp.where(forced, FORCED_SCORE, jnp.where(blk <= blk_of_t, imp, -1.0))
        rank = jnp.zeros((n_sel, TQ), F32)
        for mm in range(n_sel):
            sm = score[mm:mm + 1, :]
            ahead = (sm > score) | ((sm == score) & (blk > mm))
            rank = rank + jnp.where(ahead, 1.0, 0.0)
        unsel_t = jnp.where(rank < min(SEL_TOP, n_sel), 0.0, 1.0)
        unsel_t = jnp.concatenate([unsel_t, jnp.zeros((LANES - n_sel, TQ), F32)], axis=0).astype(BF16)
        eye = (lax.broadcasted_iota(jnp.int32, (TQ, TQ), 0)
               == lax.broadcasted_iota(jnp.int32, (TQ, TQ), 1)).astype(BF16)
        uns_ref[h] = lax.dot_general(eye, unsel_t, nt_dims, preferred_element_type=F32).astype(BF16)

    def sel_scores(c, bias_cols):
        k = ks_ref[0, chunk_rows(c), :]
        for h in heads:
            madd = jnp.dot(uns_ref[h], exp_ref[c], preferred_element_type=F32)
            s = lax.dot_general(qs(h), k, nt_dims, preferred_element_type=F32)
            s = (s.reshape(GQA, TQ, TQ) + madd[None]).reshape(rows, TQ)
            if bias_cols is not None:
                s = s + bs_ref[h, :, bias_cols:bias_cols + TQ]
            s_ref[h, c] = s
            m_ref[h] = jnp.maximum(m_ref[h], s)

    def softmax_pv(c, v):
        for h in heads:
            p = jnp.exp2(s_ref[h, c] - m_ref[h])
            l_ref[h] += p
            acc_ref[h] += jnp.dot(p.astype(BF16), v, preferred_element_type=F32)

    m_ref[...] = jnp.full(m_ref.shape, NEG, F32)
    paired_loop(jnp.maximum(i - 1, 0), lambda c: sel_scores(c, None))

    @pl.when(i >= 1)
    def _():
        sel_scores(i - 1, 0)

    sel_scores(i, TQ)
    row_max_to_lanes()
    l_ref[...] = jnp.zeros(l_ref.shape, F32)
    acc_ref[...] = jnp.zeros(acc_ref.shape, F32)
    paired_loop(i + 1, lambda c: softmax_pv(c, vs_ref[0, chunk_rows(c), :]))
    for h in heads:
        os_ref[h] = normalise(acc_ref[h], l_ref[h])

    m_ref[...] = jnp.full(m_ref.shape, NEG, F32)
    for jj in range(n_band):
        c = i - (n_band - 1) + jj
        k = kw_ref[0, chunk_rows(jnp.maximum(c, 0)), :]
        for h in heads:
            s = lax.dot_general(qs(h), k, nt_dims, preferred_element_type=F32)
            s = s + bw_ref[h, :, jj * TQ:(jj + 1) * TQ] + jnp.where(c >= 0, 0.0, NEG)
            s_ref[h, jj] = s
            m_ref[h] = jnp.maximum(m_ref[h], s)
    row_max_to_lanes()
    o_win = []
    for h in heads:
        acc = lsum = None
        for jj in range(n_band):
            v = vw_ref[0, chunk_rows(jnp.maximum(i - (n_band - 1) + jj, 0)), :]
            p = jnp.exp2(s_ref[h, jj] - m_ref[h])
            pv = jnp.dot(p.astype(BF16), v, preferred_element_type=F32)
            acc = pv if acc is None else acc + pv
            lsum = p if lsum is None else lsum + p
        o_win.append(normalise(acc, lsum))

    gates = gate_ref[0]
    for hd in range(NSA_HEADS):
        h, g = hd // GQA, hd % GQA
        r = slice(g * TQ, (g + 1) * TQ)
        o = (gates[:, hd:hd + 1] * oc_ref[h, r, :]
             + gates[:, NSA_HEADS + hd:NSA_HEADS + hd + 1] * os_ref[h, r, :]
             + gates[:, 2 * NSA_HEADS + hd:2 * NSA_HEADS + hd + 1] * o_win[h][r])
        ob_ref[:, hd * DH:(hd + 1) * DH] = o[:, h * DH:(h + 1) * DH]

    o = ob_ref[...]
    ms = jnp.mean(o * o, axis=-1, keepdims=True)
    o_ref[0] = (o * lax.rsqrt(ms + EPS) * gn_ref[...]).astype(BF16)


def _nsa(q, kc, vc, ks, vs, kw, vw, gates, tables, attn_out_norm):
    b, s, _ = q.shape
    bias_c, bias_w, bias_s, ovl, expand = tables
    gn = attn_out_norm.reshape(1, NSA_WIDTH)
    full = lambda a: pl.BlockSpec(a.shape, lambda bi, i: (0,) * a.ndim)
    seq = pl.BlockSpec((1, s, LANES), lambda bi, i: (bi, 0, 0))
    cmp = pl.BlockSpec((1,) + kc.shape[1:], lambda bi, i: (bi, 0, 0))
    rows = GQA * TQ
    return pl.pallas_call(
        _nsa_kernel,
        out_shape=jax.ShapeDtypeStruct((b, s, NSA_WIDTH), BF16),
        grid=(b, s // TQ),
        in_specs=[pl.BlockSpec((1, TQ, NSA_HEADS * LANES), lambda bi, i: (bi, i, 0)),
                  cmp, cmp, seq, seq, seq, seq,
                  pl.BlockSpec((1, TQ, LANES), lambda bi, i: (bi, i, 0)),
                  pl.BlockSpec((HKV, GQA, TQ, LANES), lambda bi, i: (0, 0, i, 0)),
                  full(bias_w), full(bias_s), full(ovl), full(expand), full(gn)],
        out_specs=pl.BlockSpec((1, TQ, NSA_WIDTH), lambda bi, i: (bi, i, 0)),
        scratch_shapes=[pltpu.VMEM((HKV, rows, LANES), BF16),
                        pltpu.VMEM((HKV, s // TQ, rows, TQ), F32),
                        pltpu.VMEM((HKV, rows, TQ), F32),
                        pltpu.VMEM((HKV, rows, TQ), F32),
                        pltpu.VMEM((HKV, rows, LANES), F32),
                        pltpu.VMEM((HKV, TQ, LANES), BF16),
                        pltpu.VMEM((HKV, rows, LANES), F32),
                        pltpu.VMEM((HKV, rows, LANES), F32),
                        pltpu.VMEM((TQ, NSA_WIDTH), F32)],
        compiler_params=_cparams("parallel", "arbitrary"),
        name="nsa_attention",
    )(q, kc, vc, ks, vs, kw, vw, gates, bias_c, bias_w, bias_s, ovl, expand, gn)


def _ssd_kernel(xbc_ref, z_ref, dt_ref, cw_ref, cb_ref, dtb_ref, alog_ref, dsk_ref, ng_ref, o_ref,
                xbuf_ref, state_ref, y_ref):
    c = pl.program_id(1)
    L, P, N = SSD_L, SSD_P, SSD_N
    hpg = SSD_HEADS // SSD_GROUPS
    pad = 8

    @pl.when(c == 0)
    def _():
        xbuf_ref[0:pad, :] = jnp.zeros((pad, SSD_CONV_DIM), F32)
        state_ref[...] = jnp.zeros(state_ref.shape, F32)

    xbuf_ref[pad:pad + L, :] = xbc_ref[0]
    conv = cb_ref[...]
    for k in range(SSD_CONV):
        shift = SSD_CONV - 1 - k
        conv = conv + xbuf_ref[pad - shift:pad - shift + L, :] * cw_ref[k:k + 1, :]
    xbuf_ref[0:pad, :] = xbuf_ref[L:L + pad, :]
    xa = _silu(conv)
    xs = xa[:, :SSD_WIDTH]
    bm = xa[:, SSD_WIDTH:SSD_WIDTH + SSD_GROUPS * N]
    cm = xa[:, SSD_WIDTH + SSD_GROUPS * N:]

    dtv = dt_ref[0] + dtb_ref[...]
    dt = jnp.maximum(dtv, 0.0) + jnp.log1p(jnp.exp(-jnp.abs(dtv)))
    a = dt * (-jnp.exp(alog_ref[...]))
    ri = lax.broadcasted_iota(jnp.int32, (L, L), 0)
    ci = lax.broadcasted_iota(jnp.int32, (L, L), 1)
    causal = ri >= ci
    cs = jnp.dot(causal.astype(F32), a, preferred_element_type=F32,
                 precision=lax.Precision.HIGHEST)
    cs_t = cs.T
    nt_dims = (((1,), (1,)), ((), ()))

    for gr in range(SSD_GROUPS):
        b_g = bm[:, gr * N:(gr + 1) * N]
        c_g = cm[:, gr * N:(gr + 1) * N]
        scores = lax.dot_general(c_g.astype(BF16), b_g.astype(BF16), nt_dims, preferred_element_type=F32)
        b_gt = b_g.T
        for hh in range(hpg):
            h = gr * hpg + hh
            cs_col = cs[:, h:h + 1]
            cs_row = cs_t[h:h + 1, :]
            cs_last = cs[L - 1:L, h:h + 1]
            decay = jnp.exp(jnp.where(causal, cs_col - cs_row, NEG))
            xs_h = xs[:, h * P:(h + 1) * P]
            xc = (xs_h * dt[:, h:h + 1]).astype(BF16)
            y = jnp.dot((scores * decay).astype(BF16), xc, preferred_element_type=F32)
            prev = state_ref[h]
            y = y + jnp.dot((c_g * jnp.exp(cs_col)).astype(BF16), prev.astype(BF16),
                            preferred_element_type=F32)
            contrib = jnp.dot((b_gt * jnp.exp(cs_last - cs_row)).astype(BF16), xc,
                              preferred_element_type=F32)
            state_ref[h] = jnp.exp(cs_last) * prev + contrib
            y_ref[:, h * P:(h + 1) * P] = y + xs_h * dsk_ref[:, h * P:(h + 1) * P]

    z = z_ref[0]
    y = y_ref[...] * _silu(z)
    gw = SSD_WIDTH // SSD_GROUPS
    for gr in range(SSD_GROUPS):
        yg = y[:, gr * gw:(gr + 1) * gw]
        ms = jnp.mean(yg * yg, axis=-1, keepdims=True)
        o_ref[0, :, gr * gw:(gr + 1) * gw] = (yg * lax.rsqrt(ms + EPS)
                                              * ng_ref[:, gr * gw:(gr + 1) * gw]).astype(BF16)


def _ssd(xbc, z, dt, conv_w, conv_b, dt_bias, a_log, d_skip, norm_g):
    b, s, _ = xbc.shape
    padl = lambda v: jnp.pad(v, (0, LANES - v.shape[0])).reshape(1, LANES)
    args = (conv_w, conv_b.reshape(1, SSD_CONV_DIM), padl(dt_bias), padl(a_log),
            jnp.repeat(d_skip, SSD_P).reshape(1, SSD_WIDTH), norm_g.reshape(1, SSD_WIDTH))
    full = lambda a: pl.BlockSpec(a.shape, lambda bi, c: (0,) * a.ndim)
    blk = lambda w: pl.BlockSpec((1, SSD_L, w), lambda bi, c: (bi, c, 0))
    return pl.pallas_call(
        _ssd_kernel,
        out_shape=jax.ShapeDtypeStruct((b, s, SSD_WIDTH), BF16),
        grid=(b, s // SSD_L),
        in_specs=[blk(SSD_CONV_DIM), blk(SSD_WIDTH), blk(LANES)] + [full(a) for a in args],
        out_specs=blk(SSD_WIDTH),
        scratch_shapes=[pltpu.VMEM((SSD_L + 8, SSD_CONV_DIM), F32),
                        pltpu.VMEM((SSD_HEADS, SSD_N, SSD_P), F32),
                        pltpu.VMEM((SSD_L, SSD_WIDTH), F32)],
        compiler_params=_cparams("parallel", "arbitrary"),
        name="ssd_mixer",
    )(xbc, z, dt, *args)


def _out_proj_kernel(x_ref, on_ref, os_ref, wo_ref, fg_ref, rw_ref, rb_ref, h_ref, hn_ref, comb_ref, slot_ref):
    h = (x_ref[...]
         + jnp.dot(on_ref[...], wo_ref[0:NSA_WIDTH, :], preferred_element_type=F32)
         + jnp.dot(os_ref[...], wo_ref[NSA_WIDTH:, :], preferred_element_type=F32))
    tm = h.shape[0]
    for j in range(D_MODEL // LANES):
        h_ref[pl.ds(j, tm, stride=D_MODEL // LANES), :] = h[:, j * LANES:(j + 1) * LANES]
    ms = jnp.mean(h * h, axis=-1, keepdims=True)
    hn = h * lax.rsqrt(ms + EPS) * fg_ref[...]
    for j in range(D_MODEL // LANES):
        hn_ref[pl.ds(j, tm, stride=D_MODEL // LANES), :] = hn[:, j * LANES:(j + 1) * LANES]
    logits = jnp.dot(hn, rw_ref[...], preferred_element_type=F32,
                     precision=lax.Precision.HIGHEST) + rb_ref[...]
    lane = lax.broadcasted_iota(jnp.int32, logits.shape, 1)
    work = logits
    picks = []
    for _ in range(TOP_K):
        v = jnp.max(work, axis=-1, keepdims=True)
        idx = jnp.min(jnp.where(work == v, lane, LANES), axis=-1, keepdims=True)
        hit = lane == idx
        picks.append((v, hit))
        work = jnp.where(hit, -3e38, work)
    v0 = picks[0][0]
    es = [jnp.exp(v - v0) for v, _ in picks]
    den = es[0] + es[1] + es[2] + es[3]
    comb = jnp.zeros_like(logits)
    slot = jnp.zeros(logits.shape, jnp.int32)
    for k, (e, (_, hit)) in enumerate(zip(es, picks)):
        comb = comb + jnp.where(hit, e / den, 0.0)
        slot = jnp.where(hit, k + 1, slot)
    comb_ref[...] = comb
    slot_ref[...] = slot


def _out_proj(x2, o_nsa, o_ssd, w_out, ffn_norm, router_w, router_b, tm=512):
    t = x2.shape[0]
    row = lambda w: pl.BlockSpec((tm, w), lambda i: (i, 0))
    full = lambda a: pl.BlockSpec(a.shape, lambda i: (0,) * a.ndim)
    wo = w_out.astype(BF16)
    fg = ffn_norm.reshape(1, D_MODEL)
    rw = jnp.pad(router_w, ((0, 0), (0, LANES - N_EXPERTS)))
    rb = jnp.pad(router_b, (0, LANES - N_EXPERTS), constant_values=NEG).reshape(1, LANES)
    return pl.pallas_call(
        _out_proj_kernel,
        out_shape=[jax.ShapeDtypeStruct((t * H_SLAB, LANES), F32),
                   jax.ShapeDtypeStruct((t * H_SLAB, LANES), F32),
                   jax.ShapeDtypeStruct((t, LANES), F32),
                   jax.ShapeDtypeStruct((t, LANES), jnp.int32)],
        grid=(t // tm,),
        in_specs=[row(D_MODEL), row(NSA_WIDTH), row(SSD_WIDTH), full(wo), full(fg), full(rw), full(rb)],
        out_specs=[pl.BlockSpec((tm * H_SLAB, LANES), lambda i: (i, 0)),
                   pl.BlockSpec((tm * H_SLAB, LANES), lambda i: (i, 0)), row(LANES), row(LANES)],
        compiler_params=_cparams("parallel"),
        name="out_proj_router",
    )(x2, o_nsa, o_ssd, wo, fg, rw, rb)


MOE_CHUNK = 4096
MOE_TILE = 256
ROUTE_TILE = 256


def _route_kernel(comb_ref, slot_ref, dest_ref, wrow_ref, starts_ref, pos_ref):
    tc = comb_ref.shape[0]
    nt_dims = (((1,), (1,)), ((), ()))
    hi = lax.Precision.HIGHEST
    ri = lax.broadcasted_iota(jnp.int32, (ROUTE_TILE, ROUTE_TILE), 0)
    ci = lax.broadcasted_iota(jnp.int32, (ROUTE_TILE, ROUTE_TILE), 1)
    below = (ri > ci).astype(BF16)
    carry = jnp.zeros((1, LANES), F32)
    for j in range(tc // ROUTE_TILE):
        rows = slice(j * ROUTE_TILE, (j + 1) * ROUTE_TILE)
        sel = jnp.where(slot_ref[rows, :] > 0, 1.0, 0.0)
        pos_ref[rows, :] = jnp.dot(below, sel.astype(BF16), preferred_element_type=F32) + carry
        carry = carry + jnp.sum(sel, axis=0, keepdims=True)
    li = lax.broadcasted_iota(jnp.int32, (LANES, LANES), 0)
    lj = lax.broadcasted_iota(jnp.int32, (LANES, LANES), 1)
    before = (li < lj).astype(F32)
    counts = jnp.broadcast_to(carry, (8, LANES))
    starts = jnp.dot(counts, before, preferred_element_type=F32, precision=hi)
    starts_ref[0] = starts.astype(jnp.int32)
    dest = pos_ref[...] + starts[0:1, :]
    ones = jnp.ones((8, LANES), F32)
    slot = slot_ref[...]
    comb = comb_ref[...]
    for k in range(TOP_K):
        hit = slot == k + 1
        d = lax.dot_general(ones, jnp.where(hit, dest, 0.0), nt_dims, preferred_element_type=F32, precision=hi)
        w = lax.dot_general(ones, jnp.where(hit, comb, 0.0), nt_dims, preferred_element_type=F32, precision=hi)
        dest_ref[0, k:k + 1, :] = d[0:1, :].astype(jnp.int32)
        wrow_ref[0, k:k + 1, :] = w[0:1, :]


def _route(comb, slot, tc):
    t = comb.shape[0]
    nch = t // tc
    blk = pl.BlockSpec((tc, LANES), lambda c: (c, 0))
    return pl.pallas_call(
        _route_kernel,
        out_shape=[jax.ShapeDtypeStruct((nch, TOP_K, tc), jnp.int32),
                   jax.ShapeDtypeStruct((nch, TOP_K, tc), F32),
                   jax.ShapeDtypeStruct((nch, 8, LANES), jnp.int32)],
        grid=(nch,),
        in_specs=[blk, blk],
        out_specs=[pl.BlockSpec((1, TOP_K, tc), lambda c: (c, 0, 0)),
                   pl.BlockSpec((1, TOP_K, tc), lambda c: (c, 0, 0)),
                   pl.BlockSpec((1, 8, LANES), lambda c: (c, 0, 0))],
        scratch_shapes=[pltpu.VMEM((tc, LANES), F32)],
        compiler_params=_cparams("parallel"),
        name="moe_route",
    )(comb, slot)


def _moe_kernel(starts_ref, dest_hbm, wrow_hbm, hn_hbm, h_hbm, wgu_ref, bgu_ref, wd_ref, bd_ref, fn_ref, o_hbm,
                x_ref, acc_ref, xs0_ref, xs1_ref, y0_ref, y1_ref, ob_ref, dest_s, wrow_s, tok_s, wt_s, st_s,
                sem, osem):
    c = pl.program_id(0)
    e = pl.program_id(1)
    tc = x_ref.shape[0] // H_SLAB
    n_rows = TOP_K * tc

    def slab(i, n, width):
        return pl.ds(pl.multiple_of(i * width, width), n * width)

    def gather(r0, xs_ref):
        for r in range(MOE_TILE):
            tok = tok_s[jnp.minimum(r0 + r, n_rows - 1)]
            xs_ref[slab(r, 1, H_SLAB), :] = x_ref[slab(tok, 1, H_SLAB), :]

    def ffn(xs_ref, y_ref):
        x = jnp.concatenate([xs_ref[pl.ds(j, MOE_TILE, stride=H_SLAB), :].astype(BF16) for j in range(H_SLAB)],
                            axis=1)
        gu = jnp.dot(x, wgu_ref[0], preferred_element_type=F32) + bgu_ref[0]
        gate = jnp.minimum(gu[:, :D_FF], SWIGLU_LIMIT)
        up = jnp.clip(gu[:, D_FF:], -SWIGLU_LIMIT, SWIGLU_LIMIT)
        act = (up + 1.0) * gate / (1.0 + jnp.exp(-SWIGLU_ALPHA * gate))
        y = jnp.dot(act.astype(BF16), wd_ref[0], preferred_element_type=F32) + bd_ref[0]
        for j in range(H_SLAB):
            y_ref[pl.ds(j, MOE_TILE, stride=H_SLAB), :] = y[:, j * LANES:(j + 1) * LANES]

    def scatter(r0, n_valid, y_ref):
        for g in range(MOE_TILE // SUBLANES):
            rows = []
            for u in range(SUBLANES):
                r = g * SUBLANES + u
                idx = jnp.minimum(r0 + r, n_rows - 1)
                tok = jnp.where(r < n_valid, tok_s[idx], tc)
                rows.append((tok, acc_ref[slab(tok, 1, H_SLAB), :] + wt_s[idx] * y_ref[slab(r, 1, H_SLAB), :]))
            for tok, val in rows:
                acc_ref[slab(tok, 1, H_SLAB), :] = val

    copies = lambda: (pltpu.make_async_copy(hn_hbm.at[slab(c * tc, tc, H_SLAB), :], x_ref, sem.at[0]),
                      pltpu.make_async_copy(h_hbm.at[slab(c * tc, tc, H_SLAB), :],
                                            acc_ref.at[pl.ds(0, tc * H_SLAB), :], sem.at[1]),
                      pltpu.make_async_copy(dest_hbm.at[c], dest_s, sem.at[2]),
                      pltpu.make_async_copy(wrow_hbm.at[c], wrow_s, sem.at[3]))

    @pl.when(e == 0)
    def _():
        for cp in copies():
            cp.start()
        for cp in copies():
            cp.wait()
        acc_ref[pl.ds(tc * H_SLAB, H_SLAB), :] = jnp.zeros((H_SLAB, LANES), F32)

        def invert(j, carry):
            r = dest_s[j]
            tok_s[r] = j & (tc - 1)
            wt_s[r] = wrow_s[j]
            return carry

        lax.fori_loop(0, n_rows, invert, 0, unroll=8)
        st_s[0] = 0
        st_s[1] = 0
        st_s[2] = 0
        for y_ref in (y0_ref, y1_ref):
            y_ref[...] = jnp.zeros(y_ref.shape, F32)
        gather(0, xs0_ref)

    base = starts_ref[c * LANES + e]
    n_e = starts_ref[c * LANES + e + 1] - base
    xs_bufs = (xs0_ref, xs1_ref)
    y_bufs = (y0_ref, y1_ref)

    def tile_step(p, r0_next, prev_r0, prev_nv):
        gather(r0_next, xs_bufs[1 - p])
        scatter(prev_r0, prev_nv, y_bufs[1 - p])
        ffn(xs_bufs[p], y_bufs[p])

    def tile(j, carry):
        r0 = base + j * MOE_TILE
        r0_next = jnp.minimum(r0 + MOE_TILE, base + n_e)
        k = st_s[0]
        prev_r0 = st_s[1]
        prev_nv = st_s[2]
        for p in range(2):
            pl.when(k % 2 == p)(functools.partial(tile_step, p, r0_next, prev_r0, prev_nv))
        st_s[0] = k + 1
        st_s[1] = r0
        st_s[2] = jnp.minimum(n_e - j * MOE_TILE, MOE_TILE)
        return carry

    lax.fori_loop(0, (n_e + MOE_TILE - 1) // MOE_TILE, tile, 0)

    @pl.when(e == pl.num_programs(1) - 1)
    def _():
        for p in range(2):
            pl.when(st_s[0] % 2 == p)(functools.partial(scatter, st_s[1], st_s[2], y_bufs[1 - p]))
        n_groups = tc // MOE_TILE
        out_copy = lambda g, buf: pltpu.make_async_copy(
            ob_ref.at[buf], o_hbm.at[pl.ds(pl.multiple_of(c * tc + g * MOE_TILE, MOE_TILE), MOE_TILE), :],
            osem.at[buf])

        def norm(g, carry):
            buf = g % 2

            @pl.when(g >= 2)
            def _():
                out_copy(g - 2, buf).wait()

            first = pl.multiple_of(g * MOE_TILE * H_SLAB, MOE_TILE * H_SLAB)
            hs = [acc_ref[pl.ds(first + j, MOE_TILE, stride=H_SLAB), :] for j in range(H_SLAB)]
            ss = hs[0] * hs[0]
            for hj in hs[1:]:
                ss = ss + hj * hj
            inv = lax.rsqrt(jnp.sum(ss, axis=-1, keepdims=True) / D_MODEL + EPS)
            for j, hj in enumerate(hs):
                ob_ref[buf, :, j * LANES:(j + 1) * LANES] = hj * inv * fn_ref[:, j * LANES:(j + 1) * LANES]
            out_copy(g, buf).start()
            return carry

        lax.fori_loop(0, n_groups, norm, 0)
        for g in range(max(n_groups - 2, 0), n_groups):
            out_copy(g, g % 2).wait()


def _moe(hnp, comb, slot, h1, w_gate_up, b_gate_up, w_down, b_down, final_norm, tc=MOE_CHUNK):
    t = comb.shape[0]
    tc = min(tc, t)
    assert tc & (tc - 1) == 0 and tc % MOE_TILE == 0
    nch = t // tc
    dest, wrow, starts = _route(comb, slot, tc)
    dest = dest.reshape(nch, TOP_K * tc)
    wrow = wrow.reshape(nch, TOP_K * tc)
    starts = starts[:, 0, :].reshape(nch * LANES)
    wgu = w_gate_up.astype(BF16)
    wd = w_down.astype(BF16)
    bgu = b_gate_up.reshape(N_EXPERTS, 1, 2 * D_FF)
    bd = b_down.reshape(N_EXPERTS, 1, D_MODEL)
    fn = final_norm.reshape(1, D_MODEL)
    anyspace = pl.BlockSpec(memory_space=pl.ANY)
    exp = lambda a: pl.BlockSpec((1,) + a.shape[1:], lambda c, e, st: (e, 0, 0))
    return pl.pallas_call(
        _moe_kernel,
        out_shape=jax.ShapeDtypeStruct((t, D_MODEL), F32),
        grid_spec=pltpu.PrefetchScalarGridSpec(
            num_scalar_prefetch=1,
            grid=(nch, N_EXPERTS),
            in_specs=[anyspace, anyspace, anyspace, anyspace, exp(wgu), exp(bgu), exp(wd), exp(bd),
                      pl.BlockSpec(fn.shape, lambda c, e, st: (0, 0))],
            out_specs=anyspace,
            scratch_shapes=[pltpu.VMEM((tc * H_SLAB, LANES), F32),
                            pltpu.VMEM(((tc + 1) * H_SLAB, LANES), F32),
                            pltpu.VMEM((MOE_TILE * H_SLAB, LANES), F32),
                            pltpu.VMEM((MOE_TILE * H_SLAB, LANES), F32),
                            pltpu.VMEM((MOE_TILE * H_SLAB, LANES), F32),
                            pltpu.VMEM((MOE_TILE * H_SLAB, LANES), F32),
                            pltpu.VMEM((2, MOE_TILE, D_MODEL), F32),
                            pltpu.SMEM((TOP_K * tc,), jnp.int32),
                            pltpu.SMEM((TOP_K * tc,), F32),
                            pltpu.SMEM((TOP_K * tc,), jnp.int32),
                            pltpu.SMEM((TOP_K * tc,), F32),
                            pltpu.SMEM((4,), jnp.int32),
                            pltpu.SemaphoreType.DMA((4,)),
                            pltpu.SemaphoreType.DMA((2,))]),
        compiler_params=_cparams("arbitrary", "arbitrary", vmem=MOE_VMEM_LIMIT),
        name="moe_experts",
    )(starts, dest, wrow, hnp, h1, wgu, bgu, wd, bd, fn)


def kernel(x, attn_norm, w_in, rel_bias, cmp_pos, cmp_w1, cmp_w2, attn_out_norm, conv_w, conv_b,
           dt_bias, a_log, d_skip, ssm_out_norm, w_out, ffn_norm, router_w, router_b,
           w_gate_up, b_gate_up, w_down, b_down, final_norm):
    b, s, d = x.shape
    t = b * s
    depth = w_in.shape[0]
    tables = tuple(_bias_tables(rel_bias, s)) + _sel_tables(s)
    h = x.reshape(t, d)
    for l in range(depth):
        (q, kc_raw, vc_raw, ks, vs, kw, vw, gates, z, xbc, dt) = _in_proj(h, attn_norm[l], _pad_w_in(w_in[l]))
        grp = CMP_STRIDE * HKV * DH
        kc, vc = _compress(kc_raw.reshape(b, s // CMP_STRIDE, grp), vc_raw.reshape(b, s // CMP_STRIDE, grp),
                           _compress_weights(cmp_pos[l], cmp_w1[l], cmp_w2[l]))
        seq = lambda a: a.reshape(b, s, a.shape[-1])
        o_nsa = _nsa(seq(q), kc, vc, seq(ks), seq(vs), seq(kw), seq(vw), seq(gates), tables, attn_out_norm[l])
        o_ssd = _ssd(seq(xbc), seq(z), seq(dt), conv_w[l], conv_b[l], dt_bias[l], a_log[l], d_skip[l],
                     ssm_out_norm[l])
        h1, hnp, comb, slot = _out_proj(h, o_nsa.reshape(t, NSA_WIDTH), o_ssd.reshape(t, SSD_WIDTH), w_out[l],
                                        ffn_norm[l], router_w[l], router_b[l])
        assert depth == 1
        h = _moe(hnp, comb, slot, h1, w_gate_up[l], b_gate_up[l], w_down[l], b_down[l], final_norm)
    return h.reshape(b, s, d)
```

```python
import functools
import math

import numpy as np
import jax
import jax.numpy as jnp
from jax import lax
from jax.experimental import pallas as pl
from jax.experimental.pallas import tpu as pltpu

F32 = jnp.float32
BF16 = jnp.bfloat16

D_MODEL = 1024
NSA_HEADS = 8
HKV = 2
GQA = NSA_HEADS // HKV
DH = 64
NSA_WIDTH = NSA_HEADS * DH
CMP_BLOCK = 32
CMP_STRIDE = 16
SEL_BLOCK = 64
SEL_TOP = 16
WINDOW = 512
TQ = 128
NSA_NB = 2
SSD_HEADS = 8
SSD_P = 64
SSD_WIDTH = SSD_HEADS * SSD_P
SSD_GROUPS = 2
SSD_N = 128
SSD_CONV = 4
SSD_L = 128
SSD_CONV_DIM = SSD_WIDTH + 2 * SSD_GROUPS * SSD_N
N_BUCKETS = 32
MAX_DISTANCE = 128
N_EXPERTS = 32
TOP_K = 4
D_FF = 1024
SWIGLU_LIMIT = 7.0
SWIGLU_ALPHA = 1.702

EPS = 1e-6
NEG = -1e30
FORCED_SCORE = 1e4
LOG2E = 1.4426950408889634
LANES = 128
SUBLANES = 8
H_SLAB = D_MODEL // LANES
VMEM_LIMIT = 56 * 1024 * 1024
MOE_VMEM_LIMIT = (2 * 4096 * D_MODEL * 4 + 2 * 3 * D_MODEL * D_FF * 2 + 12 * 1024 * 1024)


def _cparams(*sem, vmem=VMEM_LIMIT):
    return pltpu.CompilerParams(dimension_semantics=sem, vmem_limit_bytes=vmem)


def _silu(v):
    return v / (1.0 + jnp.exp(-v))


_Q0, _Q1 = 0, NSA_HEADS * LANES
_KV0 = _Q1
_G0 = _KV0 + 6 * LANES
_Z0 = _G0 + LANES
_X0 = _Z0 + SSD_WIDTH
_DT0 = _X0 + SSD_CONV_DIM
_WCOLS = _DT0 + LANES


def _pad_w_in(w_in):
    d = w_in.shape[0]
    nsa_cols = NSA_WIDTH + 6 * HKV * DH + 3 * NSA_HEADS
    wq = w_in[:, :NSA_WIDTH].reshape(d, HKV, GQA, DH)
    zq = jnp.zeros_like(wq)
    q0 = jnp.concatenate([wq[:, 0], zq[:, 0]], axis=-1)
    q1 = jnp.concatenate([zq[:, 1], wq[:, 1]], axis=-1)
    wq_pad = jnp.stack([q0, q1], axis=1).reshape(d, NSA_HEADS * LANES)
    wkv = w_in[:, NSA_WIDTH:NSA_WIDTH + 6 * HKV * DH]
    wg = w_in[:, NSA_WIDTH + 6 * HKV * DH:nsa_cols]
    wg = jnp.pad(wg, ((0, 0), (0, LANES - wg.shape[1])))
    wz = w_in[:, nsa_cols:nsa_cols + SSD_WIDTH]
    wx = w_in[:, nsa_cols + SSD_WIDTH:nsa_cols + SSD_WIDTH + SSD_CONV_DIM]
    wdt = w_in[:, nsa_cols + SSD_WIDTH + SSD_CONV_DIM:]
    wdt = jnp.pad(wdt, ((0, 0), (0, LANES - wdt.shape[1])))
    return jnp.concatenate([wq_pad, wkv, wg, wz, wx, wdt], axis=1).astype(BF16)


def _in_proj_kernel(x_ref, g_ref, w_ref, q_ref, kc_ref, vc_ref, ks_ref, vs_ref, kw_ref, vw_ref,
                    gate_ref, z_ref, xbc_ref, dt_ref):
    x = x_ref[...]
    ms = jnp.mean(x * x, axis=-1, keepdims=True)
    xn = (x * lax.rsqrt(ms + EPS) * g_ref[...]).astype(BF16)

    def seg(lo, hi):
        return jnp.dot(xn, w_ref[:, lo:hi], preferred_element_type=F32)

    q_ref[...] = (seg(_Q0, _Q1) * (DH ** -0.5 * LOG2E)).astype(BF16)
    ones = jnp.ones((x.shape[0], LANES - DH), F32)
    for j, ref in enumerate((kc_ref, vc_ref, ks_ref, vs_ref, kw_ref, vw_ref)):
        kv = seg(_KV0 + j * LANES, _KV0 + (j + 1) * LANES)
        if j in (3, 5):
            for h in range(HKV):
                ref[h] = jnp.concatenate([kv[:, h * DH:(h + 1) * DH], ones], axis=1).astype(BF16)
        else:
            ref[...] = kv.astype(BF16)
    gate_ref[...] = 1.0 / (1.0 + jnp.exp(-seg(_G0, _Z0)))
    z_ref[...] = seg(_Z0, _X0)
    xbc_ref[...] = seg(_X0, _DT0)
    dt_ref[...] = seg(_DT0, _WCOLS)


def _in_proj(x2, attn_norm, w_pad, tm=512):
    t = x2.shape[0]
    row = lambda w: pl.BlockSpec((tm, w), lambda i: (i, 0))
    full = lambda a: pl.BlockSpec(a.shape, lambda i: (0,) * a.ndim)
    g = attn_norm.reshape(1, D_MODEL)
    kv = jax.ShapeDtypeStruct((t, LANES), BF16)
    val = jax.ShapeDtypeStruct((HKV, t, LANES), BF16)
    outs = ([jax.ShapeDtypeStruct((t, NSA_HEADS * LANES), BF16), kv, kv, kv, val, kv, val]
            + [jax.ShapeDtypeStruct((t, LANES), F32),
               jax.ShapeDtypeStruct((t, SSD_WIDTH), F32),
               jax.ShapeDtypeStruct((t, SSD_CONV_DIM), F32),
               jax.ShapeDtypeStruct((t, LANES), F32)])
    spec = lambda s: (row(s.shape[1]) if len(s.shape) == 2
                      else pl.BlockSpec((HKV, tm, LANES), lambda i: (0, i, 0)))
    return pl.pallas_call(
        _in_proj_kernel,
        out_shape=outs,
        grid=(t // tm,),
        in_specs=[row(D_MODEL), full(g), full(w_pad)],
        out_specs=[spec(s) for s in outs],
        compiler_params=_cparams("parallel"),
        name="in_proj",
    )(x2, g, w_pad)


def _compress_weights(cmp_pos, cmp_w1, cmp_w2):
    half = CMP_BLOCK // 2
    eye = jnp.eye(HKV, dtype=F32)
    w1 = cmp_w1.reshape(2, CMP_BLOCK, DH, DH)
    w1big = jnp.einsum('jlde,hk->jlhdke', w1, eye)
    w1lo = w1big[:, :half].reshape(2, half * HKV * DH, HKV * DH).astype(BF16)
    w1hi = w1big[:, half:].reshape(2, half * HKV * DH, HKV * DH).astype(BF16)
    pos = jnp.broadcast_to(cmp_pos[:, :, None, :], (2, CMP_BLOCK, HKV, DH))
    poslo = pos[:, :half].reshape(2, 1, half * HKV * DH)
    poshi = pos[:, half:].reshape(2, 1, half * HKV * DH)
    w2big = jnp.einsum('jde,hk->jhdke', cmp_w2, eye).reshape(2, HKV * DH, HKV * DH).astype(BF16)
    return w1lo, w1hi, poslo, poshi, w2big


def _compress_kernel(kr_ref, vr_ref, w1lo_ref, w1hi_ref, poslo_ref, poshi_ref, w2_ref, kc_ref, vc_ref):
    for j, (src, dst) in enumerate(((kr_ref, kc_ref), (vr_ref, vc_ref))):
        r = src[0].astype(F32)
        a = jnp.dot((r + poslo_ref[j]).astype(BF16), w1lo_ref[j], preferred_element_type=F32)
        b = jnp.dot((r + poshi_ref[j]).astype(BF16), w1hi_ref[j], preferred_element_type=F32)
        hid = a + pltpu.roll(b, b.shape[0] - 1, 0)
        out = jnp.dot(_silu(hid).astype(BF16), w2_ref[j], preferred_element_type=F32)
        if j == 0:
            dst[0] = out.astype(BF16)
        else:
            dst[0, 0] = out.astype(BF16)
            dst[0, 1] = pltpu.roll(out, DH, 1).astype(BF16)


def _compress(kr, vr, cw):
    b, ng, width = kr.shape
    w1lo, w1hi, poslo, poshi, w2big = cw
    full = lambda a: pl.BlockSpec(a.shape, lambda i: (0,) * a.ndim)
    bspec = pl.BlockSpec((1, ng, width), lambda i: (i, 0, 0))
    return pl.pallas_call(
        _compress_kernel,
        out_shape=[jax.ShapeDtypeStruct((b, ng, HKV * DH), BF16),
                   jax.ShapeDtypeStruct((b, HKV, ng, HKV * DH), BF16)],
        grid=(b,),
        in_specs=[bspec, bspec, full(w1lo), full(w1hi), full(poslo), full(poshi), full(w2big)],
        out_specs=[pl.BlockSpec((1, ng, HKV * DH), lambda i: (i, 0, 0)),
                   pl.BlockSpec((1, HKV, ng, HKV * DH), lambda i: (i, 0, 0, 0))],
        compiler_params=_cparams("parallel"),
        name="nsa_compress",
    )(kr, vr, w1lo, w1hi, poslo, poshi, w2big)


def _bucket_thresholds():
    d = np.arange(MAX_DISTANCE + 1)
    max_exact = N_BUCKETS // 2
    nf = np.maximum(d, max_exact).astype(np.float32)
    large = max_exact + (np.log(nf / np.float32(max_exact)) / np.float32(math.log(MAX_DISTANCE / max_exact))
                         * np.float32(N_BUCKETS - max_exact)).astype(np.int32)
    bucket = np.where(d < max_exact, d, np.minimum(large, N_BUCKETS - 1))
    assert np.all(np.diff(bucket) >= 0) and bucket[MAX_DISTANCE] == N_BUCKETS - 1
    return [int(np.argmax(bucket >= k)) for k in range(N_BUCKETS)]


def _bias_kernel(rb_ref, bc_ref, bw_ref, bs_ref, *, n_cmp):
    i = pl.program_id(0)
    thr = _bucket_thresholds()

    def table(dist, valid, hd, shift):
        v = jnp.full(dist.shape, rb_ref[0, hd], F32)
        for k in range(1, N_BUCKETS):
            v = jnp.where(dist >= thr[k], rb_ref[k, hd], v)
        return jnp.where(valid, (v - shift) * LOG2E, NEG)

    row = lax.broadcasted_iota(jnp.int32, (TQ, LANES), 0)
    col = lax.broadcasted_iota(jnp.int32, (TQ, LANES), 1)
    dist_c = i * TQ + row - (col * CMP_STRIDE + CMP_BLOCK - 1)
    valid_c = (dist_c >= 0) & (col < n_cmp)
    for hd in range(NSA_HEADS):
        bc_ref[hd // GQA, hd % GQA] = table(dist_c, valid_c, hd, 0.0)

    @pl.when(i == 0)
    def _():
        band = WINDOW + TQ
        qi_w = lax.broadcasted_iota(jnp.int32, (TQ, band), 0)
        dist_w = qi_w + WINDOW - lax.broadcasted_iota(jnp.int32, (TQ, band), 1)
        valid_w = (dist_w >= 0) & (dist_w < WINDOW)
        qi_s = lax.broadcasted_iota(jnp.int32, (TQ, 2 * TQ), 0)
        dist_s = qi_s + TQ - lax.broadcasted_iota(jnp.int32, (TQ, 2 * TQ), 1)
        for hd in range(NSA_HEADS):
            k, g = hd // GQA, hd % GQA
            bw_ref[k, g * TQ:(g + 1) * TQ, :] = table(dist_w, valid_w, hd, 0.0)
            bs_ref[k, g * TQ:(g + 1) * TQ, :] = table(dist_s, dist_s >= 0, hd, rb_ref[N_BUCKETS - 1, hd])


def _bias_tables(rel_bias, s):
    n_cmp = (s - CMP_BLOCK) // CMP_STRIDE + 1
    assert n_cmp < LANES and TQ >= MAX_DISTANCE
    band = WINDOW + TQ
    return pl.pallas_call(
        functools.partial(_bias_kernel, n_cmp=n_cmp),
        out_shape=[jax.ShapeDtypeStruct((HKV, GQA, s, LANES), F32),
                   jax.ShapeDtypeStruct((HKV, GQA * TQ, band), F32),
                   jax.ShapeDtypeStruct((HKV, GQA * TQ, 2 * TQ), F32)],
        grid=(s // TQ,),
        in_specs=[pl.BlockSpec(memory_space=pltpu.SMEM)],
        out_specs=[pl.BlockSpec((HKV, GQA, TQ, LANES), lambda i: (0, 0, i, 0)),
                   pl.BlockSpec((HKV, GQA * TQ, band), lambda i: (0, 0, 0)),
                   pl.BlockSpec((HKV, GQA * TQ, 2 * TQ), lambda i: (0, 0, 0))],
        compiler_params=_cparams("arbitrary"),
        name="nsa_bias_tables",
    )(rel_bias)


def _sel_tables(s):
    n_cmp = (s - CMP_BLOCK) // CMP_STRIDE + 1
    n_sel = s // SEL_BLOCK
    c_start = np.arange(LANES) * CMP_STRIDE
    s_start = np.arange(n_sel) * SEL_BLOCK
    ovl = ((c_start[None, :] < s_start[:, None] + SEL_BLOCK)
           & (c_start[None, :] + CMP_BLOCK > s_start[:, None])
           & (np.arange(LANES)[None, :] < n_cmp)).astype(np.float32)
    key_blk = np.arange(s) // SEL_BLOCK
    expand = np.where(key_blk[None, :] == np.arange(LANES)[:, None], NEG, 0.0).astype(np.float32)
    expand_wide = expand.reshape(LANES, s // (2 * TQ), 2 * TQ).transpose(1, 0, 2)
    expand = expand.reshape(LANES, s // TQ, TQ).transpose(1, 0, 2)
    return jnp.asarray(ovl, BF16), jnp.asarray(expand, BF16), jnp.asarray(expand_wide, BF16)


def _nsa_kernel(q_ref, kc_ref, vc_ref, ks_ref, vs_ref, kw_ref, vw_ref, gate_ref,
                bc_ref, bw_ref, bs_ref, ovl_ref, exp_ref, expw_ref, gn_ref, o_ref,
                qs_ref, s_ref, m_ref, acc_ref, uns_ref, oc_ref, os_ref, ob_ref):
    i = pl.program_id(1)
    rows = GQA * TQ
    n_sel = ovl_ref.shape[0]
    n_band = WINDOW // TQ + 1
    batch = range(NSA_NB)
    units = [(b, h) for b in batch for h in range(HKV)]
    nt_dims = (((1,), (1,)), ((), ()))

    for b in batch:
        for hd in range(NSA_HEADS):
            qs_ref[b, hd // GQA, (hd % GQA) * TQ:(hd % GQA + 1) * TQ, :] = q_ref[b, :, hd * LANES:(hd + 1) * LANES]

    def chunk_rows(c, n=1):
        return pl.ds(pl.multiple_of(c * TQ, TQ), n * TQ)

    def row_max_to_lanes():
        for u in units:
            m_ref[u] = jnp.broadcast_to(jnp.max(m_ref[u], axis=-1, keepdims=True), (rows, TQ))

    def normalise(acc):
        row_sum = pltpu.roll(acc, DH, 1)
        return acc / jnp.maximum(row_sum, 1e-30)

    for b, h in units:
        bias_c = bc_ref[h].reshape(rows, LANES)
        sc = lax.dot_general(qs_ref[b, h], kc_ref[b], nt_dims, preferred_element_type=F32) + bias_c
        mc = jnp.max(sc, axis=-1, keepdims=True)
        pc = jnp.where(bias_c > 0.5 * NEG, jnp.exp2(sc - mc), 0.0)
        pc = pc / jnp.maximum(jnp.sum(pc, axis=-1, keepdims=True), 1e-30)
        oc_ref[b, h] = jnp.dot(pc.astype(BF16), vc_ref[b, h], preferred_element_type=F32)

        psum = jnp.sum(pc.reshape(GQA, TQ, LANES), axis=0)
        p_hi = psum.astype(BF16)
        p_lo = (psum - p_hi.astype(F32)).astype(BF16)
        imp = (lax.dot_general(ovl_ref[...], p_hi, nt_dims, preferred_element_type=F32)
               + lax.dot_general(ovl_ref[...], p_lo, nt_dims, preferred_element_type=F32))
        blk = lax.broadcasted_iota(jnp.int32, (n_sel, TQ), 0)
        tok = lax.broadcasted_iota(jnp.int32, (n_sel, TQ), 1) + i * TQ
        blk_of_t = tok // SEL_BLOCK
        forced = (blk == 0) | (blk == blk_of_t) | (blk == blk_of_t - 1)
        score = jnp.where(forced, FORCED_SCORE, jnp.where(blk <= blk_of_t, imp, -1.0))
        rank = jnp.zeros((n_sel, TQ), F32)
        for mm in range(n_sel):
            sm = score[mm:mm + 1, :]
            ahead = (sm > score) | ((sm == score) & (blk > mm))
            rank = rank + jnp.where(ahead, 1.0, 0.0)
        unsel_t = jnp.where(rank < min(SEL_TOP, n_sel), 0.0, 1.0)
        unsel_t = jnp.concatenate([unsel_t, jnp.zeros((LANES - n_sel, TQ), F32)], axis=0).astype(BF16)
        eye = (lax.broadcasted_iota(jnp.int32, (TQ, TQ), 0)
               == lax.broadcasted_iota(jnp.int32, (TQ, TQ), 1)).astype(BF16)
        uns_ref[b, h] = lax.dot_general(eye, unsel_t, nt_dims, preferred_element_type=F32).astype(BF16)

    def sel_scores(c, bias_cols):
        for b in batch:
            k = ks_ref[b, chunk_rows(c), :]
            for h in range(HKV):
                madd = jnp.dot(uns_ref[b, h], exp_ref[c], preferred_element_type=F32)
                s = lax.dot_general(qs_ref[b, h], k, nt_dims, preferred_element_type=F32)
                s = (s.reshape(GQA, TQ, TQ) + madd[None]).reshape(rows, TQ)
                if bias_cols is not None:
                    s = s + bs_ref[h, :, bias_cols:bias_cols + TQ]
                s_ref[b, h, c] = s
                m_ref[b, h] = jnp.maximum(m_ref[b, h], s)

    def sel_scores_wide(j):
        for b in batch:
            k = ks_ref[b, chunk_rows(2 * j, 2), :]
            for h in range(HKV):
                madd = jnp.dot(uns_ref[b, h], expw_ref[j], preferred_element_type=F32)
                s = lax.dot_general(qs_ref[b, h], k, nt_dims, preferred_element_type=F32)
                s = (s.reshape(GQA, TQ, 2 * TQ) + madd[None]).reshape(rows, 2 * TQ)
                s_ref[b, h, 2 * j] = s[:, :TQ]
                s_ref[b, h, 2 * j + 1] = s[:, TQ:]
                m_ref[b, h] = jnp.maximum(m_ref[b, h], jnp.maximum(s[:, :TQ], s[:, TQ:]))

    def softmax_pv(c):
        for b, h in units:
            p = jnp.exp2(s_ref[b, h, c] - m_ref[b, h])
            acc_ref[b, h] += jnp.dot(p.astype(BF16), vs_ref[h, b, chunk_rows(c), :], preferred_element_type=F32)

    def softmax_pv_wide(j):
        for b, h in units:
            m = m_ref[b, h]
            p = jnp.concatenate([jnp.exp2(s_ref[b, h, 2 * j] - m), jnp.exp2(s_ref[b, h, 2 * j + 1] - m)], axis=1)
            acc_ref[b, h] += jnp.dot(p.astype(BF16), vs_ref[h, b, chunk_rows(2 * j, 2), :],
                                     preferred_element_type=F32)

    def loop(n, body):
        lax.fori_loop(0, n, lambda j, carry: (body(j), carry)[1], 0)

    m_ref[...] = jnp.full(m_ref.shape, NEG, F32)
    n_far = jnp.maximum(i - 1, 0)
    loop(n_far // 2, sel_scores_wide)
    pl.when(n_far % 2 == 1)(lambda: sel_scores(n_far - 1, None))
    pl.when(i >= 1)(lambda: sel_scores(i - 1, 0))
    sel_scores(i, TQ)
    row_max_to_lanes()
    acc_ref[...] = jnp.zeros(acc_ref.shape, F32)
    loop((i + 1) // 2, softmax_pv_wide)
    pl.when((i + 1) % 2 == 1)(lambda: softmax_pv(i))
    for u in units:
        os_ref[u] = normalise(acc_ref[u])

    band = WINDOW + TQ
    col = lax.broadcasted_iota(jnp.int32, (1, band), 1)
    before_start = jnp.where(col + i * TQ >= WINDOW, 0.0, NEG)
    o_win = {}
    for b, h in units:
        k = kw_ref[b, chunk_rows(i, n_band), :]
        s = lax.dot_general(qs_ref[b, h], k, nt_dims, preferred_element_type=F32) + bw_ref[h] + before_start
        p = jnp.exp2(s - jnp.max(s, axis=-1, keepdims=True))
        o_win[b, h] = normalise(jnp.dot(p.astype(BF16), vw_ref[h, b, chunk_rows(i, n_band), :],
                                        preferred_element_type=F32))

    for b in batch:
        gates = gate_ref[b]
        for hd in range(NSA_HEADS):
            h, g = hd // GQA, hd % GQA
            r = slice(g * TQ, (g + 1) * TQ)
            o = (gates[:, hd:hd + 1] * oc_ref[b, h, r, :]
                 + gates[:, NSA_HEADS + hd:NSA_HEADS + hd + 1] * os_ref[b, h, r, :]
                 + gates[:, 2 * NSA_HEADS + hd:2 * NSA_HEADS + hd + 1] * o_win[b, h][r])
            ob_ref[b, :, hd * DH:(hd + 1) * DH] = o[:, :DH]
        o = ob_ref[b]
        ms = jnp.mean(o * o, axis=-1, keepdims=True)
        o_ref[b] = (o * lax.rsqrt(ms + EPS) * gn_ref[...]).astype(BF16)


def _nsa(q, kc, vc, ks, vs, kw, vw, gates, tables, attn_out_norm):
    b, s, _ = q.shape
    bias_c, bias_w, bias_s, ovl, expand, expand_wide = tables
    gn = attn_out_norm.reshape(1, NSA_WIDTH)
    nb = NSA_NB
    assert b % nb == 0 and (s // TQ) % 2 == 0
    full = lambda a: pl.BlockSpec(a.shape, lambda bi, i: (0,) * a.ndim)
    tile = lambda w: pl.BlockSpec((nb, TQ, w), lambda bi, i: (bi, i, 0))
    kseq = lambda a: pl.BlockSpec((nb,) + a.shape[1:], lambda bi, i: (bi, 0, 0))
    vseq = lambda a: pl.BlockSpec((HKV, nb) + a.shape[2:], lambda bi, i: (0, bi, 0, 0))
    assert kw.shape[1] == s + WINDOW and vw.shape[2] == s + WINDOW
    rows = GQA * TQ
    unit = (nb, HKV)
    return pl.pallas_call(
        _nsa_kernel,
        out_shape=jax.ShapeDtypeStruct((b, s, NSA_WIDTH), BF16),
        grid=(b // nb, s // TQ),
        in_specs=[tile(NSA_HEADS * LANES),
                  pl.BlockSpec((nb,) + kc.shape[1:], lambda bi, i: (bi, 0, 0)),
                  pl.BlockSpec((nb,) + vc.shape[1:], lambda bi, i: (bi, 0, 0, 0)),
                  kseq(ks), vseq(vs), kseq(kw), vseq(vw), tile(LANES),
                  pl.BlockSpec((HKV, GQA, TQ, LANES), lambda bi, i: (0, 0, i, 0)),
                  full(bias_w), full(bias_s), full(ovl), full(expand), full(expand_wide), full(gn)],
        out_specs=tile(NSA_WIDTH),
        scratch_shapes=[pltpu.VMEM(unit + (rows, LANES), BF16),
                        pltpu.VMEM(unit + (s // TQ, rows, TQ), F32),
                        pltpu.VMEM(unit + (rows, TQ), F32),
                        pltpu.VMEM(unit + (rows, LANES), F32),
                        pltpu.VMEM(unit + (TQ, LANES), BF16),
                        pltpu.VMEM(unit + (rows, LANES), F32),
                        pltpu.VMEM(unit + (rows, LANES), F32),
                        pltpu.VMEM((nb, TQ, NSA_WIDTH), F32)],
        compiler_params=_cparams("parallel", "arbitrary"),
        name="nsa_attention",
    )(q, kc, vc, ks, vs, kw, vw, gates, bias_c, bias_w, bias_s, ovl, expand, expand_wide, gn)


def _ssd_kernel(xbc_ref, z_ref, dt_ref, cw_ref, cb_ref, dtb_ref, alog_ref, dsk_ref, ng_ref, o_ref,
                xbuf_ref, state_ref, y_ref):
    c = pl.program_id(1)
    L, P, N = SSD_L, SSD_P, SSD_N
    hpg = SSD_HEADS // SSD_GROUPS
    pad = 8

    @pl.when(c == 0)
    def _():
        xbuf_ref[0:pad, :] = jnp.zeros((pad, SSD_CONV_DIM), F32)
        state_ref[...] = jnp.zeros(state_ref.shape, F32)

    xbuf_ref[pad:pad + L, :] = xbc_ref[0]
    conv = cb_ref[...]
    for k in range(SSD_CONV):
        shift = SSD_CONV - 1 - k
        conv = conv + xbuf_ref[pad - shift:pad - shift + L, :] * cw_ref[k:k + 1, :]
    xbuf_ref[0:pad, :] = xbuf_ref[L:L + pad, :]
    xa = _silu(conv)
    xs = xa[:, :SSD_WIDTH]
    bm = xa[:, SSD_WIDTH:SSD_WIDTH + SSD_GROUPS * N]
    cm = xa[:, SSD_WIDTH + SSD_GROUPS * N:]

    dtv = dt_ref[0] + dtb_ref[...]
    dt = jnp.maximum(dtv, 0.0) + jnp.log1p(jnp.exp(-jnp.abs(dtv)))
    a = dt * (-jnp.exp(alog_ref[...]))
    ri = lax.broadcasted_iota(jnp.int32, (L, L), 0)
    ci = lax.broadcasted_iota(jnp.int32, (L, L), 1)
    causal = ri >= ci
    cs = jnp.dot(causal.astype(F32), a, preferred_element_type=F32,
                 precision=lax.Precision.HIGHEST)
    cs_t = cs.T
    nt_dims = (((1,), (1,)), ((), ()))

    for gr in range(SSD_GROUPS):
        b_g = bm[:, gr * N:(gr + 1) * N]
        c_g = cm[:, gr * N:(gr + 1) * N]
        scores = lax.dot_general(c_g.astype(BF16), b_g.astype(BF16), nt_dims, preferred_element_type=F32)
        b_gt = b_g.T
        for hh in range(hpg):
            h = gr * hpg + hh
            cs_col = cs[:, h:h + 1]
            cs_row = cs_t[h:h + 1, :]
            cs_last = cs[L - 1:L, h:h + 1]
            decay = jnp.exp(jnp.where(causal, cs_col - cs_row, NEG))
            xs_h = xs[:, h * P:(h + 1) * P]
            xc = (xs_h * dt[:, h:h + 1]).astype(BF16)
            y = jnp.dot((scores * decay).astype(BF16), xc, preferred_element_type=F32)
            prev = state_ref[h]
            y = y + jnp.dot((c_g * jnp.exp(cs_col)).astype(BF16), prev.astype(BF16),
                            preferred_element_type=F32)
            contrib = jnp.dot((b_gt * jnp.exp(cs_last - cs_row)).astype(BF16), xc,
                              preferred_element_type=F32)
            state_ref[h] = jnp.exp(cs_last) * prev + contrib
            y_ref[:, h * P:(h + 1) * P] = y + xs_h * dsk_ref[:, h * P:(h + 1) * P]

    z = z_ref[0]
    y = y_ref[...] * _silu(z)
    gw = SSD_WIDTH // SSD_GROUPS
    for gr in range(SSD_GROUPS):
        yg = y[:, gr * gw:(gr + 1) * gw]
        ms = jnp.mean(yg * yg, axis=-1, keepdims=True)
        o_ref[0, :, gr * gw:(gr + 1) * gw] = (yg * lax.rsqrt(ms + EPS)
                                              * ng_ref[:, gr * gw:(gr + 1) * gw]).astype(BF16)


def _ssd(xbc, z, dt, conv_w, conv_b, dt_bias, a_log, d_skip, norm_g):
    b, s, _ = xbc.shape
    padl = lambda v: jnp.pad(v, (0, LANES - v.shape[0])).reshape(1, LANES)
    args = (conv_w, conv_b.reshape(1, SSD_CONV_DIM), padl(dt_bias), padl(a_log),
            jnp.repeat(d_skip, SSD_P).reshape(1, SSD_WIDTH), norm_g.reshape(1, SSD_WIDTH))
    full = lambda a: pl.BlockSpec(a.shape, lambda bi, c: (0,) * a.ndim)
    blk = lambda w: pl.BlockSpec((1, SSD_L, w), lambda bi, c: (bi, c, 0))
    return pl.pallas_call(
        _ssd_kernel,
        out_shape=jax.ShapeDtypeStruct((b, s, SSD_WIDTH), BF16),
        grid=(b, s // SSD_L),
        in_specs=[blk(SSD_CONV_DIM), blk(SSD_WIDTH), blk(LANES)] + [full(a) for a in args],
        out_specs=blk(SSD_WIDTH),
        scratch_shapes=[pltpu.VMEM((SSD_L + 8, SSD_CONV_DIM), F32),
                        pltpu.VMEM((SSD_HEADS, SSD_N, SSD_P), F32),
                        pltpu.VMEM((SSD_L, SSD_WIDTH), F32)],
        compiler_params=_cparams("parallel", "arbitrary"),
        name="ssd_mixer",
    )(xbc, z, dt, *args)


def _out_proj_kernel(x_ref, on_ref, os_ref, wo_ref, fg_ref, rw_ref, rb_ref, h_ref, hn_ref, comb_ref, slot_ref):
    h = (x_ref[...]
         + jnp.dot(on_ref[...], wo_ref[0:NSA_WIDTH, :], preferred_element_type=F32)
         + jnp.dot(os_ref[...], wo_ref[NSA_WIDTH:, :], preferred_element_type=F32))
    tm = h.shape[0]
    for j in range(D_MODEL // LANES):
        h_ref[pl.ds(j, tm, stride=D_MODEL // LANES), :] = h[:, j * LANES:(j + 1) * LANES]
    ms = jnp.mean(h * h, axis=-1, keepdims=True)
    hn = h * lax.rsqrt(ms + EPS) * fg_ref[...]
    for j in range(D_MODEL // LANES):
        hn_ref[pl.ds(j, tm, stride=D_MODEL // LANES), :] = hn[:, j * LANES:(j + 1) * LANES]
    logits = jnp.dot(hn, rw_ref[...], preferred_element_type=F32,
                     precision=lax.Precision.HIGHEST) + rb_ref[...]
    lane = lax.broadcasted_iota(jnp.int32, logits.shape, 1)
    work = logits
    picks = []
    for _ in range(TOP_K):
        v = jnp.max(work, axis=-1, keepdims=True)
        idx = jnp.min(jnp.where(work == v, lane, LANES), axis=-1, keepdims=True)
        hit = lane == idx
        picks.append((v, hit))
        work = jnp.where(hit, -3e38, work)
    v0 = picks[0][0]
    es = [jnp.exp(v - v0) for v, _ in picks]
    den = es[0] + es[1] + es[2] + es[3]
    comb = jnp.zeros_like(logits)
    slot = jnp.zeros(logits.shape, jnp.int32)
    for k, (e, (_, hit)) in enumerate(zip(es, picks)):
        comb = comb + jnp.where(hit, e / den, 0.0)
        slot = jnp.where(hit, k + 1, slot)
    comb_ref[...] = comb
    slot_ref[...] = slot


def _out_proj(x2, o_nsa, o_ssd, w_out, ffn_norm, router_w, router_b, tm=512):
    t = x2.shape[0]
    row = lambda w: pl.BlockSpec((tm, w), lambda i: (i, 0))
    full = lambda a: pl.BlockSpec(a.shape, lambda i: (0,) * a.ndim)
    wo = w_out.astype(BF16)
    fg = ffn_norm.reshape(1, D_MODEL)
    rw = jnp.pad(router_w, ((0, 0), (0, LANES - N_EXPERTS)))
    rb = jnp.pad(router_b, (0, LANES - N_EXPERTS), constant_values=NEG).reshape(1, LANES)
    return pl.pallas_call(
        _out_proj_kernel,
        out_shape=[jax.ShapeDtypeStruct((t * H_SLAB, LANES), F32),
                   jax.ShapeDtypeStruct((t * H_SLAB, LANES), F32),
                   jax.ShapeDtypeStruct((t, LANES), F32),
                   jax.ShapeDtypeStruct((t, LANES), jnp.int32)],
        grid=(t // tm,),
        in_specs=[row(D_MODEL), row(NSA_WIDTH), row(SSD_WIDTH), full(wo), full(fg), full(rw), full(rb)],
        out_specs=[pl.BlockSpec((tm * H_SLAB, LANES), lambda i: (i, 0)),
                   pl.BlockSpec((tm * H_SLAB, LANES), lambda i: (i, 0)), row(LANES), row(LANES)],
        compiler_params=_cparams("parallel"),
        name="out_proj_router",
    )(x2, o_nsa, o_ssd, wo, fg, rw, rb)


MOE_CHUNK = 4096
MOE_TILE = 256
ROUTE_TILE = 256


def _route_kernel(comb_ref, slot_ref, dest_ref, wrow_ref, starts_ref, pos_ref):
    tc = comb_ref.shape[0]
    nt_dims = (((1,), (1,)), ((), ()))
    hi = lax.Precision.HIGHEST
    ri = lax.broadcasted_iota(jnp.int32, (ROUTE_TILE, ROUTE_TILE), 0)
    ci = lax.broadcasted_iota(jnp.int32, (ROUTE_TILE, ROUTE_TILE), 1)
    below = (ri > ci).astype(BF16)
    carry = jnp.zeros((1, LANES), F32)
    for j in range(tc // ROUTE_TILE):
        rows = slice(j * ROUTE_TILE, (j + 1) * ROUTE_TILE)
        sel = jnp.where(slot_ref[rows, :] > 0, 1.0, 0.0)
        pos_ref[rows, :] = jnp.dot(below, sel.astype(BF16), preferred_element_type=F32) + carry
        carry = carry + jnp.sum(sel, axis=0, keepdims=True)
    li = lax.broadcasted_iota(jnp.int32, (LANES, LANES), 0)
    lj = lax.broadcasted_iota(jnp.int32, (LANES, LANES), 1)
    before = (li < lj).astype(F32)
    counts = jnp.broadcast_to(carry, (8, LANES))
    starts = jnp.dot(counts, before, preferred_element_type=F32, precision=hi)
    starts_ref[0] = starts.astype(jnp.int32)
    dest = pos_ref[...] + starts[0:1, :]
    ones = jnp.ones((8, LANES), F32)
    slot = slot_ref[...]
    comb = comb_ref[...]
    for k in range(TOP_K):
        hit = slot == k + 1
        d = lax.dot_general(ones, jnp.where(hit, dest, 0.0), nt_dims, preferred_element_type=F32, precision=hi)
        w = lax.dot_general(ones, jnp.where(hit, comb, 0.0), nt_dims, preferred_element_type=F32, precision=hi)
        dest_ref[0, k:k + 1, :] = d[0:1, :].astype(jnp.int32)
        wrow_ref[0, k:k + 1, :] = w[0:1, :]


def _route(comb, slot, tc):
    t = comb.shape[0]
    nch = t // tc
    blk = pl.BlockSpec((tc, LANES), lambda c: (c, 0))
    return pl.pallas_call(
        _route_kernel,
        out_shape=[jax.ShapeDtypeStruct((nch, TOP_K, tc), jnp.int32),
                   jax.ShapeDtypeStruct((nch, TOP_K, tc), F32),
                   jax.ShapeDtypeStruct((nch, 8, LANES), jnp.int32)],
        grid=(nch,),
        in_specs=[blk, blk],
        out_specs=[pl.BlockSpec((1, TOP_K, tc), lambda c: (c, 0, 0)),
                   pl.BlockSpec((1, TOP_K, tc), lambda c: (c, 0, 0)),
                   pl.BlockSpec((1, 8, LANES), lambda c: (c, 0, 0))],
        scratch_shapes=[pltpu.VMEM((tc, LANES), F32)],
        compiler_params=_cparams("parallel"),
        name="moe_route",
    )(comb, slot)


def _moe_kernel(starts_ref, dest_hbm, wrow_hbm, hn_hbm, h_hbm, wgu_ref, bgu_ref, wd_ref, bd_ref, fn_ref, o_hbm,
                x_ref, acc_ref, xs0_ref, xs1_ref, y0_ref, y1_ref, ob_ref, dest_s, wrow_s, tok_s, wt_s, st_s,
                sem, osem):
    c = pl.program_id(0)
    e = pl.program_id(1)
    tc = x_ref.shape[0] // H_SLAB
    n_rows = TOP_K * tc

    def slab(i, n, width):
        return pl.ds(pl.multiple_of(i * width, width), n * width)

    def gather(r0, xs_ref):
        for r in range(MOE_TILE):
            tok = tok_s[jnp.minimum(r0 + r, n_rows - 1)]
            xs_ref[slab(r, 1, H_SLAB), :] = x_ref[slab(tok, 1, H_SLAB), :]

    def ffn(xs_ref, y_ref):
        x = jnp.concatenate([xs_ref[pl.ds(j, MOE_TILE, stride=H_SLAB), :].astype(BF16) for j in range(H_SLAB)],
                            axis=1)
        gu = jnp.dot(x, wgu_ref[0], preferred_element_type=F32) + bgu_ref[0]
        gate = jnp.minimum(gu[:, :D_FF], SWIGLU_LIMIT)
        up = jnp.clip(gu[:, D_FF:], -SWIGLU_LIMIT, SWIGLU_LIMIT)
        act = (up + 1.0) * gate / (1.0 + jnp.exp(-SWIGLU_ALPHA * gate))
        y = jnp.dot(act.astype(BF16), wd_ref[0], preferred_element_type=F32) + bd_ref[0]
        for j in range(H_SLAB):
            y_ref[pl.ds(j, MOE_TILE, stride=H_SLAB), :] = y[:, j * LANES:(j + 1) * LANES]

    def scatter(r0, n_valid, y_ref):
        for g in range(MOE_TILE // SUBLANES):
            rows = []
            for u in range(SUBLANES):
                r = g * SUBLANES + u
                idx = jnp.minimum(r0 + r, n_rows - 1)
                tok = jnp.where(r < n_valid, tok_s[idx], tc)
                rows.append((tok, acc_ref[slab(tok, 1, H_SLAB), :] + wt_s[idx] * y_ref[slab(r, 1, H_SLAB), :]))
            for tok, val in rows:
                acc_ref[slab(tok, 1, H_SLAB), :] = val

    copies = lambda: (pltpu.make_async_copy(hn_hbm.at[slab(c * tc, tc, H_SLAB), :], x_ref, sem.at[0]),
                      pltpu.make_async_copy(h_hbm.at[slab(c * tc, tc, H_SLAB), :],
                                            acc_ref.at[pl.ds(0, tc * H_SLAB), :], sem.at[1]),
                      pltpu.make_async_copy(dest_hbm.at[c], dest_s, sem.at[2]),
                      pltpu.make_async_copy(wrow_hbm.at[c], wrow_s, sem.at[3]))

    @pl.when(e == 0)
    def _():
        for cp in copies():
            cp.start()
        for cp in copies():
            cp.wait()
        acc_ref[pl.ds(tc * H_SLAB, H_SLAB), :] = jnp.zeros((H_SLAB, LANES), F32)

        def invert(j, carry):
            r = dest_s[j]
            tok_s[r] = j & (tc - 1)
            wt_s[r] = wrow_s[j]
            return carry

        lax.fori_loop(0, n_rows, invert, 0, unroll=8)
        st_s[0] = 0
        st_s[1] = 0
        st_s[2] = 0
        for y_ref in (y0_ref, y1_ref):
            y_ref[...] = jnp.zeros(y_ref.shape, F32)
        gather(0, xs0_ref)

    base = starts_ref[c * LANES + e]
    n_e = starts_ref[c * LANES + e + 1] - base
    xs_bufs = (xs0_ref, xs1_ref)
    y_bufs = (y0_ref, y1_ref)

    def tile_step(p, r0_next, prev_r0, prev_nv):
        gather(r0_next, xs_bufs[1 - p])
        scatter(prev_r0, prev_nv, y_bufs[1 - p])
        ffn(xs_bufs[p], y_bufs[p])

    def tile(j, carry):
        r0 = base + j * MOE_TILE
        r0_next = jnp.minimum(r0 + MOE_TILE, base + n_e)
        k = st_s[0]
        prev_r0 = st_s[1]
        prev_nv = st_s[2]
        for p in range(2):
            pl.when(k % 2 == p)(functools.partial(tile_step, p, r0_next, prev_r0, prev_nv))
        st_s[0] = k + 1
        st_s[1] = r0
        st_s[2] = jnp.minimum(n_e - j * MOE_TILE, MOE_TILE)
        return carry

    lax.fori_loop(0, (n_e + MOE_TILE - 1) // MOE_TILE, tile, 0)

    @pl.when(e == pl.num_programs(1) - 1)
    def _():
        for p in range(2):
            pl.when(st_s[0] % 2 == p)(functools.partial(scatter, st_s[1], st_s[2], y_bufs[1 - p]))
        n_groups = tc // MOE_TILE
        out_copy = lambda g, buf: pltpu.make_async_copy(
            ob_ref.at[buf], o_hbm.at[pl.ds(pl.multiple_of(c * tc + g * MOE_TILE, MOE_TILE), MOE_TILE), :],
            osem.at[buf])

        def norm(g, carry):
            buf = g % 2

            @pl.when(g >= 2)
            def _():
                out_copy(g - 2, buf).wait()

            first = pl.multiple_of(g * MOE_TILE * H_SLAB, MOE_TILE * H_SLAB)
            hs = [acc_ref[pl.ds(first + j, MOE_TILE, stride=H_SLAB), :] for j in range(H_SLAB)]
            ss = hs[0] * hs[0]
            for hj in hs[1:]:
                ss = ss + hj * hj
            inv = lax.rsqrt(jnp.sum(ss, axis=-1, keepdims=True) / D_MODEL + EPS)
            for j, hj in enumerate(hs):
                ob_ref[buf, :, j * LANES:(j + 1) * LANES] = hj * inv * fn_ref[:, j * LANES:(j + 1) * LANES]
            out_copy(g, buf).start()
            return carry

        lax.fori_loop(0, n_groups, norm, 0)
        for g in range(max(n_groups - 2, 0), n_groups):
            out_copy(g, g % 2).wait()


def _moe(hnp, comb, slot, h1, w_gate_up, b_gate_up, w_down, b_down, final_norm, tc=MOE_CHUNK):
    t = comb.shape[0]
    tc = min(tc, t)
    assert tc & (tc - 1) == 0 and tc % MOE_TILE == 0
    nch = t // tc
    dest, wrow, starts = _route(comb, slot, tc)
    dest = dest.reshape(nch, TOP_K * tc)
    wrow = wrow.reshape(nch, TOP_K * tc)
    starts = starts[:, 0, :].reshape(nch * LANES)
    wgu = w_gate_up.astype(BF16)
    wd = w_down.astype(BF16)
    bgu = b_gate_up.reshape(N_EXPERTS, 1, 2 * D_FF)
    bd = b_down.reshape(N_EXPERTS, 1, D_MODEL)
    fn = final_norm.reshape(1, D_MODEL)
    anyspace = pl.BlockSpec(memory_space=pl.ANY)
    exp = lambda a: pl.BlockSpec((1,) + a.shape[1:], lambda c, e, st: (e, 0, 0))
    return pl.pallas_call(
        _moe_kernel,
        out_shape=jax.ShapeDtypeStruct((t, D_MODEL), F32),
        grid_spec=pltpu.PrefetchScalarGridSpec(
            num_scalar_prefetch=1,
            grid=(nch, N_EXPERTS),
            in_specs=[anyspace, anyspace, anyspace, anyspace, exp(wgu), exp(bgu), exp(wd), exp(bd),
                      pl.BlockSpec(fn.shape, lambda c, e, st: (0, 0))],
            out_specs=anyspace,
            scratch_shapes=[pltpu.VMEM((tc * H_SLAB, LANES), F32),
                            pltpu.VMEM(((tc + 1) * H_SLAB, LANES), F32),
                            pltpu.VMEM((MOE_TILE * H_SLAB, LANES), F32),
                            pltpu.VMEM((MOE_TILE * H_SLAB, LANES), F32),
                            pltpu.VMEM((MOE_TILE * H_SLAB, LANES), F32),
                            pltpu.VMEM((MOE_TILE * H_SLAB, LANES), F32),
                            pltpu.VMEM((2, MOE_TILE, D_MODEL), F32),
                            pltpu.SMEM((TOP_K * tc,), jnp.int32),
                            pltpu.SMEM((TOP_K * tc,), F32),
                            pltpu.SMEM((TOP_K * tc,), jnp.int32),
                            pltpu.SMEM((TOP_K * tc,), F32),
                            pltpu.SMEM((4,), jnp.int32),
                            pltpu.SemaphoreType.DMA((4,)),
                            pltpu.SemaphoreType.DMA((2,))]),
        compiler_params=_cparams("arbitrary", "arbitrary", vmem=MOE_VMEM_LIMIT),
        name="moe_experts",
    )(starts, dest, wrow, hnp, h1, wgu, bgu, wd, bd, fn)


def kernel(x, attn_norm, w_in, rel_bias, cmp_pos, cmp_w1, cmp_w2, attn_out_norm, conv_w, conv_b,
           dt_bias, a_log, d_skip, ssm_out_norm, w_out, ffn_norm, router_w, router_b,
           w_gate_up, b_gate_up, w_down, b_down, final_norm):
    b, s, d = x.shape
    t = b * s
    depth = w_in.shape[0]
    tables = tuple(_bias_tables(rel_bias, s)) + _sel_tables(s)
    h = x.reshape(t, d)
    for l in range(depth):
        (q, kc_raw, vc_raw, ks, vs, kw, vw, gates, z, xbc, dt) = _in_proj(h, attn_norm[l], _pad_w_in(w_in[l]))
        grp = CMP_STRIDE * HKV * DH
        kc, vc = _compress(kc_raw.reshape(b, s // CMP_STRIDE, grp), vc_raw.reshape(b, s // CMP_STRIDE, grp),
                           _compress_weights(cmp_pos[l], cmp_w1[l], cmp_w2[l]))
        seq = lambda a: a.reshape(a.shape[:-2] + (b, s, a.shape[-1]))
        front = lambda a: jnp.pad(a, [(0, 0)] * (a.ndim - 2) + [(WINDOW, 0), (0, 0)])
        o_nsa = _nsa(seq(q), kc, vc, seq(ks), seq(vs), front(seq(kw)), front(seq(vw)), seq(gates), tables,
                     attn_out_norm[l])
        o_ssd = _ssd(seq(xbc), seq(z), seq(dt), conv_w[l], conv_b[l], dt_bias[l], a_log[l], d_skip[l],
                     ssm_out_norm[l])
        h1, hnp, comb, slot = _out_proj(h, o_nsa.reshape(t, NSA_WIDTH), o_ssd.reshape(t, SSD_WIDTH), w_out[l],
                                        ffn_norm[l], router_w[l], router_b[l])
        assert depth == 1
        h = _moe(hnp, comb, slot, h1, w_gate_up[l], b_gate_up[l], w_down[l], b_down[l], final_norm)
    return h.reshape(b, s, d)
```

```python
import functools
import math

import numpy as np
import jax
import jax.numpy as jnp
from jax import lax
from jax.experimental import pallas as pl
from jax.experimental.pallas import tpu as pltpu

F32 = jnp.float32
BF16 = jnp.bfloat16

D_MODEL = 1024
NSA_HEADS = 8
HKV = 2
GQA = NSA_HEADS // HKV
DH = 64
NSA_WIDTH = NSA_HEADS * DH
CMP_BLOCK = 32
CMP_STRIDE = 16
SEL_BLOCK = 64
SEL_TOP = 16
WINDOW = 512
TQ = 128
NSA_NB = 2
SSD_HEADS = 8
SSD_P = 64
SSD_WIDTH = SSD_HEADS * SSD_P
SSD_GROUPS = 2
SSD_N = 128
SSD_CONV = 4
SSD_L = 128
SSD_CONV_DIM = SSD_WIDTH + 2 * SSD_GROUPS * SSD_N
N_BUCKETS = 32
MAX_DISTANCE = 128
N_EXPERTS = 32
TOP_K = 4
D_FF = 1024
SWIGLU_LIMIT = 7.0
SWIGLU_ALPHA = 1.702

EPS = 1e-6
NEG = -1e30
FORCED_SCORE = 1e4
LOG2E = 1.4426950408889634
LANES = 128
SUBLANES = 8
H_SLAB = D_MODEL // LANES
VMEM_LIMIT = 56 * 1024 * 1024
MOE_VMEM_LIMIT = (2 * 4096 * D_MODEL * 4 + 2 * 3 * D_MODEL * D_FF * 2 + 12 * 1024 * 1024)


def _cparams(*sem, vmem=VMEM_LIMIT):
    return pltpu.CompilerParams(dimension_semantics=sem, vmem_limit_bytes=vmem)


def _silu(v):
    return v / (1.0 + jnp.exp(-v))


_Q0, _Q1 = 0, NSA_HEADS * LANES
_KV0 = _Q1
_G0 = _KV0 + 6 * LANES
_Z0 = _G0 + LANES
_X0 = _Z0 + SSD_WIDTH
_DT0 = _X0 + SSD_CONV_DIM
_WCOLS = _DT0 + LANES


def _pad_w_in(w_in):
    d = w_in.shape[0]
    nsa_cols = NSA_WIDTH + 6 * HKV * DH + 3 * NSA_HEADS
    wq = w_in[:, :NSA_WIDTH].reshape(d, HKV, GQA, DH)
    zq = jnp.zeros_like(wq)
    q0 = jnp.concatenate([wq[:, 0], zq[:, 0]], axis=-1)
    q1 = jnp.concatenate([zq[:, 1], wq[:, 1]], axis=-1)
    wq_pad = jnp.stack([q0, q1], axis=1).reshape(d, NSA_HEADS * LANES)
    wkv = w_in[:, NSA_WIDTH:NSA_WIDTH + 6 * HKV * DH]
    wg = w_in[:, NSA_WIDTH + 6 * HKV * DH:nsa_cols]
    wg = jnp.pad(wg, ((0, 0), (0, LANES - wg.shape[1])))
    wz = w_in[:, nsa_cols:nsa_cols + SSD_WIDTH]
    wx = w_in[:, nsa_cols + SSD_WIDTH:nsa_cols + SSD_WIDTH + SSD_CONV_DIM]
    wdt = w_in[:, nsa_cols + SSD_WIDTH + SSD_CONV_DIM:]
    wdt = jnp.pad(wdt, ((0, 0), (0, LANES - wdt.shape[1])))
    return jnp.concatenate([wq_pad, wkv, wg, wz, wx, wdt], axis=1).astype(BF16)


def _in_proj_kernel(x_ref, g_ref, w_ref, q_ref, kc_ref, vc_ref, ks_ref, vs_ref, kw_ref, vw_ref,
                    gate_ref, z_ref, xbc_ref, dt_ref):
    x = x_ref[...]
    ms = jnp.mean(x * x, axis=-1, keepdims=True)
    xn = (x * lax.rsqrt(ms + EPS) * g_ref[...]).astype(BF16)

    def seg(lo, hi):
        return jnp.dot(xn, w_ref[:, lo:hi], preferred_element_type=F32)

    q_ref[...] = (seg(_Q0, _Q1) * (DH ** -0.5 * LOG2E)).astype(BF16)
    ones = jnp.ones((x.shape[0], LANES - DH), F32)
    for j, ref in enumerate((kc_ref, vc_ref, ks_ref, vs_ref, kw_ref, vw_ref)):
        kv = seg(_KV0 + j * LANES, _KV0 + (j + 1) * LANES)
        if j in (3, 5):
            for h in range(HKV):
                ref[h] = jnp.concatenate([kv[:, h * DH:(h + 1) * DH], ones], axis=1).astype(BF16)
        else:
            ref[...] = kv.astype(BF16)
    gate_ref[...] = 1.0 / (1.0 + jnp.exp(-seg(_G0, _Z0)))
    z_ref[...] = seg(_Z0, _X0)
    xbc_ref[...] = seg(_X0, _DT0)
    dt_ref[...] = seg(_DT0, _WCOLS)


def _in_proj(x2, attn_norm, w_pad, tm=512):
    t = x2.shape[0]
    row = lambda w: pl.BlockSpec((tm, w), lambda i: (i, 0))
    full = lambda a: pl.BlockSpec(a.shape, lambda i: (0,) * a.ndim)
    g = attn_norm.reshape(1, D_MODEL)
    kv = jax.ShapeDtypeStruct((t, LANES), BF16)
    val = jax.ShapeDtypeStruct((HKV, t, LANES), BF16)
    outs = ([jax.ShapeDtypeStruct((t, NSA_HEADS * LANES), BF16), kv, kv, kv, val, kv, val]
            + [jax.ShapeDtypeStruct((t, LANES), F32),
               jax.ShapeDtypeStruct((t, SSD_WIDTH), F32),
               jax.ShapeDtypeStruct((t, SSD_CONV_DIM), F32),
               jax.ShapeDtypeStruct((t, LANES), F32)])
    spec = lambda s: (row(s.shape[1]) if len(s.shape) == 2
                      else pl.BlockSpec((HKV, tm, LANES), lambda i: (0, i, 0)))
    return pl.pallas_call(
        _in_proj_kernel,
        out_shape=outs,
        grid=(t // tm,),
        in_specs=[row(D_MODEL), full(g), full(w_pad)],
        out_specs=[spec(s) for s in outs],
        compiler_params=_cparams("parallel"),
        name="in_proj",
    )(x2, g, w_pad)


def _compress_weights(cmp_pos, cmp_w1, cmp_w2):
    half = CMP_BLOCK // 2
    eye = jnp.eye(HKV, dtype=F32)
    w1 = cmp_w1.reshape(2, CMP_BLOCK, DH, DH)
    w1big = jnp.einsum('jlde,hk->jlhdke', w1, eye)
    w1lo = w1big[:, :half].reshape(2, half * HKV * DH, HKV * DH).astype(BF16)
    w1hi = w1big[:, half:].reshape(2, half * HKV * DH, HKV * DH).astype(BF16)
    pos = jnp.broadcast_to(cmp_pos[:, :, None, :], (2, CMP_BLOCK, HKV, DH))
    poslo = pos[:, :half].reshape(2, 1, half * HKV * DH)
    poshi = pos[:, half:].reshape(2, 1, half * HKV * DH)
    w2big = jnp.einsum('jde,hk->jhdke', cmp_w2, eye).reshape(2, HKV * DH, HKV * DH).astype(BF16)
    return w1lo, w1hi, poslo, poshi, w2big


def _compress_kernel(kr_ref, vr_ref, w1lo_ref, w1hi_ref, poslo_ref, poshi_ref, w2_ref, kc_ref, vc_ref):
    for j, (src, dst) in enumerate(((kr_ref, kc_ref), (vr_ref, vc_ref))):
        r = src[0].astype(F32)
        a = jnp.dot((r + poslo_ref[j]).astype(BF16), w1lo_ref[j], preferred_element_type=F32)
        b = jnp.dot((r + poshi_ref[j]).astype(BF16), w1hi_ref[j], preferred_element_type=F32)
        hid = a + pltpu.roll(b, b.shape[0] - 1, 0)
        out = jnp.dot(_silu(hid).astype(BF16), w2_ref[j], preferred_element_type=F32)
        if j == 0:
            dst[0] = out.astype(BF16)
        else:
            dst[0, 0] = out.astype(BF16)
            dst[0, 1] = pltpu.roll(out, DH, 1).astype(BF16)


def _compress(kr, vr, cw):
    b, ng, width = kr.shape
    w1lo, w1hi, poslo, poshi, w2big = cw
    full = lambda a: pl.BlockSpec(a.shape, lambda i: (0,) * a.ndim)
    bspec = pl.BlockSpec((1, ng, width), lambda i: (i, 0, 0))
    return pl.pallas_call(
        _compress_kernel,
        out_shape=[jax.ShapeDtypeStruct((b, ng, HKV * DH), BF16),
                   jax.ShapeDtypeStruct((b, HKV, ng, HKV * DH), BF16)],
        grid=(b,),
        in_specs=[bspec, bspec, full(w1lo), full(w1hi), full(poslo), full(poshi), full(w2big)],
        out_specs=[pl.BlockSpec((1, ng, HKV * DH), lambda i: (i, 0, 0)),
                   pl.BlockSpec((1, HKV, ng, HKV * DH), lambda i: (i, 0, 0, 0))],
        compiler_params=_cparams("parallel"),
        name="nsa_compress",
    )(kr, vr, w1lo, w1hi, poslo, poshi, w2big)


def _bucket_thresholds():
    d = np.arange(MAX_DISTANCE + 1)
    max_exact = N_BUCKETS // 2
    nf = np.maximum(d, max_exact).astype(np.float32)
    large = max_exact + (np.log(nf / np.float32(max_exact)) / np.float32(math.log(MAX_DISTANCE / max_exact))
                         * np.float32(N_BUCKETS - max_exact)).astype(np.int32)
    bucket = np.where(d < max_exact, d, np.minimum(large, N_BUCKETS - 1))
    assert np.all(np.diff(bucket) >= 0) and bucket[MAX_DISTANCE] == N_BUCKETS - 1
    return [int(np.argmax(bucket >= k)) for k in range(N_BUCKETS)]


def _bias_kernel(rb_ref, bc_ref, bw_ref, bs_ref, *, n_cmp):
    i = pl.program_id(0)
    thr = _bucket_thresholds()

    def table(dist, valid, hd, shift):
        v = jnp.full(dist.shape, rb_ref[0, hd], F32)
        for k in range(1, N_BUCKETS):
            v = jnp.where(dist >= thr[k], rb_ref[k, hd], v)
        return jnp.where(valid, (v - shift) * LOG2E, NEG)

    row = lax.broadcasted_iota(jnp.int32, (TQ, LANES), 0)
    col = lax.broadcasted_iota(jnp.int32, (TQ, LANES), 1)
    dist_c = i * TQ + row - (col * CMP_STRIDE + CMP_BLOCK - 1)
    valid_c = (dist_c >= 0) & (col < n_cmp)
    for hd in range(NSA_HEADS):
        bc_ref[hd // GQA, hd % GQA] = table(dist_c, valid_c, hd, 0.0)

    @pl.when(i == 0)
    def _():
        band = WINDOW + TQ
        qi_w = lax.broadcasted_iota(jnp.int32, (TQ, band), 0)
        dist_w = qi_w + WINDOW - lax.broadcasted_iota(jnp.int32, (TQ, band), 1)
        valid_w = (dist_w >= 0) & (dist_w < WINDOW)
        qi_s = lax.broadcasted_iota(jnp.int32, (TQ, 2 * TQ), 0)
        dist_s = qi_s + TQ - lax.broadcasted_iota(jnp.int32, (TQ, 2 * TQ), 1)
        for hd in range(NSA_HEADS):
            k, g = hd // GQA, hd % GQA
            bw_ref[k, g * TQ:(g + 1) * TQ, :] = table(dist_w, valid_w, hd, 0.0)
            bs_ref[k, g * TQ:(g + 1) * TQ, :] = table(dist_s, dist_s >= 0, hd, rb_ref[N_BUCKETS - 1, hd])


def _bias_tables(rel_bias, s):
    n_cmp = (s - CMP_BLOCK) // CMP_STRIDE + 1
    assert n_cmp < LANES and TQ >= MAX_DISTANCE
    band = WINDOW + TQ
    return pl.pallas_call(
        functools.partial(_bias_kernel, n_cmp=n_cmp),
        out_shape=[jax.ShapeDtypeStruct((HKV, GQA, s, LANES), F32),
                   jax.ShapeDtypeStruct((HKV, GQA * TQ, band), F32),
                   jax.ShapeDtypeStruct((HKV, GQA * TQ, 2 * TQ), F32)],
        grid=(s // TQ,),
        in_specs=[pl.BlockSpec(memory_space=pltpu.SMEM)],
        out_specs=[pl.BlockSpec((HKV, GQA, TQ, LANES), lambda i: (0, 0, i, 0)),
                   pl.BlockSpec((HKV, GQA * TQ, band), lambda i: (0, 0, 0)),
                   pl.BlockSpec((HKV, GQA * TQ, 2 * TQ), lambda i: (0, 0, 0))],
        compiler_params=_cparams("arbitrary"),
        name="nsa_bias_tables",
    )(rel_bias)


def _sel_tables(s):
    n_cmp = (s - CMP_BLOCK) // CMP_STRIDE + 1
    n_sel = s // SEL_BLOCK
    c_start = np.arange(LANES) * CMP_STRIDE
    s_start = np.arange(n_sel) * SEL_BLOCK
    ovl = ((c_start[None, :] < s_start[:, None] + SEL_BLOCK)
           & (c_start[None, :] + CMP_BLOCK > s_start[:, None])
           & (np.arange(LANES)[None, :] < n_cmp)).astype(np.float32)
    key_blk = np.arange(s) // SEL_BLOCK
    expand = np.where(key_blk[None, :] == np.arange(LANES)[:, None], NEG, 0.0).astype(np.float32)
    expand_wide = expand.reshape(LANES, s // (2 * TQ), 2 * TQ).transpose(1, 0, 2)
    expand = expand.reshape(LANES, s // TQ, TQ).transpose(1, 0, 2)
    return jnp.asarray(ovl, BF16), jnp.asarray(expand, BF16), jnp.asarray(expand_wide, BF16)


def _nsa_kernel(q_ref, kc_ref, vc_ref, ks_ref, vs_ref, kw_ref, vw_ref, gate_ref,
                bc_ref, bw_ref, bs_ref, ovl_ref, exp_ref, expw_ref, gn_ref, o_ref,
                qs_ref, s_ref, m_ref, acc_ref, uns_ref, oc_ref, os_ref, ob_ref):
    i = pl.program_id(1)
    rows = GQA * TQ
    n_sel = ovl_ref.shape[0]
    n_band = WINDOW // TQ + 1
    batch = range(NSA_NB)
    units = [(b, h) for b in batch for h in range(HKV)]
    nt_dims = (((1,), (1,)), ((), ()))

    for b in batch:
        for hd in range(NSA_HEADS):
            qs_ref[b, hd // GQA, (hd % GQA) * TQ:(hd % GQA + 1) * TQ, :] = q_ref[b, :, hd * LANES:(hd + 1) * LANES]

    def chunk_rows(c, n=1):
        return pl.ds(pl.multiple_of(c * TQ, TQ), n * TQ)

    def row_max_to_lanes():
        for u in units:
            m_ref[u] = jnp.broadcast_to(jnp.max(m_ref[u], axis=-1, keepdims=True), (rows, TQ))

    def normalise(acc):
        row_sum = pltpu.roll(acc, DH, 1)
        return acc / jnp.maximum(row_sum, 1e-30)

    for b, h in units:
        bias_c = bc_ref[h].reshape(rows, LANES)
        sc = lax.dot_general(qs_ref[b, h], kc_ref[b], nt_dims, preferred_element_type=F32) + bias_c
        mc = jnp.max(sc, axis=-1, keepdims=True)
        pc = jnp.where(bias_c > 0.5 * NEG, jnp.exp2(sc - mc), 0.0)
        pc = pc / jnp.maximum(jnp.sum(pc, axis=-1, keepdims=True), 1e-30)
        oc_ref[b, h] = jnp.dot(pc.astype(BF16), vc_ref[b, h], preferred_element_type=F32)

        psum = jnp.sum(pc.reshape(GQA, TQ, LANES), axis=0)
        p_hi = psum.astype(BF16)
        p_lo = (psum - p_hi.astype(F32)).astype(BF16)
        imp = (lax.dot_general(ovl_ref[...], p_hi, nt_dims, preferred_element_type=F32)
               + lax.dot_general(ovl_ref[...], p_lo, nt_dims, preferred_element_type=F32))
        blk = lax.broadcasted_iota(jnp.int32, (n_sel, TQ), 0)
        tok = lax.broadcasted_iota(jnp.int32, (n_sel, TQ), 1) + i * TQ
        blk_of_t = tok // SEL_BLOCK
        forced = (blk == 0) | (blk == blk_of_t) | (blk == blk_of_t - 1)
        score = jnp.where(forced, FORCED_SCORE, jnp.where(blk <= blk_of_t, imp, -1.0))
        rank = jnp.zeros((n_sel, TQ), F32)
        for mm in range(n_sel):
            sm = score[mm:mm + 1, :]
            ahead = (sm > score) | ((sm == score) & (blk > mm))
            rank = rank + jnp.where(ahead, 1.0, 0.0)
        unsel_t = jnp.where(rank < min(SEL_TOP, n_sel), 0.0, 1.0)
        unsel_t = jnp.concatenate([unsel_t, jnp.zeros((LANES - n_sel, TQ), F32)], axis=0).astype(BF16)
        eye = (lax.broadcasted_iota(jnp.int32, (TQ, TQ), 0)
               == lax.broadcasted_iota(jnp.int32, (TQ, TQ), 1)).astype(BF16)
        uns_ref[b, h] = lax.dot_general(eye, unsel_t, nt_dims, preferred_element_type=F32).astype(BF16)

    def sel_scores(c, bias_cols):
        for b in batch:
            k = ks_ref[b, chunk_rows(c), :]
            for h in range(HKV):
                madd = jnp.dot(uns_ref[b, h], exp_ref[c], preferred_element_type=F32)
                s = lax.dot_general(qs_ref[b, h], k, nt_dims, preferred_element_type=F32)
                s = (s.reshape(GQA, TQ, TQ) + madd[None]).reshape(rows, TQ)
                if bias_cols is not None:
                    s = s + bs_ref[h, :, bias_cols:bias_cols + TQ]
                s_ref[b, h, c] = s
                m_ref[b, h] = jnp.maximum(m_ref[b, h], s)

    def sel_scores_wide(j):
        for b in batch:
            k = ks_ref[b, chunk_rows(2 * j, 2), :]
            for h in range(HKV):
                madd = jnp.dot(uns_ref[b, h], expw_ref[j], preferred_element_type=F32)
                s = lax.dot_general(qs_ref[b, h], k, nt_dims, preferred_element_type=F32)
                s = (s.reshape(GQA, TQ, 2 * TQ) + madd[None]).reshape(rows, 2 * TQ)
                s_ref[b, h, 2 * j] = s[:, :TQ]
                s_ref[b, h, 2 * j + 1] = s[:, TQ:]
                m_ref[b, h] = jnp.maximum(m_ref[b, h], jnp.maximum(s[:, :TQ], s[:, TQ:]))

    def softmax_pv(c):
        for b, h in units:
            p = jnp.exp2(s_ref[b, h, c] - m_ref[b, h])
            acc_ref[b, h] += jnp.dot(p.astype(BF16), vs_ref[h, b, chunk_rows(c), :], preferred_element_type=F32)

    def softmax_pv_wide(j):
        for b, h in units:
            m = m_ref[b, h]
            p = jnp.concatenate([jnp.exp2(s_ref[b, h, 2 * j] - m), jnp.exp2(s_ref[b, h, 2 * j + 1] - m)], axis=1)
            acc_ref[b, h] += jnp.dot(p.astype(BF16), vs_ref[h, b, chunk_rows(2 * j, 2), :],
                                     preferred_element_type=F32)

    def loop(n, body):
        lax.fori_loop(0, n, lambda j, carry: (body(j), carry)[1], 0)

    m_ref[...] = jnp.full(m_ref.shape, NEG, F32)
    n_far = jnp.maximum(i - 1, 0)
    loop(n_far // 2, sel_scores_wide)
    pl.when(n_far % 2 == 1)(lambda: sel_scores(n_far - 1, None))
    pl.when(i >= 1)(lambda: sel_scores(i - 1, 0))
    sel_scores(i, TQ)
    row_max_to_lanes()
    acc_ref[...] = jnp.zeros(acc_ref.shape, F32)
    loop((i + 1) // 2, softmax_pv_wide)
    pl.when((i + 1) % 2 == 1)(lambda: softmax_pv(i))
    for u in units:
        os_ref[u] = normalise(acc_ref[u])

    band = WINDOW + TQ
    col = lax.broadcasted_iota(jnp.int32, (1, band), 1)
    before_start = jnp.where(col + i * TQ >= WINDOW, 0.0, NEG)
    o_win = {}
    for b, h in units:
        k = kw_ref[b, chunk_rows(i, n_band), :]
        s = lax.dot_general(qs_ref[b, h], k, nt_dims, preferred_element_type=F32) + bw_ref[h] + before_start
        p = jnp.exp2(s - jnp.max(s, axis=-1, keepdims=True))
        o_win[b, h] = normalise(jnp.dot(p.astype(BF16), vw_ref[h, b, chunk_rows(i, n_band), :],
                                        preferred_element_type=F32))

    for b in batch:
        gates = gate_ref[b]
        for hd in range(NSA_HEADS):
            h, g = hd // GQA, hd % GQA
            r = slice(g * TQ, (g + 1) * TQ)
            o = (gates[:, hd:hd + 1] * oc_ref[b, h, r, :]
                 + gates[:, NSA_HEADS + hd:NSA_HEADS + hd + 1] * os_ref[b, h, r, :]
                 + gates[:, 2 * NSA_HEADS + hd:2 * NSA_HEADS + hd + 1] * o_win[b, h][r])
            ob_ref[b, :, hd * DH:(hd + 1) * DH] = o[:, :DH]
        o = ob_ref[b]
        ms = jnp.mean(o * o, axis=-1, keepdims=True)
        o_ref[b] = (o * lax.rsqrt(ms + EPS) * gn_ref[...]).astype(BF16)


def _nsa(q, kc, vc, ks, vs, kw, vw, gates, tables, attn_out_norm):
    b, s, _ = q.shape
    bias_c, bias_w, bias_s, ovl, expand, expand_wide = tables
    gn = attn_out_norm.reshape(1, NSA_WIDTH)
    nb = NSA_NB
    assert b % nb == 0 and (s // TQ) % 2 == 0
    full = lambda a: pl.BlockSpec(a.shape, lambda bi, i: (0,) * a.ndim)
    tile = lambda w: pl.BlockSpec((nb, TQ, w), lambda bi, i: (bi, i, 0))
    kseq = lambda a: pl.BlockSpec((nb,) + a.shape[1:], lambda bi, i: (bi, 0, 0))
    vseq = lambda a: pl.BlockSpec((HKV, nb) + a.shape[2:], lambda bi, i: (0, bi, 0, 0))
    assert kw.shape[1] == s + WINDOW and vw.shape[2] == s + WINDOW
    rows = GQA * TQ
    unit = (nb, HKV)
    return pl.pallas_call(
        _nsa_kernel,
        out_shape=jax.ShapeDtypeStruct((b, s, NSA_WIDTH), BF16),
        grid=(b // nb, s // TQ),
        in_specs=[tile(NSA_HEADS * LANES),
                  pl.BlockSpec((nb,) + kc.shape[1:], lambda bi, i: (bi, 0, 0)),
                  pl.BlockSpec((nb,) + vc.shape[1:], lambda bi, i: (bi, 0, 0, 0)),
                  kseq(ks), vseq(vs), kseq(kw), vseq(vw), tile(LANES),
                  pl.BlockSpec((HKV, GQA, TQ, LANES), lambda bi, i: (0, 0, i, 0)),
                  full(bias_w), full(bias_s), full(ovl), full(expand), full(expand_wide), full(gn)],
        out_specs=tile(NSA_WIDTH),
        scratch_shapes=[pltpu.VMEM(unit + (rows, LANES), BF16),
                        pltpu.VMEM(unit + (s // TQ, rows, TQ), F32),
                        pltpu.VMEM(unit + (rows, TQ), F32),
                        pltpu.VMEM(unit + (rows, LANES), F32),
                        pltpu.VMEM(unit + (TQ, LANES), BF16),
                        pltpu.VMEM(unit + (rows, LANES), F32),
                        pltpu.VMEM(unit + (rows, LANES), F32),
                        pltpu.VMEM((nb, TQ, NSA_WIDTH), F32)],
        compiler_params=_cparams("parallel", "arbitrary"),
        name="nsa_attention",
    )(q, kc, vc, ks, vs, kw, vw, gates, bias_c, bias_w, bias_s, ovl, expand, expand_wide, gn)


def _ssd_kernel(xbc_ref, z_ref, dt_ref, cw_ref, cb_ref, dtb_ref, alog_ref, dsk_ref, ng_ref, wsrc_ref, o_ref, wdst_ref,
                xbuf_ref, state_ref, y_ref):
    c = pl.program_id(1)
    wdst_ref[...] = wsrc_ref[...].astype(BF16)
    L, P, N = SSD_L, SSD_P, SSD_N
    hpg = SSD_HEADS // SSD_GROUPS
    pad = 8

    @pl.when(c == 0)
    def _():
        xbuf_ref[0:pad, :] = jnp.zeros((pad, SSD_CONV_DIM), F32)
        state_ref[...] = jnp.zeros(state_ref.shape, F32)

    xbuf_ref[pad:pad + L, :] = xbc_ref[0]
    conv = cb_ref[...]
    for k in range(SSD_CONV):
        shift = SSD_CONV - 1 - k
        conv = conv + xbuf_ref[pad - shift:pad - shift + L, :] * cw_ref[k:k + 1, :]
    xbuf_ref[0:pad, :] = xbuf_ref[L:L + pad, :]
    xa = _silu(conv)
    xs = xa[:, :SSD_WIDTH]
    bm = xa[:, SSD_WIDTH:SSD_WIDTH + SSD_GROUPS * N]
    cm = xa[:, SSD_WIDTH + SSD_GROUPS * N:]

    dtv = dt_ref[0] + dtb_ref[...]
    dt = jnp.maximum(dtv, 0.0) + jnp.log1p(jnp.exp(-jnp.abs(dtv)))
    a = dt * (-jnp.exp(alog_ref[...]))
    ri = lax.broadcasted_iota(jnp.int32, (L, L), 0)
    ci = lax.broadcasted_iota(jnp.int32, (L, L), 1)
    causal = ri >= ci
    cs = jnp.dot(causal.astype(F32), a, preferred_element_type=F32,
                 precision=lax.Precision.HIGHEST)
    cs_t = cs.T
    nt_dims = (((1,), (1,)), ((), ()))

    for gr in range(SSD_GROUPS):
        b_g = bm[:, gr * N:(gr + 1) * N]
        c_g = cm[:, gr * N:(gr + 1) * N]
        scores = lax.dot_general(c_g.astype(BF16), b_g.astype(BF16), nt_dims, preferred_element_type=F32)
        b_gt = b_g.T
        for hh in range(hpg):
            h = gr * hpg + hh
            cs_col = cs[:, h:h + 1]
            cs_row = cs_t[h:h + 1, :]
            cs_last = cs[L - 1:L, h:h + 1]
            decay = jnp.exp(jnp.where(causal, cs_col - cs_row, NEG))
            xs_h = xs[:, h * P:(h + 1) * P]
            xc = (xs_h * dt[:, h:h + 1]).astype(BF16)
            y = jnp.dot((scores * decay).astype(BF16), xc, preferred_element_type=F32)
            prev = state_ref[h]
            y = y + jnp.dot((c_g * jnp.exp(cs_col)).astype(BF16), prev.astype(BF16),
                            preferred_element_type=F32)
            contrib = jnp.dot((b_gt * jnp.exp(cs_last - cs_row)).astype(BF16), xc,
                              preferred_element_type=F32)
            state_ref[h] = jnp.exp(cs_last) * prev + contrib
            y_ref[:, h * P:(h + 1) * P] = y + xs_h * dsk_ref[:, h * P:(h + 1) * P]

    z = z_ref[0]
    y = y_ref[...] * _silu(z)
    gw = SSD_WIDTH // SSD_GROUPS
    for gr in range(SSD_GROUPS):
        yg = y[:, gr * gw:(gr + 1) * gw]
        ms = jnp.mean(yg * yg, axis=-1, keepdims=True)
        o_ref[0, :, gr * gw:(gr + 1) * gw] = (yg * lax.rsqrt(ms + EPS)
                                              * ng_ref[:, gr * gw:(gr + 1) * gw]).astype(BF16)


def _ssd(xbc, z, dt, conv_w, conv_b, dt_bias, a_log, d_skip, norm_g, w_cast):
    b, s, _ = xbc.shape
    n_steps = b * (s // SSD_L)
    wr, wc = w_cast.shape
    assert wr % n_steps == 0
    wblk = pl.BlockSpec((wr // n_steps, wc), lambda bi, c: (bi * (s // SSD_L) + c, 0))
    padl = lambda v: jnp.pad(v, (0, LANES - v.shape[0])).reshape(1, LANES)
    args = (conv_w, conv_b.reshape(1, SSD_CONV_DIM), padl(dt_bias), padl(a_log),
            jnp.repeat(d_skip, SSD_P).reshape(1, SSD_WIDTH), norm_g.reshape(1, SSD_WIDTH))
    full = lambda a: pl.BlockSpec(a.shape, lambda bi, c: (0,) * a.ndim)
    blk = lambda w: pl.BlockSpec((1, SSD_L, w), lambda bi, c: (bi, c, 0))
    return pl.pallas_call(
        _ssd_kernel,
        out_shape=[jax.ShapeDtypeStruct((b, s, SSD_WIDTH), BF16), jax.ShapeDtypeStruct((wr, wc), BF16)],
        grid=(b, s // SSD_L),
        in_specs=[blk(SSD_CONV_DIM), blk(SSD_WIDTH), blk(LANES)] + [full(a) for a in args] + [wblk],
        out_specs=[blk(SSD_WIDTH), wblk],
        scratch_shapes=[pltpu.VMEM((SSD_L + 8, SSD_CONV_DIM), F32),
                        pltpu.VMEM((SSD_HEADS, SSD_N, SSD_P), F32),
                        pltpu.VMEM((SSD_L, SSD_WIDTH), F32)],
        compiler_params=_cparams("parallel", "arbitrary"),
        name="ssd_mixer",
    )(xbc, z, dt, *args, w_cast)


def _out_proj_kernel(x_ref, on_ref, os_ref, wo_ref, fg_ref, rw_ref, rb_ref, wsrc_ref,
                     h_ref, hn_ref, comb_ref, slot_ref, wdst_ref):
    wdst_ref[...] = wsrc_ref[...].astype(BF16)
    h = (x_ref[...]
         + jnp.dot(on_ref[...], wo_ref[0:NSA_WIDTH, :], preferred_element_type=F32)
         + jnp.dot(os_ref[...], wo_ref[NSA_WIDTH:, :], preferred_element_type=F32))
    tm = h.shape[0]
    for j in range(D_MODEL // LANES):
        h_ref[pl.ds(j, tm, stride=D_MODEL // LANES), :] = h[:, j * LANES:(j + 1) * LANES]
    ms = jnp.mean(h * h, axis=-1, keepdims=True)
    hn = h * lax.rsqrt(ms + EPS) * fg_ref[...]
    for j in range(D_MODEL // LANES):
        hn_ref[pl.ds(j, tm, stride=D_MODEL // LANES), :] = hn[:, j * LANES:(j + 1) * LANES]
    hn_hi = hn.astype(BF16)
    hn_lo = (hn - hn_hi.astype(F32)).astype(BF16)
    rw = rw_ref[...]
    rw_hi = rw.astype(BF16)
    rw_lo = (rw - rw_hi.astype(F32)).astype(BF16)
    logits = (jnp.dot(hn_hi, rw_hi, preferred_element_type=F32) + jnp.dot(hn_lo, rw_hi, preferred_element_type=F32)
              + jnp.dot(hn_hi, rw_lo, preferred_element_type=F32)) + rb_ref[...]
    lane = lax.broadcasted_iota(jnp.int32, logits.shape, 1)
    work = logits
    picks = []
    for _ in range(TOP_K):
        v = jnp.max(work, axis=-1, keepdims=True)
        idx = jnp.min(jnp.where(work == v, lane, LANES), axis=-1, keepdims=True)
        hit = lane == idx
        picks.append((v, hit))
        work = jnp.where(hit, -3e38, work)
    v0 = picks[0][0]
    es = [jnp.exp(v - v0) for v, _ in picks]
    den = es[0] + es[1] + es[2] + es[3]
    comb = jnp.zeros_like(logits)
    slot = jnp.zeros(logits.shape, jnp.int32)
    for k, (e, (_, hit)) in enumerate(zip(es, picks)):
        comb = comb + jnp.where(hit, e / den, 0.0)
        slot = jnp.where(hit, k + 1, slot)
    comb_ref[...] = comb
    slot_ref[...] = slot


def _out_proj(x2, o_nsa, o_ssd, w_out, ffn_norm, router_w, router_b, w_cast, tm=512):
    t = x2.shape[0]
    wr, wc = w_cast.shape
    assert wr % (t // tm) == 0
    wblk = pl.BlockSpec((wr // (t // tm), wc), lambda i: (i, 0))
    row = lambda w: pl.BlockSpec((tm, w), lambda i: (i, 0))
    full = lambda a: pl.BlockSpec(a.shape, lambda i: (0,) * a.ndim)
    wo = w_out.astype(BF16)
    fg = ffn_norm.reshape(1, D_MODEL)
    rw = jnp.pad(router_w, ((0, 0), (0, LANES - N_EXPERTS)))
    rb = jnp.pad(router_b, (0, LANES - N_EXPERTS), constant_values=NEG).reshape(1, LANES)
    return pl.pallas_call(
        _out_proj_kernel,
        out_shape=[jax.ShapeDtypeStruct((t * H_SLAB, LANES), F32),
                   jax.ShapeDtypeStruct((t * H_SLAB, LANES), F32),
                   jax.ShapeDtypeStruct((t, LANES), F32),
                   jax.ShapeDtypeStruct((t, LANES), jnp.int32),
                   jax.ShapeDtypeStruct((wr, wc), BF16)],
        grid=(t // tm,),
        in_specs=[row(D_MODEL), row(NSA_WIDTH), row(SSD_WIDTH), full(wo), full(fg), full(rw), full(rb), wblk],
        out_specs=[pl.BlockSpec((tm * H_SLAB, LANES), lambda i: (i, 0)),
                   pl.BlockSpec((tm * H_SLAB, LANES), lambda i: (i, 0)), row(LANES), row(LANES), wblk],
        compiler_params=_cparams("parallel"),
        name="out_proj_router",
    )(x2, o_nsa, o_ssd, wo, fg, rw, rb, w_cast)


MOE_CHUNK = 4096
MOE_TILE = 256
ROUTE_TILE = 256


def _route_kernel(comb_ref, slot_ref, dest_ref, wrow_ref, starts_ref, pos_ref):
    tc = comb_ref.shape[0]
    nt_dims = (((1,), (1,)), ((), ()))
    hi = lax.Precision.HIGHEST
    ri = lax.broadcasted_iota(jnp.int32, (ROUTE_TILE, ROUTE_TILE), 0)
    ci = lax.broadcasted_iota(jnp.int32, (ROUTE_TILE, ROUTE_TILE), 1)
    below = (ri > ci).astype(BF16)
    carry = jnp.zeros((1, LANES), F32)
    for j in range(tc // ROUTE_TILE):
        rows = slice(j * ROUTE_TILE, (j + 1) * ROUTE_TILE)
        sel = jnp.where(slot_ref[rows, :] > 0, 1.0, 0.0)
        pos_ref[rows, :] = jnp.dot(below, sel.astype(BF16), preferred_element_type=F32) + carry
        carry = carry + jnp.sum(sel, axis=0, keepdims=True)
    li = lax.broadcasted_iota(jnp.int32, (LANES, LANES), 0)
    lj = lax.broadcasted_iota(jnp.int32, (LANES, LANES), 1)
    before = (li < lj).astype(F32)
    counts = jnp.broadcast_to(carry, (8, LANES))
    starts = jnp.dot(counts, before, preferred_element_type=F32, precision=hi)
    starts_ref[0] = starts.astype(jnp.int32)
    dest = pos_ref[...] + starts[0:1, :]
    ones = jnp.ones((8, LANES), F32)
    slot = slot_ref[...]
    comb = comb_ref[...]
    for k in range(TOP_K):
        hit = slot == k + 1
        d = lax.dot_general(ones, jnp.where(hit, dest, 0.0), nt_dims, preferred_element_type=F32, precision=hi)
        w = lax.dot_general(ones, jnp.where(hit, comb, 0.0), nt_dims, preferred_element_type=F32, precision=hi)
        dest_ref[0, k:k + 1, :] = d[0:1, :].astype(jnp.int32)
        wrow_ref[0, k:k + 1, :] = w[0:1, :]


def _route(comb, slot, tc):
    t = comb.shape[0]
    nch = t // tc
    blk = pl.BlockSpec((tc, LANES), lambda c: (c, 0))
    return pl.pallas_call(
        _route_kernel,
        out_shape=[jax.ShapeDtypeStruct((nch, TOP_K, tc), jnp.int32),
                   jax.ShapeDtypeStruct((nch, TOP_K, tc), F32),
                   jax.ShapeDtypeStruct((nch, 8, LANES), jnp.int32)],
        grid=(nch,),
        in_specs=[blk, blk],
        out_specs=[pl.BlockSpec((1, TOP_K, tc), lambda c: (c, 0, 0)),
                   pl.BlockSpec((1, TOP_K, tc), lambda c: (c, 0, 0)),
                   pl.BlockSpec((1, 8, LANES), lambda c: (c, 0, 0))],
        scratch_shapes=[pltpu.VMEM((tc, LANES), F32)],
        compiler_params=_cparams("parallel"),
        name="moe_route",
    )(comb, slot)


def _moe_kernel(starts_ref, dest_hbm, wrow_hbm, hn_hbm, h_hbm, wgu_ref, bgu_ref, wd_ref, bd_ref, fn_ref, o_hbm,
                x_ref, acc_ref, xs0_ref, xs1_ref, y0_ref, y1_ref, ob_ref, dest_s, wrow_s, tok_s, wt_s, st_s,
                sem, osem):
    c = pl.program_id(0)
    e = pl.program_id(1)
    tc = x_ref.shape[0] // H_SLAB
    n_rows = TOP_K * tc

    def slab(i, n, width):
        return pl.ds(pl.multiple_of(i * width, width), n * width)

    half_len = n_rows + MOE_TILE
    cur = (c % 2) * half_len
    nxt = half_len - cur

    def invert(j0, u, half):
        r = half + dest_s[j0 + u]
        tok_s[r] = (j0 & (tc - 1)) + u
        wt_s[r] = wrow_s[j0 + u]

    def gather(r0, xs_ref):
        i0 = cur + r0
        for r in range(MOE_TILE):
            xs_ref[slab(r, 1, H_SLAB), :] = x_ref[slab(tok_s[i0 + r], 1, H_SLAB), :]

    def ffn(xs_ref, y_ref):
        x = jnp.concatenate([xs_ref[pl.ds(j, MOE_TILE, stride=H_SLAB), :].astype(BF16) for j in range(H_SLAB)],
                            axis=1)
        gu = jnp.dot(x, wgu_ref[0], preferred_element_type=F32) + bgu_ref[0]
        gate = jnp.minimum(gu[:, :D_FF], SWIGLU_LIMIT)
        up = jnp.clip(gu[:, D_FF:], -SWIGLU_LIMIT, SWIGLU_LIMIT)
        act = (up + 1.0) * gate / (1.0 + jnp.exp(-SWIGLU_ALPHA * gate))
        y = jnp.dot(act.astype(BF16), wd_ref[0], preferred_element_type=F32) + bd_ref[0]
        for j in range(H_SLAB):
            y_ref[pl.ds(j, MOE_TILE, stride=H_SLAB), :] = y[:, j * LANES:(j + 1) * LANES]

    def scatter(r0, n_valid, y_ref):
        i0 = cur + r0
        for g in range(MOE_TILE // SUBLANES):
            rows = []
            for u in range(SUBLANES):
                r = g * SUBLANES + u
                tok = jnp.where(r < n_valid, tok_s[i0 + r], tc)
                rows.append((tok, acc_ref[slab(tok, 1, H_SLAB), :] + wt_s[i0 + r] * y_ref[slab(r, 1, H_SLAB), :]))
            for tok, val in rows:
                acc_ref[slab(tok, 1, H_SLAB), :] = val

    chunk_copies = lambda: (pltpu.make_async_copy(hn_hbm.at[slab(c * tc, tc, H_SLAB), :], x_ref, sem.at[0]),
                            pltpu.make_async_copy(h_hbm.at[slab(c * tc, tc, H_SLAB), :],
                                                  acc_ref.at[pl.ds(0, tc * H_SLAB), :], sem.at[1]))
    map_copies = lambda cc: (pltpu.make_async_copy(dest_hbm.at[cc], dest_s, sem.at[2]),
                             pltpu.make_async_copy(wrow_hbm.at[cc], wrow_s, sem.at[3]))

    @pl.when(e == 0)
    def _():
        for cp in chunk_copies():
            cp.start()

        @pl.when(c == 0)
        def _():
            for cp in map_copies(0):
                cp.start()
            for cp in map_copies(0):
                cp.wait()

            def invert_slice(g, carry):
                for u in range(MOE_TILE):
                    invert(g * MOE_TILE, u, cur)
                return carry

            lax.fori_loop(0, n_rows // MOE_TILE, invert_slice, 0)
            for half in (0, half_len):
                for u in range(MOE_TILE):
                    tok_s[half + n_rows + u] = 0
                    wt_s[half + n_rows + u] = 0.0

        @pl.when(c + 1 < pl.num_programs(0))
        def _():
            for cp in map_copies(c + 1):
                cp.start()
            for cp in map_copies(c + 1):
                cp.wait()

        for cp in chunk_copies():
            cp.wait()
        acc_ref[pl.ds(tc * H_SLAB, H_SLAB), :] = jnp.zeros((H_SLAB, LANES), F32)
        st_s[0] = 0
        st_s[1] = 0
        st_s[2] = 0
        for y_ref in (y0_ref, y1_ref):
            y_ref[...] = jnp.zeros(y_ref.shape, F32)
        gather(0, xs0_ref)

    base = starts_ref[c * LANES + e]
    n_e = starts_ref[c * LANES + e + 1] - base
    xs_bufs = (xs0_ref, xs1_ref)
    y_bufs = (y0_ref, y1_ref)

    def tile_step(p, k, r0_next, prev_r0, prev_nv):
        first = jnp.minimum(k, n_rows // MOE_TILE - 1) * MOE_TILE
        for u in range(MOE_TILE):
            invert(first, u, nxt)
        gather(r0_next, xs_bufs[1 - p])
        scatter(prev_r0, prev_nv, y_bufs[1 - p])
        ffn(xs_bufs[p], y_bufs[p])

    def tile(j, carry):
        r0 = base + j * MOE_TILE
        r0_next = jnp.minimum(r0 + MOE_TILE, base + n_e)
        k = st_s[0]
        prev_r0 = st_s[1]
        prev_nv = st_s[2]
        for p in range(2):
            pl.when(k % 2 == p)(functools.partial(tile_step, p, k, r0_next, prev_r0, prev_nv))
        st_s[0] = k + 1
        st_s[1] = r0
        st_s[2] = jnp.minimum(n_e - j * MOE_TILE, MOE_TILE)
        return carry

    lax.fori_loop(0, (n_e + MOE_TILE - 1) // MOE_TILE, tile, 0)

    @pl.when(e == pl.num_programs(1) - 1)
    def _():
        for p in range(2):
            pl.when(st_s[0] % 2 == p)(functools.partial(scatter, st_s[1], st_s[2], y_bufs[1 - p]))
        n_groups = tc // MOE_TILE
        out_copy = lambda g, buf: pltpu.make_async_copy(
            ob_ref.at[buf], o_hbm.at[pl.ds(pl.multiple_of(c * tc + g * MOE_TILE, MOE_TILE), MOE_TILE), :],
            osem.at[buf])

        def norm(g, carry):
            buf = g % 2

            @pl.when(g >= 2)
            def _():
                out_copy(g - 2, buf).wait()

            first = pl.multiple_of(g * MOE_TILE * H_SLAB, MOE_TILE * H_SLAB)
            hs = [acc_ref[pl.ds(first + j, MOE_TILE, stride=H_SLAB), :] for j in range(H_SLAB)]
            ss = hs[0] * hs[0]
            for hj in hs[1:]:
                ss = ss + hj * hj
            inv = lax.rsqrt(jnp.sum(ss, axis=-1, keepdims=True) / D_MODEL + EPS)
            for j, hj in enumerate(hs):
                ob_ref[buf, :, j * LANES:(j + 1) * LANES] = hj * inv * fn_ref[:, j * LANES:(j + 1) * LANES]
            out_copy(g, buf).start()
            return carry

        lax.fori_loop(0, n_groups, norm, 0)
        for g in range(max(n_groups - 2, 0), n_groups):
            out_copy(g, g % 2).wait()


def _moe(hnp, comb, slot, h1, w_gate_up, b_gate_up, w_down, b_down, final_norm, tc=MOE_CHUNK):
    t = comb.shape[0]
    tc = min(tc, t)
    assert tc & (tc - 1) == 0 and tc % MOE_TILE == 0
    nch = t // tc
    dest, wrow, starts = _route(comb, slot, tc)
    dest = dest.reshape(nch, TOP_K * tc)
    wrow = wrow.reshape(nch, TOP_K * tc)
    starts = starts[:, 0, :].reshape(nch * LANES)
    wgu = w_gate_up.astype(BF16)
    wd = w_down.astype(BF16)
    bgu = b_gate_up.reshape(N_EXPERTS, 1, 2 * D_FF)
    bd = b_down.reshape(N_EXPERTS, 1, D_MODEL)
    fn = final_norm.reshape(1, D_MODEL)
    anyspace = pl.BlockSpec(memory_space=pl.ANY)
    exp = lambda a: pl.BlockSpec((1,) + a.shape[1:], lambda c, e, st: (e, 0, 0))
    return pl.pallas_call(
        _moe_kernel,
        out_shape=jax.ShapeDtypeStruct((t, D_MODEL), F32),
        grid_spec=pltpu.PrefetchScalarGridSpec(
            num_scalar_prefetch=1,
            grid=(nch, N_EXPERTS),
            in_specs=[anyspace, anyspace, anyspace, anyspace, exp(wgu), exp(bgu), exp(wd), exp(bd),
                      pl.BlockSpec(fn.shape, lambda c, e, st: (0, 0))],
            out_specs=anyspace,
            scratch_shapes=[pltpu.VMEM((tc * H_SLAB, LANES), F32),
                            pltpu.VMEM(((tc + 1) * H_SLAB, LANES), F32),
                            pltpu.VMEM((MOE_TILE * H_SLAB, LANES), F32),
                            pltpu.VMEM((MOE_TILE * H_SLAB, LANES), F32),
                            pltpu.VMEM((MOE_TILE * H_SLAB, LANES), F32),
                            pltpu.VMEM((MOE_TILE * H_SLAB, LANES), F32),
                            pltpu.VMEM((2, MOE_TILE, D_MODEL), F32),
                            pltpu.SMEM((TOP_K * tc,), jnp.int32),
                            pltpu.SMEM((TOP_K * tc,), F32),
                            pltpu.SMEM((2 * (TOP_K * tc + MOE_TILE),), jnp.int32),
                            pltpu.SMEM((2 * (TOP_K * tc + MOE_TILE),), F32),
                            pltpu.SMEM((4,), jnp.int32),
                            pltpu.SemaphoreType.DMA((4,)),
                            pltpu.SemaphoreType.DMA((2,))]),
        compiler_params=_cparams("arbitrary", "arbitrary", vmem=MOE_VMEM_LIMIT),
        name="moe_experts",
    )(starts, dest, wrow, hnp, h1, wgu, bgu, wd, bd, fn)


def kernel(x, attn_norm, w_in, rel_bias, cmp_pos, cmp_w1, cmp_w2, attn_out_norm, conv_w, conv_b,
           dt_bias, a_log, d_skip, ssm_out_norm, w_out, ffn_norm, router_w, router_b,
           w_gate_up, b_gate_up, w_down, b_down, final_norm):
    b, s, d = x.shape
    t = b * s
    depth = w_in.shape[0]
    tables = tuple(_bias_tables(rel_bias, s)) + _sel_tables(s)
    h = x.reshape(t, d)
    for l in range(depth):
        (q, kc_raw, vc_raw, ks, vs, kw, vw, gates, z, xbc, dt) = _in_proj(h, attn_norm[l], _pad_w_in(w_in[l]))
        grp = CMP_STRIDE * HKV * DH
        kc, vc = _compress(kc_raw.reshape(b, s // CMP_STRIDE, grp), vc_raw.reshape(b, s // CMP_STRIDE, grp),
                           _compress_weights(cmp_pos[l], cmp_w1[l], cmp_w2[l]))
        seq = lambda a: a.reshape(a.shape[:-2] + (b, s, a.shape[-1]))
        front = lambda a: jnp.pad(a, [(0, 0)] * (a.ndim - 2) + [(WINDOW, 0), (0, 0)])
        o_nsa = _nsa(seq(q), kc, vc, seq(ks), seq(vs), front(seq(kw)), front(seq(vw)), seq(gates), tables,
                     attn_out_norm[l])
        o_ssd, wgu = _ssd(seq(xbc), seq(z), seq(dt), conv_w[l], conv_b[l], dt_bias[l], a_log[l], d_skip[l],
                          ssm_out_norm[l], w_gate_up[l].reshape(N_EXPERTS * D_MODEL, 2 * D_FF))
        h1, hnp, comb, slot, wd = _out_proj(h, o_nsa.reshape(t, NSA_WIDTH), o_ssd.reshape(t, SSD_WIDTH), w_out[l],
                                            ffn_norm[l], router_w[l], router_b[l],
                                            w_down[l].reshape(N_EXPERTS * D_FF, D_MODEL))
        assert depth == 1
        h = _moe(hnp, comb, slot, h1, wgu.reshape(N_EXPERTS, D_MODEL, 2 * D_FF), b_gate_up[l],
                 wd.reshape(N_EXPERTS, D_FF, D_MODEL), b_down[l], final_norm)
    return h.reshape(b, s, d)
```

```python
import functools
import math

import numpy as np
import jax
import jax.numpy as jnp
from jax import lax
from jax.experimental import pallas as pl
from jax.experimental.pallas import tpu as pltpu

F32 = jnp.float32
BF16 = jnp.bfloat16

D_MODEL = 1024
NSA_HEADS = 8
HKV = 2
GQA = NSA_HEADS // HKV
DH = 64
NSA_WIDTH = NSA_HEADS * DH
CMP_BLOCK = 32
CMP_STRIDE = 16
SEL_BLOCK = 64
SEL_TOP = 16
WINDOW = 512
TQ = 128
NSA_NB = 2
SSD_HEADS = 8
SSD_P = 64
SSD_WIDTH = SSD_HEADS * SSD_P
SSD_GROUPS = 2
SSD_N = 128
SSD_CONV = 4
SSD_L = 128
SSD_NB = 2
SSD_CONV_DIM = SSD_WIDTH + 2 * SSD_GROUPS * SSD_N
N_BUCKETS = 32
MAX_DISTANCE = 128
N_EXPERTS = 32
TOP_K = 4
D_FF = 1024
SWIGLU_LIMIT = 7.0
SWIGLU_ALPHA = 1.702

EPS = 1e-6
NEG = -1e30
FORCED_SCORE = 1e4
LOG2E = 1.4426950408889634
LANES = 128
SUBLANES = 8
H_SLAB = D_MODEL // LANES
VMEM_LIMIT = 56 * 1024 * 1024
MOE_VMEM_LIMIT = (2 * 4096 * D_MODEL * 4 + 2 * 3 * D_MODEL * D_FF * 2 + 12 * 1024 * 1024)


def _cparams(*sem, vmem=VMEM_LIMIT):
    return pltpu.CompilerParams(dimension_semantics=sem, vmem_limit_bytes=vmem)


def _silu(v):
    return v / (1.0 + jnp.exp(-v))


_Q0, _Q1 = 0, NSA_HEADS * LANES
_KV0 = _Q1
_G0 = _KV0 + 6 * LANES
_DT0 = _G0 + LANES
_Z0 = _DT0 + LANES
_X0 = _Z0 + SSD_WIDTH
_WCOLS = _X0 + SSD_CONV_DIM
MXU_N = 256


def _pad_w_in(w_in):
    d = w_in.shape[0]
    nsa_cols = NSA_WIDTH + 6 * HKV * DH + 3 * NSA_HEADS
    wq = w_in[:, :NSA_WIDTH].reshape(d, HKV, GQA, DH)
    zq = jnp.zeros_like(wq)
    q0 = jnp.concatenate([wq[:, 0], zq[:, 0]], axis=-1)
    q1 = jnp.concatenate([zq[:, 1], wq[:, 1]], axis=-1)
    wq_pad = jnp.stack([q0, q1], axis=1).reshape(d, NSA_HEADS * LANES)
    wkv = w_in[:, NSA_WIDTH:NSA_WIDTH + 6 * HKV * DH]
    wg = w_in[:, NSA_WIDTH + 6 * HKV * DH:nsa_cols]
    wg = jnp.pad(wg, ((0, 0), (0, LANES - wg.shape[1])))
    wz = w_in[:, nsa_cols:nsa_cols + SSD_WIDTH]
    wx = w_in[:, nsa_cols + SSD_WIDTH:nsa_cols + SSD_WIDTH + SSD_CONV_DIM]
    wdt = w_in[:, nsa_cols + SSD_WIDTH + SSD_CONV_DIM:]
    wdt = jnp.pad(wdt, ((0, 0), (0, LANES - wdt.shape[1])))
    return jnp.concatenate([wq_pad, wkv, wg, wdt, wz, wx], axis=1).astype(BF16)


def _in_proj_kernel(x_ref, g_ref, w_ref, q_ref, kc_ref, vc_ref, ks_ref, vs_ref, kw_ref, vw_ref,
                    gate_ref, z_ref, xbc_ref, dt_ref):
    x = x_ref[...]
    ms = jnp.mean(x * x, axis=-1, keepdims=True)
    xn = (x * lax.rsqrt(ms + EPS) * g_ref[...]).astype(BF16)

    def seg(lo, hi):
        return jnp.dot(xn, w_ref[:, lo:hi], preferred_element_type=F32)

    q_ref[...] = (seg(_Q0, _Q1) * (DH ** -0.5 * LOG2E)).astype(BF16)
    ones = jnp.ones((x.shape[0], LANES - DH), F32)
    kv_refs = (kc_ref, vc_ref, ks_ref, vs_ref, kw_ref, vw_ref)
    for j in range(0, len(kv_refs), MXU_N // LANES):
        pair = seg(_KV0 + j * LANES, _KV0 + j * LANES + MXU_N)
        kv_refs[j][...] = pair[:, :LANES].astype(BF16)
        val = pair[:, LANES:]
        if j == 0:
            kv_refs[j + 1][...] = val.astype(BF16)
        else:
            for h in range(HKV):
                kv_refs[j + 1][h] = jnp.concatenate([val[:, h * DH:(h + 1) * DH], ones], axis=1).astype(BF16)
    gate_dt = seg(_G0, _Z0)
    gate_ref[...] = 1.0 / (1.0 + jnp.exp(-gate_dt[:, :LANES]))
    dt_ref[...] = gate_dt[:, LANES:]
    z_ref[...] = seg(_Z0, _X0)
    xbc_ref[...] = seg(_X0, _WCOLS)


def _in_proj(x2, attn_norm, w_pad, tm=512):
    t = x2.shape[0]
    row = lambda w: pl.BlockSpec((tm, w), lambda i: (i, 0))
    full = lambda a: pl.BlockSpec(a.shape, lambda i: (0,) * a.ndim)
    g = attn_norm.reshape(1, D_MODEL)
    kv = jax.ShapeDtypeStruct((t, LANES), BF16)
    val = jax.ShapeDtypeStruct((HKV, t, LANES), BF16)
    outs = ([jax.ShapeDtypeStruct((t, NSA_HEADS * LANES), BF16), kv, kv, kv, val, kv, val]
            + [jax.ShapeDtypeStruct((t, LANES), F32),
               jax.ShapeDtypeStruct((t, SSD_WIDTH), F32),
               jax.ShapeDtypeStruct((t, SSD_CONV_DIM), F32),
               jax.ShapeDtypeStruct((t, LANES), F32)])
    spec = lambda s: (row(s.shape[1]) if len(s.shape) == 2
                      else pl.BlockSpec((HKV, tm, LANES), lambda i: (0, i, 0)))
    return pl.pallas_call(
        _in_proj_kernel,
        out_shape=outs,
        grid=(t // tm,),
        in_specs=[row(D_MODEL), full(g), full(w_pad)],
        out_specs=[spec(s) for s in outs],
        compiler_params=_cparams("parallel"),
        name="in_proj",
    )(x2, g, w_pad)


def _compress_weights(cmp_pos, cmp_w1, cmp_w2):
    half = CMP_BLOCK // 2
    eye = jnp.eye(HKV, dtype=F32)
    w1 = cmp_w1.reshape(2, CMP_BLOCK, DH, DH)
    w1big = jnp.einsum('jlde,hk->jlhdke', w1, eye)
    w1lo = w1big[:, :half].reshape(2, half * HKV * DH, HKV * DH).astype(BF16)
    w1hi = w1big[:, half:].reshape(2, half * HKV * DH, HKV * DH).astype(BF16)
    pos = jnp.broadcast_to(cmp_pos[:, :, None, :], (2, CMP_BLOCK, HKV, DH))
    poslo = pos[:, :half].reshape(2, 1, half * HKV * DH)
    poshi = pos[:, half:].reshape(2, 1, half * HKV * DH)
    w2big = jnp.einsum('jde,hk->jhdke', cmp_w2, eye).reshape(2, HKV * DH, HKV * DH).astype(BF16)
    return w1lo, w1hi, poslo, poshi, w2big


def _compress_kernel(kr_ref, vr_ref, w1lo_ref, w1hi_ref, poslo_ref, poshi_ref, w2_ref, kc_ref, vc_ref):
    for j, (src, dst) in enumerate(((kr_ref, kc_ref), (vr_ref, vc_ref))):
        r = src[0].astype(F32)
        a = jnp.dot((r + poslo_ref[j]).astype(BF16), w1lo_ref[j], preferred_element_type=F32)
        b = jnp.dot((r + poshi_ref[j]).astype(BF16), w1hi_ref[j], preferred_element_type=F32)
        hid = a + pltpu.roll(b, b.shape[0] - 1, 0)
        out = jnp.dot(_silu(hid).astype(BF16), w2_ref[j], preferred_element_type=F32)
        if j == 0:
            dst[0] = out.astype(BF16)
        else:
            dst[0, 0] = out.astype(BF16)
            dst[0, 1] = pltpu.roll(out, DH, 1).astype(BF16)


def _compress(kr, vr, cw):
    b, ng, width = kr.shape
    w1lo, w1hi, poslo, poshi, w2big = cw
    full = lambda a: pl.BlockSpec(a.shape, lambda i: (0,) * a.ndim)
    bspec = pl.BlockSpec((1, ng, width), lambda i: (i, 0, 0))
    return pl.pallas_call(
        _compress_kernel,
        out_shape=[jax.ShapeDtypeStruct((b, ng, HKV * DH), BF16),
                   jax.ShapeDtypeStruct((b, HKV, ng, HKV * DH), BF16)],
        grid=(b,),
        in_specs=[bspec, bspec, full(w1lo), full(w1hi), full(poslo), full(poshi), full(w2big)],
        out_specs=[pl.BlockSpec((1, ng, HKV * DH), lambda i: (i, 0, 0)),
                   pl.BlockSpec((1, HKV, ng, HKV * DH), lambda i: (i, 0, 0, 0))],
        compiler_params=_cparams("parallel"),
        name="nsa_compress",
    )(kr, vr, w1lo, w1hi, poslo, poshi, w2big)


def _bucket_thresholds():
    d = np.arange(MAX_DISTANCE + 1)
    max_exact = N_BUCKETS // 2
    nf = np.maximum(d, max_exact).astype(np.float32)
    large = max_exact + (np.log(nf / np.float32(max_exact)) / np.float32(math.log(MAX_DISTANCE / max_exact))
                         * np.float32(N_BUCKETS - max_exact)).astype(np.int32)
    bucket = np.where(d < max_exact, d, np.minimum(large, N_BUCKETS - 1))
    assert np.all(np.diff(bucket) >= 0) and bucket[MAX_DISTANCE] == N_BUCKETS - 1
    return [int(np.argmax(bucket >= k)) for k in range(N_BUCKETS)]


def _bias_kernel(rb_ref, bc_ref, bw_ref, bs_ref, *, n_cmp):
    i = pl.program_id(0)
    thr = _bucket_thresholds()

    def table(dist, valid, hd, shift):
        v = jnp.full(dist.shape, rb_ref[0, hd], F32)
        for k in range(1, N_BUCKETS):
            v = jnp.where(dist >= thr[k], rb_ref[k, hd], v)
        return jnp.where(valid, (v - shift) * LOG2E, NEG)

    row = lax.broadcasted_iota(jnp.int32, (TQ, LANES), 0)
    col = lax.broadcasted_iota(jnp.int32, (TQ, LANES), 1)
    dist_c = i * TQ + row - (col * CMP_STRIDE + CMP_BLOCK - 1)
    valid_c = (dist_c >= 0) & (col < n_cmp)
    for hd in range(NSA_HEADS):
        bc_ref[hd // GQA, hd % GQA] = table(dist_c, valid_c, hd, 0.0)

    @pl.when(i == 0)
    def _():
        band = WINDOW + TQ
        qi_w = lax.broadcasted_iota(jnp.int32, (TQ, band), 0)
        dist_w = qi_w + WINDOW - lax.broadcasted_iota(jnp.int32, (TQ, band), 1)
        valid_w = (dist_w >= 0) & (dist_w < WINDOW)
        qi_s = lax.broadcasted_iota(jnp.int32, (TQ, 2 * TQ), 0)
        dist_s = qi_s + TQ - lax.broadcasted_iota(jnp.int32, (TQ, 2 * TQ), 1)
        for hd in range(NSA_HEADS):
            k, g = hd // GQA, hd % GQA
            bw_ref[k, g * TQ:(g + 1) * TQ, :] = table(dist_w, valid_w, hd, 0.0)
            bs_ref[k, g * TQ:(g + 1) * TQ, :] = table(dist_s, dist_s >= 0, hd, rb_ref[N_BUCKETS - 1, hd])


def _bias_tables(rel_bias, s):
    n_cmp = (s - CMP_BLOCK) // CMP_STRIDE + 1
    assert n_cmp < LANES and TQ >= MAX_DISTANCE
    band = WINDOW + TQ
    return pl.pallas_call(
        functools.partial(_bias_kernel, n_cmp=n_cmp),
        out_shape=[jax.ShapeDtypeStruct((HKV, GQA, s, LANES), F32),
                   jax.ShapeDtypeStruct((HKV, GQA * TQ, band), F32),
                   jax.ShapeDtypeStruct((HKV, GQA * TQ, 2 * TQ), F32)],
        grid=(s // TQ,),
        in_specs=[pl.BlockSpec(memory_space=pltpu.SMEM)],
        out_specs=[pl.BlockSpec((HKV, GQA, TQ, LANES), lambda i: (0, 0, i, 0)),
                   pl.BlockSpec((HKV, GQA * TQ, band), lambda i: (0, 0, 0)),
                   pl.BlockSpec((HKV, GQA * TQ, 2 * TQ), lambda i: (0, 0, 0))],
        compiler_params=_cparams("arbitrary"),
        name="nsa_bias_tables",
    )(rel_bias)


def _sel_tables(s):
    n_cmp = (s - CMP_BLOCK) // CMP_STRIDE + 1
    n_sel = s // SEL_BLOCK
    c_start = np.arange(LANES) * CMP_STRIDE
    s_start = np.arange(n_sel) * SEL_BLOCK
    ovl = ((c_start[None, :] < s_start[:, None] + SEL_BLOCK)
           & (c_start[None, :] + CMP_BLOCK > s_start[:, None])
           & (np.arange(LANES)[None, :] < n_cmp)).astype(np.float32)
    key_blk = np.arange(s) // SEL_BLOCK
    expand = np.where(key_blk[None, :] == np.arange(LANES)[:, None], NEG, 0.0).astype(np.float32)
    expand_wide = expand.reshape(LANES, s // (2 * TQ), 2 * TQ).transpose(1, 0, 2)
    expand = expand.reshape(LANES, s // TQ, TQ).transpose(1, 0, 2)
    return jnp.asarray(ovl, BF16), jnp.asarray(expand, BF16), jnp.asarray(expand_wide, BF16)


def _nsa_kernel(q_ref, kc_ref, vc_ref, ks_ref, vs_ref, kw_ref, vw_ref, gate_ref,
                bc_ref, bw_ref, bs_ref, ovl_ref, exp_ref, expw_ref, gn_ref, o_ref,
                qs_ref, s_ref, m_ref, acc_ref, uns_ref, oc_ref, os_ref, ob_ref):
    i = pl.program_id(1)
    rows = GQA * TQ
    n_sel = ovl_ref.shape[0]
    n_band = WINDOW // TQ + 1
    batch = range(NSA_NB)
    units = [(b, h) for b in batch for h in range(HKV)]
    nt_dims = (((1,), (1,)), ((), ()))

    for b in batch:
        for hd in range(NSA_HEADS):
            qs_ref[b, hd // GQA, (hd % GQA) * TQ:(hd % GQA + 1) * TQ, :] = q_ref[b, :, hd * LANES:(hd + 1) * LANES]

    def chunk_rows(c, n=1):
        return pl.ds(pl.multiple_of(c * TQ, TQ), n * TQ)

    def row_max_to_lanes():
        for u in units:
            m_ref[u] = jnp.broadcast_to(jnp.max(m_ref[u], axis=-1, keepdims=True), (rows, TQ))

    def normalise(acc):
        row_sum = pltpu.roll(acc, DH, 1)
        return acc / jnp.maximum(row_sum, 1e-30)

    for b, h in units:
        bias_c = bc_ref[h].reshape(rows, LANES)
        sc = lax.dot_general(qs_ref[b, h], kc_ref[b], nt_dims, preferred_element_type=F32) + bias_c
        mc = jnp.max(sc, axis=-1, keepdims=True)
        pc = jnp.where(bias_c > 0.5 * NEG, jnp.exp2(sc - mc), 0.0)
        pc = pc / jnp.maximum(jnp.sum(pc, axis=-1, keepdims=True), 1e-30)
        oc_ref[b, h] = jnp.dot(pc.astype(BF16), vc_ref[b, h], preferred_element_type=F32)

        psum = jnp.sum(pc.reshape(GQA, TQ, LANES), axis=0)
        p_hi = psum.astype(BF16)
        p_lo = (psum - p_hi.astype(F32)).astype(BF16)
        imp = (lax.dot_general(ovl_ref[...], p_hi, nt_dims, preferred_element_type=F32)
               + lax.dot_general(ovl_ref[...], p_lo, nt_dims, preferred_element_type=F32))
        blk = lax.broadcasted_iota(jnp.int32, (n_sel, TQ), 0)
        tok = lax.broadcasted_iota(jnp.int32, (n_sel, TQ), 1) + i * TQ
        blk_of_t = tok // SEL_BLOCK
        forced = (blk == 0) | (blk == blk_of_t) | (blk == blk_of_t - 1)
        score = jnp.where(forced, FORCED_SCORE, jnp.where(blk <= blk_of_t, imp, -1.0))
        rank = jnp.zeros((n_sel, TQ), F32)
        for mm in range(n_sel):
            sm = score[mm:mm + 1, :]
            ahead = (sm > score) | ((sm == score) & (blk > mm))
            rank = rank + jnp.where(ahead, 1.0, 0.0)
        unsel_t = jnp.where(rank < min(SEL_TOP, n_sel), 0.0, 1.0)
        unsel_t = jnp.concatenate([unsel_t, jnp.zeros((LANES - n_sel, TQ), F32)], axis=0).astype(BF16)
        eye = (lax.broadcasted_iota(jnp.int32, (TQ, TQ), 0)
               == lax.broadcasted_iota(jnp.int32, (TQ, TQ), 1)).astype(BF16)
        uns_ref[b, h] = lax.dot_general(eye, unsel_t, nt_dims, preferred_element_type=F32).astype(BF16)

    def sel_scores(c, bias_cols):
        for b in batch:
            k = ks_ref[b, chunk_rows(c), :]
            for h in range(HKV):
                madd = jnp.dot(uns_ref[b, h], exp_ref[c], preferred_element_type=F32)
                s = lax.dot_general(qs_ref[b, h], k, nt_dims, preferred_element_type=F32)
                s = (s.reshape(GQA, TQ, TQ) + madd[None]).reshape(rows, TQ)
                if bias_cols is not None:
                    s = s + bs_ref[h, :, bias_cols:bias_cols + TQ]
                s_ref[b, h, c] = s
                m_ref[b, h] = jnp.maximum(m_ref[b, h], s)

    def sel_scores_wide(j):
        for b in batch:
            k = ks_ref[b, chunk_rows(2 * j, 2), :]
            for h in range(HKV):
                madd = jnp.dot(uns_ref[b, h], expw_ref[j], preferred_element_type=F32)
                s = lax.dot_general(qs_ref[b, h], k, nt_dims, preferred_element_type=F32)
                s = (s.reshape(GQA, TQ, 2 * TQ) + madd[None]).reshape(rows, 2 * TQ)
                s_ref[b, h, 2 * j] = s[:, :TQ]
                s_ref[b, h, 2 * j + 1] = s[:, TQ:]
                m_ref[b, h] = jnp.maximum(m_ref[b, h], jnp.maximum(s[:, :TQ], s[:, TQ:]))

    def softmax_pv(c):
        for b, h in units:
            p = jnp.exp2(s_ref[b, h, c] - m_ref[b, h])
            acc_ref[b, h] += jnp.dot(p.astype(BF16), vs_ref[h, b, chunk_rows(c), :], preferred_element_type=F32)

    def softmax_pv_wide(j):
        for b, h in units:
            m = m_ref[b, h]
            p = jnp.concatenate([jnp.exp2(s_ref[b, h, 2 * j] - m), jnp.exp2(s_ref[b, h, 2 * j + 1] - m)], axis=1)
            acc_ref[b, h] += jnp.dot(p.astype(BF16), vs_ref[h, b, chunk_rows(2 * j, 2), :],
                                     preferred_element_type=F32)

    def loop(n, body):
        lax.fori_loop(0, n, lambda j, carry: (body(j), carry)[1], 0)

    m_ref[...] = jnp.full(m_ref.shape, NEG, F32)
    n_far = jnp.maximum(i - 1, 0)
    loop(n_far // 2, sel_scores_wide)
    pl.when(n_far % 2 == 1)(lambda: sel_scores(n_far - 1, None))
    pl.when(i >= 1)(lambda: sel_scores(i - 1, 0))
    sel_scores(i, TQ)
    row_max_to_lanes()
    acc_ref[...] = jnp.zeros(acc_ref.shape, F32)
    loop((i + 1) // 2, softmax_pv_wide)
    pl.when((i + 1) % 2 == 1)(lambda: softmax_pv(i))
    for u in units:
        os_ref[u] = normalise(acc_ref[u])

    band = WINDOW + TQ
    col = lax.broadcasted_iota(jnp.int32, (1, band), 1)
    before_start = jnp.where(col + i * TQ >= WINDOW, 0.0, NEG)
    o_win = {}
    for b, h in units:
        k = kw_ref[b, chunk_rows(i, n_band), :]
        s = lax.dot_general(qs_ref[b, h], k, nt_dims, preferred_element_type=F32) + bw_ref[h] + before_start
        p = jnp.exp2(s - jnp.max(s, axis=-1, keepdims=True))
        o_win[b, h] = normalise(jnp.dot(p.astype(BF16), vw_ref[h, b, chunk_rows(i, n_band), :],
                                        preferred_element_type=F32))

    for b in batch:
        gates = gate_ref[b]
        for hd in range(NSA_HEADS):
            h, g = hd // GQA, hd % GQA
            r = slice(g * TQ, (g + 1) * TQ)
            o = (gates[:, hd:hd + 1] * oc_ref[b, h, r, :]
                 + gates[:, NSA_HEADS + hd:NSA_HEADS + hd + 1] * os_ref[b, h, r, :]
                 + gates[:, 2 * NSA_HEADS + hd:2 * NSA_HEADS + hd + 1] * o_win[b, h][r])
            ob_ref[b, :, hd * DH:(hd + 1) * DH] = o[:, :DH]
        o = ob_ref[b]
        ms = jnp.mean(o * o, axis=-1, keepdims=True)
        o_ref[b] = (o * lax.rsqrt(ms + EPS) * gn_ref[...]).astype(BF16)


def _nsa(q, kc, vc, ks, vs, kw, vw, gates, tables, attn_out_norm):
    b, s, _ = q.shape
    bias_c, bias_w, bias_s, ovl, expand, expand_wide = tables
    gn = attn_out_norm.reshape(1, NSA_WIDTH)
    nb = NSA_NB
    assert b % nb == 0 and (s // TQ) % 2 == 0
    full = lambda a: pl.BlockSpec(a.shape, lambda bi, i: (0,) * a.ndim)
    tile = lambda w: pl.BlockSpec((nb, TQ, w), lambda bi, i: (bi, i, 0))
    kseq = lambda a: pl.BlockSpec((nb,) + a.shape[1:], lambda bi, i: (bi, 0, 0))
    vseq = lambda a: pl.BlockSpec((HKV, nb) + a.shape[2:], lambda bi, i: (0, bi, 0, 0))
    assert kw.shape[1] == s + WINDOW and vw.shape[2] == s + WINDOW
    rows = GQA * TQ
    unit = (nb, HKV)
    return pl.pallas_call(
        _nsa_kernel,
        out_shape=jax.ShapeDtypeStruct((b, s, NSA_WIDTH), BF16),
        grid=(b // nb, s // TQ),
        in_specs=[tile(NSA_HEADS * LANES),
                  pl.BlockSpec((nb,) + kc.shape[1:], lambda bi, i: (bi, 0, 0)),
                  pl.BlockSpec((nb,) + vc.shape[1:], lambda bi, i: (bi, 0, 0, 0)),
                  kseq(ks), vseq(vs), kseq(kw), vseq(vw), tile(LANES),
                  pl.BlockSpec((HKV, GQA, TQ, LANES), lambda bi, i: (0, 0, i, 0)),
                  full(bias_w), full(bias_s), full(ovl), full(expand), full(expand_wide), full(gn)],
        out_specs=tile(NSA_WIDTH),
        scratch_shapes=[pltpu.VMEM(unit + (rows, LANES), BF16),
                        pltpu.VMEM(unit + (s // TQ, rows, TQ), F32),
                        pltpu.VMEM(unit + (rows, TQ), F32),
                        pltpu.VMEM(unit + (rows, LANES), F32),
                        pltpu.VMEM(unit + (TQ, LANES), BF16),
                        pltpu.VMEM(unit + (rows, LANES), F32),
                        pltpu.VMEM(unit + (rows, LANES), F32),
                        pltpu.VMEM((nb, TQ, NSA_WIDTH), F32)],
        compiler_params=_cparams("parallel", "arbitrary"),
        name="nsa_attention",
    )(q, kc, vc, ks, vs, kw, vw, gates, bias_c, bias_w, bias_s, ovl, expand, expand_wide, gn)


def _ssd_kernel(xbc_ref, z_ref, dt_ref, cw_ref, cb_ref, dtb_ref, alog_ref, dsk_ref, ng_ref, wsrc_ref, o_ref, wdst_ref,
                xbuf_ref, state_ref, y_ref):
    c = pl.program_id(1)
    wdst_ref[...] = wsrc_ref[...].astype(BF16)
    L, P, N = SSD_L, SSD_P, SSD_N
    hpg = SSD_HEADS // SSD_GROUPS
    pad = 8

    @pl.when(c == 0)
    def _():
        xbuf_ref[:, 0:pad, :] = jnp.zeros((SSD_NB, pad, SSD_CONV_DIM), F32)
        state_ref[...] = jnp.zeros(state_ref.shape, F32)

    ri = lax.broadcasted_iota(jnp.int32, (L, L), 0)
    ci = lax.broadcasted_iota(jnp.int32, (L, L), 1)
    causal = ri >= ci
    nt_dims = (((1,), (1,)), ((), ()))
    a_neg = -jnp.exp(alog_ref[...])

    for bb in range(SSD_NB):
        xbuf_ref[bb, pad:pad + L, :] = xbc_ref[bb]
        conv = cb_ref[...]
        for k in range(SSD_CONV):
            shift = SSD_CONV - 1 - k
            conv = conv + xbuf_ref[bb, pad - shift:pad - shift + L, :] * cw_ref[k:k + 1, :]
        xbuf_ref[bb, 0:pad, :] = xbuf_ref[bb, L:L + pad, :]
        xa = _silu(conv)
        xs = xa[:, :SSD_WIDTH]
        bm = xa[:, SSD_WIDTH:SSD_WIDTH + SSD_GROUPS * N]
        cm = xa[:, SSD_WIDTH + SSD_GROUPS * N:]

        dtv = dt_ref[bb] + dtb_ref[...]
        dt = jnp.maximum(dtv, 0.0) + jnp.log1p(jnp.exp(-jnp.abs(dtv)))
        cs = jnp.dot(causal.astype(F32), dt * a_neg, preferred_element_type=F32,
                     precision=lax.Precision.HIGHEST)
        cs_t = cs.T

        for gr in range(SSD_GROUPS):
            b_g = bm[:, gr * N:(gr + 1) * N]
            c_g = cm[:, gr * N:(gr + 1) * N]
            scores = lax.dot_general(c_g.astype(BF16), b_g.astype(BF16), nt_dims, preferred_element_type=F32)
            b_gt = b_g.T
            for hh in range(hpg):
                h = gr * hpg + hh
                cs_col = cs[:, h:h + 1]
                cs_row = cs_t[h:h + 1, :]
                cs_last = cs[L - 1:L, h:h + 1]
                decay = jnp.exp(jnp.where(causal, cs_col - cs_row, NEG))
                xs_h = xs[:, h * P:(h + 1) * P]
                xc = (xs_h * dt[:, h:h + 1]).astype(BF16)
                y = jnp.dot((scores * decay).astype(BF16), xc, preferred_element_type=F32)
                prev = state_ref[bb, h]
                y = y + jnp.dot((c_g * jnp.exp(cs_col)).astype(BF16), prev.astype(BF16),
                                preferred_element_type=F32)
                contrib = jnp.dot((b_gt * jnp.exp(cs_last - cs_row)).astype(BF16), xc,
                                  preferred_element_type=F32)
                state_ref[bb, h] = jnp.exp(cs_last) * prev + contrib
                y_ref[bb, :, h * P:(h + 1) * P] = y + xs_h * dsk_ref[:, h * P:(h + 1) * P]

        y = y_ref[bb] * _silu(z_ref[bb])
        gw = SSD_WIDTH // SSD_GROUPS
        for gr in range(SSD_GROUPS):
            yg = y[:, gr * gw:(gr + 1) * gw]
            ms = jnp.mean(yg * yg, axis=-1, keepdims=True)
            o_ref[bb, :, gr * gw:(gr + 1) * gw] = (yg * lax.rsqrt(ms + EPS)
                                                   * ng_ref[:, gr * gw:(gr + 1) * gw]).astype(BF16)


def _ssd(xbc, z, dt, conv_w, conv_b, dt_bias, a_log, d_skip, norm_g, w_cast):
    b, s, _ = xbc.shape
    nb = SSD_NB
    assert b % nb == 0
    n_steps = (b // nb) * (s // SSD_L)
    wr, wc = w_cast.shape
    assert wr % n_steps == 0
    wblk = pl.BlockSpec((wr // n_steps, wc), lambda bi, c: (bi * (s // SSD_L) + c, 0))
    padl = lambda v: jnp.pad(v, (0, LANES - v.shape[0])).reshape(1, LANES)
    args = (conv_w, conv_b.reshape(1, SSD_CONV_DIM), padl(dt_bias), padl(a_log),
            jnp.repeat(d_skip, SSD_P).reshape(1, SSD_WIDTH), norm_g.reshape(1, SSD_WIDTH))
    full = lambda a: pl.BlockSpec(a.shape, lambda bi, c: (0,) * a.ndim)
    blk = lambda w: pl.BlockSpec((nb, SSD_L, w), lambda bi, c: (bi, c, 0))
    return pl.pallas_call(
        _ssd_kernel,
        out_shape=[jax.ShapeDtypeStruct((b, s, SSD_WIDTH), BF16), jax.ShapeDtypeStruct((wr, wc), BF16)],
        grid=(b // nb, s // SSD_L),
        in_specs=[blk(SSD_CONV_DIM), blk(SSD_WIDTH), blk(LANES)] + [full(a) for a in args] + [wblk],
        out_specs=[blk(SSD_WIDTH), wblk],
        scratch_shapes=[pltpu.VMEM((nb, SSD_L + 8, SSD_CONV_DIM), F32),
                        pltpu.VMEM((nb, SSD_HEADS, SSD_N, SSD_P), F32),
                        pltpu.VMEM((nb, SSD_L, SSD_WIDTH), F32)],
        compiler_params=_cparams("parallel", "arbitrary"),
        name="ssd_mixer",
    )(xbc, z, dt, *args, w_cast)


def _out_proj_kernel(x_ref, on_ref, os_ref, wo_ref, fg_ref, rwt_ref, rbt_ref, wsrc_ref,
                     h_ref, hn_ref, comb_ref, slot_ref, wdst_ref):
    wdst_ref[...] = wsrc_ref[...].astype(BF16)
    h = (x_ref[...]
         + jnp.dot(on_ref[...], wo_ref[0:NSA_WIDTH, :], preferred_element_type=F32)
         + jnp.dot(os_ref[...], wo_ref[NSA_WIDTH:, :], preferred_element_type=F32))
    tm = h.shape[0]
    for j in range(H_SLAB):
        h_ref[pl.ds(j, tm, stride=H_SLAB), :] = h[:, j * LANES:(j + 1) * LANES]
    ms = jnp.mean(h * h, axis=-1, keepdims=True)
    hn = h * lax.rsqrt(ms + EPS) * fg_ref[...]
    for j in range(H_SLAB):
        hn_ref[pl.ds(j, tm, stride=H_SLAB), :] = hn[:, j * LANES:(j + 1) * LANES]

    nt_dims = (((1,), (1,)), ((), ()))
    hn_hi = hn.astype(BF16)
    hn_lo = (hn - hn_hi.astype(F32)).astype(BF16)
    rwt = rwt_ref[...]
    rwt_hi = rwt.astype(BF16)
    rwt_lo = (rwt - rwt_hi.astype(F32)).astype(BF16)
    both = lax.dot_general(jnp.concatenate([rwt_hi, rwt_lo], axis=0), hn_hi, nt_dims, preferred_element_type=F32)
    logits = (both[:N_EXPERTS] + both[N_EXPERTS:]
              + lax.dot_general(rwt_hi, hn_lo, nt_dims, preferred_element_type=F32) + rbt_ref[...])
    row = lax.broadcasted_iota(jnp.int32, logits.shape, 0)
    work = logits
    picks = []
    for _ in range(TOP_K):
        v = jnp.max(work, axis=0, keepdims=True)
        idx = jnp.min(jnp.where(work == v, row, N_EXPERTS), axis=0, keepdims=True)
        hit = row == idx
        picks.append((v, hit))
        work = jnp.where(hit, -3e38, work)
    v0 = picks[0][0]
    es = [jnp.exp(v - v0) for v, _ in picks]
    den = es[0] + es[1] + es[2] + es[3]
    comb = jnp.zeros_like(logits)
    slot = jnp.zeros(logits.shape, jnp.int32)
    for k, (e, (_, hit)) in enumerate(zip(es, picks)):
        comb = comb + jnp.where(hit, e / den, 0.0)
        slot = jnp.where(hit, k + 1, slot)
    comb_ref[...] = comb
    slot_ref[...] = slot


def _out_proj(x2, o_nsa, o_ssd, w_out, ffn_norm, router_w, router_b, w_cast, tm=512):
    t = x2.shape[0]
    wr, wc = w_cast.shape
    assert wr % (t // tm) == 0
    wblk = pl.BlockSpec((wr // (t // tm), wc), lambda i: (i, 0))
    row = lambda w: pl.BlockSpec((tm, w), lambda i: (i, 0))
    full = lambda a: pl.BlockSpec(a.shape, lambda i: (0,) * a.ndim)
    wo = w_out.astype(BF16)
    fg = ffn_norm.reshape(1, D_MODEL)
    rwt = router_w.T
    rbt = router_b.reshape(N_EXPERTS, 1)
    tok = pl.BlockSpec((N_EXPERTS, tm), lambda i: (0, i))
    return pl.pallas_call(
        _out_proj_kernel,
        out_shape=[jax.ShapeDtypeStruct((t * H_SLAB, LANES), F32),
                   jax.ShapeDtypeStruct((t * H_SLAB, LANES), F32),
                   jax.ShapeDtypeStruct((N_EXPERTS, t), F32),
                   jax.ShapeDtypeStruct((N_EXPERTS, t), jnp.int32),
                   jax.ShapeDtypeStruct((wr, wc), BF16)],
        grid=(t // tm,),
        in_specs=[row(D_MODEL), row(NSA_WIDTH), row(SSD_WIDTH), full(wo), full(fg), full(rwt), full(rbt), wblk],
        out_specs=[pl.BlockSpec((tm * H_SLAB, LANES), lambda i: (i, 0)),
                   pl.BlockSpec((tm * H_SLAB, LANES), lambda i: (i, 0)), tok, tok, wblk],
        compiler_params=_cparams("parallel"),
        name="out_proj_router",
    )(x2, o_nsa, o_ssd, wo, fg, rwt, rbt, w_cast)


MOE_CHUNK = 4096
MOE_TILE = 256
ROUTE_TILE = 256


def _route_kernel(comb_ref, slot_ref, dest_ref, wrow_ref, starts_ref, pos_ref):
    ne, tc = comb_ref.shape
    ri = lax.broadcasted_iota(jnp.int32, (ROUTE_TILE, ROUTE_TILE), 0)
    ci = lax.broadcasted_iota(jnp.int32, (ROUTE_TILE, ROUTE_TILE), 1)
    earlier = (ri < ci).astype(BF16)
    carry = jnp.zeros((ne, 1), F32)
    for j in range(tc // ROUTE_TILE):
        cols = slice(j * ROUTE_TILE, (j + 1) * ROUTE_TILE)
        sel = jnp.where(slot_ref[:, cols] > 0, 1.0, 0.0)
        pos_ref[:, cols] = jnp.dot(sel.astype(BF16), earlier, preferred_element_type=F32) + carry
        carry = carry + jnp.sum(sel, axis=1, keepdims=True)
    ei = lax.broadcasted_iota(jnp.int32, (ne, LANES), 0)
    li = lax.broadcasted_iota(jnp.int32, (ne, LANES), 1)
    starts_row = jnp.sum(jnp.where(ei < li, carry, 0.0), axis=0, keepdims=True)
    starts_col = jnp.sum(jnp.where(ei == li, starts_row, 0.0), axis=1, keepdims=True)
    starts_ref[0] = jnp.broadcast_to(starts_row, (SUBLANES, LANES)).astype(jnp.int32)
    dest = pos_ref[...] + starts_col
    slot = slot_ref[...]
    comb = comb_ref[...]
    for k in range(TOP_K):
        hit = slot == k + 1
        dest_ref[0, :, k * tc:(k + 1) * tc] = jnp.sum(jnp.where(hit, dest, 0.0), axis=0,
                                                      keepdims=True).astype(jnp.int32)
        wrow_ref[0, :, k * tc:(k + 1) * tc] = jnp.sum(jnp.where(hit, comb, 0.0), axis=0, keepdims=True)


def _route(comb, slot, tc):
    ne, t = comb.shape
    nch = t // tc
    blk = pl.BlockSpec((ne, tc), lambda c: (0, c))
    return pl.pallas_call(
        _route_kernel,
        out_shape=[jax.ShapeDtypeStruct((nch, 1, TOP_K * tc), jnp.int32),
                   jax.ShapeDtypeStruct((nch, 1, TOP_K * tc), F32),
                   jax.ShapeDtypeStruct((nch, SUBLANES, LANES), jnp.int32)],
        grid=(nch,),
        in_specs=[blk, blk],
        out_specs=[pl.BlockSpec((1, 1, TOP_K * tc), lambda c: (c, 0, 0)),
                   pl.BlockSpec((1, 1, TOP_K * tc), lambda c: (c, 0, 0)),
                   pl.BlockSpec((1, SUBLANES, LANES), lambda c: (c, 0, 0))],
        scratch_shapes=[pltpu.VMEM((ne, tc), F32)],
        compiler_params=_cparams("parallel"),
        name="moe_route",
    )(comb, slot)


def _moe_kernel(starts_ref, dest_hbm, wrow_hbm, hn_hbm, h_hbm, wgu_ref, bgu_ref, wd_ref, bd_ref, fn_ref, o_hbm,
                x_ref, acc_ref, xs0_ref, xs1_ref, y0_ref, y1_ref, ob_ref, dest_s, wrow_s, tok_s, wt_s, st_s,
                sem, osem):
    c = pl.program_id(0)
    e = pl.program_id(1)
    tc = x_ref.shape[0] // H_SLAB
    n_rows = TOP_K * tc

    def slab(i, n, width):
        return pl.ds(pl.multiple_of(i * width, width), n * width)

    half_len = n_rows + MOE_TILE
    cur = (c % 2) * half_len
    nxt = half_len - cur

    def invert(j0, u, half):
        r = half + dest_s[j0 + u]
        tok_s[r] = (j0 & (tc - 1)) + u
        wt_s[r] = wrow_s[j0 + u]

    def gather(r0, xs_ref):
        i0 = cur + r0
        for r in range(MOE_TILE):
            xs_ref[slab(r, 1, H_SLAB), :] = x_ref[slab(tok_s[i0 + r], 1, H_SLAB), :]

    def ffn(xs_ref, y_ref):
        x = jnp.concatenate([xs_ref[pl.ds(j, MOE_TILE, stride=H_SLAB), :].astype(BF16) for j in range(H_SLAB)],
                            axis=1)
        gu = jnp.dot(x, wgu_ref[0], preferred_element_type=F32) + bgu_ref[0]
        gate = jnp.minimum(gu[:, :D_FF], SWIGLU_LIMIT)
        up = jnp.clip(gu[:, D_FF:], -SWIGLU_LIMIT, SWIGLU_LIMIT)
        act = (up + 1.0) * gate / (1.0 + jnp.exp(-SWIGLU_ALPHA * gate))
        y = jnp.dot(act.astype(BF16), wd_ref[0], preferred_element_type=F32) + bd_ref[0]
        for j in range(H_SLAB):
            y_ref[pl.ds(j, MOE_TILE, stride=H_SLAB), :] = y[:, j * LANES:(j + 1) * LANES]

    def scatter(r0, n_valid, y_ref):
        i0 = cur + r0
        for g in range(MOE_TILE // SUBLANES):
            rows = []
            for u in range(SUBLANES):
                r = g * SUBLANES + u
                tok = jnp.where(r < n_valid, tok_s[i0 + r], tc)
                rows.append((tok, acc_ref[slab(tok, 1, H_SLAB), :] + wt_s[i0 + r] * y_ref[slab(r, 1, H_SLAB), :]))
            for tok, val in rows:
                acc_ref[slab(tok, 1, H_SLAB), :] = val

    chunk_copies = lambda: (pltpu.make_async_copy(hn_hbm.at[slab(c * tc, tc, H_SLAB), :], x_ref, sem.at[0]),
                            pltpu.make_async_copy(h_hbm.at[slab(c * tc, tc, H_SLAB), :],
                                                  acc_ref.at[pl.ds(0, tc * H_SLAB), :], sem.at[1]))
    map_copies = lambda cc: (pltpu.make_async_copy(dest_hbm.at[cc], dest_s, sem.at[2]),
                             pltpu.make_async_copy(wrow_hbm.at[cc], wrow_s, sem.at[3]))

    @pl.when(e == 0)
    def _():
        for cp in chunk_copies():
            cp.start()

        @pl.when(c == 0)
        def _():
            for cp in map_copies(0):
                cp.start()
            for cp in map_copies(0):
                cp.wait()

            def invert_slice(g, carry):
                for u in range(MOE_TILE):
                    invert(g * MOE_TILE, u, cur)
                return carry

            lax.fori_loop(0, n_rows // MOE_TILE, invert_slice, 0)
            for half in (0, half_len):
                for u in range(MOE_TILE):
                    tok_s[half + n_rows + u] = 0
                    wt_s[half + n_rows + u] = 0.0

        @pl.when(c + 1 < pl.num_programs(0))
        def _():
            for cp in map_copies(c + 1):
                cp.start()
            for cp in map_copies(c + 1):
                cp.wait()

        for cp in chunk_copies():
            cp.wait()
        acc_ref[pl.ds(tc * H_SLAB, H_SLAB), :] = jnp.zeros((H_SLAB, LANES), F32)
        st_s[0] = 0
        st_s[1] = 0
        st_s[2] = 0
        for y_ref in (y0_ref, y1_ref):
            y_ref[...] = jnp.zeros(y_ref.shape, F32)
        gather(0, xs0_ref)

    base = starts_ref[c * LANES + e]
    n_e = starts_ref[c * LANES + e + 1] - base
    xs_bufs = (xs0_ref, xs1_ref)
    y_bufs = (y0_ref, y1_ref)

    def tile_step(p, k, r0_next, prev_r0, prev_nv):
        first = jnp.minimum(k, n_rows // MOE_TILE - 1) * MOE_TILE
        for u in range(MOE_TILE):
            invert(first, u, nxt)
        gather(r0_next, xs_bufs[1 - p])
        scatter(prev_r0, prev_nv, y_bufs[1 - p])
        ffn(xs_bufs[p], y_bufs[p])

    def tile(j, carry):
        r0 = base + j * MOE_TILE
        r0_next = jnp.minimum(r0 + MOE_TILE, base + n_e)
        k = st_s[0]
        prev_r0 = st_s[1]
        prev_nv = st_s[2]
        for p in range(2):
            pl.when(k % 2 == p)(functools.partial(tile_step, p, k, r0_next, prev_r0, prev_nv))
        st_s[0] = k + 1
        st_s[1] = r0
        st_s[2] = jnp.minimum(n_e - j * MOE_TILE, MOE_TILE)
        return carry

    lax.fori_loop(0, (n_e + MOE_TILE - 1) // MOE_TILE, tile, 0)

    @pl.when(e == pl.num_programs(1) - 1)
    def _():
        for p in range(2):
            pl.when(st_s[0] % 2 == p)(functools.partial(scatter, st_s[1], st_s[2], y_bufs[1 - p]))
        n_groups = tc // MOE_TILE
        out_copy = lambda g, buf: pltpu.make_async_copy(
            ob_ref.at[buf], o_hbm.at[pl.ds(pl.multiple_of(c * tc + g * MOE_TILE, MOE_TILE), MOE_TILE), :],
            osem.at[buf])

        def norm(g, carry):
            buf = g % 2

            @pl.when(g >= 2)
            def _():
                out_copy(g - 2, buf).wait()

            first = pl.multiple_of(g * MOE_TILE * H_SLAB, MOE_TILE * H_SLAB)
            hs = [acc_ref[pl.ds(first + j, MOE_TILE, stride=H_SLAB), :] for j in range(H_SLAB)]
            ss = hs[0] * hs[0]
            for hj in hs[1:]:
                ss = ss + hj * hj
            inv = lax.rsqrt(jnp.sum(ss, axis=-1, keepdims=True) / D_MODEL + EPS)
            for j, hj in enumerate(hs):
                ob_ref[buf, :, j * LANES:(j + 1) * LANES] = hj * inv * fn_ref[:, j * LANES:(j + 1) * LANES]
            out_copy(g, buf).start()
            return carry

        lax.fori_loop(0, n_groups, norm, 0)
        for g in range(max(n_groups - 2, 0), n_groups):
            out_copy(g, g % 2).wait()


def _moe(hnp, comb, slot, h1, w_gate_up, b_gate_up, w_down, b_down, final_norm, tc=MOE_CHUNK):
    t = comb.shape[1]
    tc = min(tc, t)
    assert tc & (tc - 1) == 0 and tc % MOE_TILE == 0
    nch = t // tc
    dest, wrow, starts = _route(comb, slot, tc)
    dest = dest.reshape(nch, TOP_K * tc)
    wrow = wrow.reshape(nch, TOP_K * tc)
    starts = starts[:, 0, :].reshape(nch * LANES)
    wgu = w_gate_up.astype(BF16)
    wd = w_down.astype(BF16)
    bgu = b_gate_up.reshape(N_EXPERTS, 1, 2 * D_FF)
    bd = b_down.reshape(N_EXPERTS, 1, D_MODEL)
    fn = final_norm.reshape(1, D_MODEL)
    anyspace = pl.BlockSpec(memory_space=pl.ANY)
    exp = lambda a: pl.BlockSpec((1,) + a.shape[1:], lambda c, e, st: (e, 0, 0))
    return pl.pallas_call(
        _moe_kernel,
        out_shape=jax.ShapeDtypeStruct((t, D_MODEL), F32),
        grid_spec=pltpu.PrefetchScalarGridSpec(
            num_scalar_prefetch=1,
            grid=(nch, N_EXPERTS),
            in_specs=[anyspace, anyspace, anyspace, anyspace, exp(wgu), exp(bgu), exp(wd), exp(bd),
                      pl.BlockSpec(fn.shape, lambda c, e, st: (0, 0))],
            out_specs=anyspace,
            scratch_shapes=[pltpu.VMEM((tc * H_SLAB, LANES), F32),
                            pltpu.VMEM(((tc + 1) * H_SLAB, LANES), F32),
                            pltpu.VMEM((MOE_TILE * H_SLAB, LANES), F32),
                            pltpu.VMEM((MOE_TILE * H_SLAB, LANES), F32),
                            pltpu.VMEM((MOE_TILE * H_SLAB, LANES), F32),
                            pltpu.VMEM((MOE_TILE * H_SLAB, LANES), F32),
                            pltpu.VMEM((2, MOE_TILE, D_MODEL), F32),
                            pltpu.SMEM((TOP_K * tc,), jnp.int32),
                            pltpu.SMEM((TOP_K * tc,), F32),
                            pltpu.SMEM((2 * (TOP_K * tc + MOE_TILE),), jnp.int32),
                            pltpu.SMEM((2 * (TOP_K * tc + MOE_TILE),), F32),
                            pltpu.SMEM((4,), jnp.int32),
                            pltpu.SemaphoreType.DMA((4,)),
                            pltpu.SemaphoreType.DMA((2,))]),
        compiler_params=_cparams("arbitrary", "arbitrary", vmem=MOE_VMEM_LIMIT),
        name="moe_experts",
    )(starts, dest, wrow, hnp, h1, wgu, bgu, wd, bd, fn)


def kernel(x, attn_norm, w_in, rel_bias, cmp_pos, cmp_w1, cmp_w2, attn_out_norm, conv_w, conv_b,
           dt_bias, a_log, d_skip, ssm_out_norm, w_out, ffn_norm, router_w, router_b,
           w_gate_up, b_gate_up, w_down, b_down, final_norm):
    b, s, d = x.shape
    t = b * s
    depth = w_in.shape[0]
    tables = tuple(_bias_tables(rel_bias, s)) + _sel_tables(s)
    h = x.reshape(t, d)
    for l in range(depth):
        (q, kc_raw, vc_raw, ks, vs, kw, vw, gates, z, xbc, dt) = _in_proj(h, attn_norm[l], _pad_w_in(w_in[l]))
        grp = CMP_STRIDE * HKV * DH
        kc, vc = _compress(kc_raw.reshape(b, s // CMP_STRIDE, grp), vc_raw.reshape(b, s // CMP_STRIDE, grp),
                           _compress_weights(cmp_pos[l], cmp_w1[l], cmp_w2[l]))
        seq = lambda a: a.reshape(a.shape[:-2] + (b, s, a.shape[-1]))
        front = lambda a: jnp.pad(a, [(0, 0)] * (a.ndim - 2) + [(WINDOW, 0), (0, 0)])
        o_nsa = _nsa(seq(q), kc, vc, seq(ks), seq(vs), front(seq(kw)), front(seq(vw)), seq(gates), tables,
                     attn_out_norm[l])
        o_ssd, wgu = _ssd(seq(xbc), seq(z), seq(dt), conv_w[l], conv_b[l], dt_bias[l], a_log[l], d_skip[l],
                          ssm_out_norm[l], w_gate_up[l].reshape(N_EXPERTS * D_MODEL, 2 * D_FF))
        h1, hnp, comb, slot, wd = _out_proj(h, o_nsa.reshape(t, NSA_WIDTH), o_ssd.reshape(t, SSD_WIDTH), w_out[l],
                                            ffn_norm[l], router_w[l], router_b[l],
                                            w_down[l].reshape(N_EXPERTS * D_FF, D_MODEL))
        assert depth == 1
        h = _moe(hnp, comb, slot, h1, wgu.reshape(N_EXPERTS, D_MODEL, 2 * D_FF), b_gate_up[l],
                 wd.reshape(N_EXPERTS, D_FF, D_MODEL), b_down[l], final_norm)
    return h.reshape(b, s, d)
```

```python
import functools
import math

import numpy as np
import jax
import jax.numpy as jnp
from jax import lax
from jax.experimental import pallas as pl
from jax.experimental.pallas import tpu as pltpu

F32 = jnp.float32
BF16 = jnp.bfloat16

D_MODEL = 1024
NSA_HEADS = 8
HKV = 2
GQA = NSA_HEADS // HKV
DH = 64
NSA_WIDTH = NSA_HEADS * DH
CMP_BLOCK = 32
CMP_STRIDE = 16
SEL_BLOCK = 64
SEL_TOP = 16
WINDOW = 512
TQ = 128
NSA_NB = 2
SSD_HEADS = 8
SSD_P = 64
SSD_WIDTH = SSD_HEADS * SSD_P
SSD_GROUPS = 2
SSD_N = 128
SSD_CONV = 4
SSD_L = 128
SSD_NB = 2
SSD_CONV_DIM = SSD_WIDTH + 2 * SSD_GROUPS * SSD_N
N_BUCKETS = 32
MAX_DISTANCE = 128
N_EXPERTS = 32
TOP_K = 4
D_FF = 1024
SWIGLU_LIMIT = 7.0
SWIGLU_ALPHA = 1.702

EPS = 1e-6
NEG = -1e30
FORCED_SCORE = 1e4
LOG2E = 1.4426950408889634
LANES = 128
SUBLANES = 8
H_SLAB = D_MODEL // LANES
VMEM_LIMIT = 56 * 1024 * 1024
MOE_VMEM_LIMIT = (2 * 4096 * D_MODEL * 4 + 2 * 3 * D_MODEL * D_FF * 2 + 12 * 1024 * 1024)


def _cparams(*sem, vmem=VMEM_LIMIT):
    return pltpu.CompilerParams(dimension_semantics=sem, vmem_limit_bytes=vmem)


def _silu(v):
    return v / (1.0 + jnp.exp(-v))


_Q0, _Q1 = 0, NSA_HEADS * LANES
_KV0 = _Q1
_G0 = _KV0 + 6 * LANES
_DT0 = _G0 + LANES
_Z0 = _DT0 + LANES
_X0 = _Z0 + SSD_WIDTH
_WCOLS = _X0 + SSD_CONV_DIM
MXU_N = 256


def _pad_w_in(w_in):
    d = w_in.shape[0]
    nsa_cols = NSA_WIDTH + 6 * HKV * DH + 3 * NSA_HEADS
    wq = w_in[:, :NSA_WIDTH].reshape(d, HKV, GQA, DH)
    zq = jnp.zeros_like(wq)
    q0 = jnp.concatenate([wq[:, 0], zq[:, 0]], axis=-1)
    q1 = jnp.concatenate([zq[:, 1], wq[:, 1]], axis=-1)
    wq_pad = jnp.stack([q0, q1], axis=1).reshape(d, NSA_HEADS * LANES)
    wkv = w_in[:, NSA_WIDTH:NSA_WIDTH + 6 * HKV * DH]
    wg = w_in[:, NSA_WIDTH + 6 * HKV * DH:nsa_cols]
    wg = jnp.pad(wg, ((0, 0), (0, LANES - wg.shape[1])))
    wz = w_in[:, nsa_cols:nsa_cols + SSD_WIDTH]
    wx = w_in[:, nsa_cols + SSD_WIDTH:nsa_cols + SSD_WIDTH + SSD_CONV_DIM]
    wdt = w_in[:, nsa_cols + SSD_WIDTH + SSD_CONV_DIM:]
    wdt = jnp.pad(wdt, ((0, 0), (0, LANES - wdt.shape[1])))
    return jnp.concatenate([wq_pad, wkv, wg, wdt, wz, wx], axis=1).astype(BF16)


def _in_proj_kernel(x_ref, g_ref, w_ref, q_ref, kc_ref, vc_ref, ks_ref, vs_ref, kw_ref, vw_ref,
                    gate_ref, z_ref, xbc_ref, dt_ref):
    x = x_ref[...]
    ms = jnp.mean(x * x, axis=-1, keepdims=True)
    xn = (x * lax.rsqrt(ms + EPS) * g_ref[...]).astype(BF16)

    def seg(lo, hi):
        return jnp.dot(xn, w_ref[:, lo:hi], preferred_element_type=F32)

    q_ref[...] = (seg(_Q0, _Q1) * (DH ** -0.5 * LOG2E)).astype(BF16)
    ones = jnp.ones((x.shape[0], LANES - DH), F32)
    kv_refs = (kc_ref, vc_ref, ks_ref, vs_ref, kw_ref, vw_ref)
    for j in range(0, len(kv_refs), MXU_N // LANES):
        pair = seg(_KV0 + j * LANES, _KV0 + j * LANES + MXU_N)
        kv_refs[j][...] = pair[:, :LANES].astype(BF16)
        val = pair[:, LANES:]
        if j == 0:
            kv_refs[j + 1][...] = val.astype(BF16)
        else:
            for h in range(HKV):
                kv_refs[j + 1][h] = jnp.concatenate([val[:, h * DH:(h + 1) * DH], ones], axis=1).astype(BF16)
    gate_dt = seg(_G0, _Z0)
    gate_ref[...] = 1.0 / (1.0 + jnp.exp(-gate_dt[:, :LANES]))
    dt_ref[...] = gate_dt[:, LANES:]
    z_ref[...] = seg(_Z0, _X0)
    xbc_ref[...] = seg(_X0, _WCOLS)


def _in_proj(x2, attn_norm, w_pad, tm=512):
    t = x2.shape[0]
    row = lambda w: pl.BlockSpec((tm, w), lambda i: (i, 0))
    full = lambda a: pl.BlockSpec(a.shape, lambda i: (0,) * a.ndim)
    g = attn_norm.reshape(1, D_MODEL)
    kv = jax.ShapeDtypeStruct((t, LANES), BF16)
    val = jax.ShapeDtypeStruct((HKV, t, LANES), BF16)
    outs = ([jax.ShapeDtypeStruct((t, NSA_HEADS * LANES), BF16), kv, kv, kv, val, kv, val]
            + [jax.ShapeDtypeStruct((t, LANES), F32),
               jax.ShapeDtypeStruct((t, SSD_WIDTH), F32),
               jax.ShapeDtypeStruct((t, SSD_CONV_DIM), F32),
               jax.ShapeDtypeStruct((t, LANES), F32)])
    spec = lambda s: (row(s.shape[1]) if len(s.shape) == 2
                      else pl.BlockSpec((HKV, tm, LANES), lambda i: (0, i, 0)))
    return pl.pallas_call(
        _in_proj_kernel,
        out_shape=outs,
        grid=(t // tm,),
        in_specs=[row(D_MODEL), full(g), full(w_pad)],
        out_specs=[spec(s) for s in outs],
        compiler_params=_cparams("parallel"),
        name="in_proj",
    )(x2, g, w_pad)


def _compress_weights(cmp_pos, cmp_w1, cmp_w2):
    half = CMP_BLOCK // 2
    eye = jnp.eye(HKV, dtype=F32)
    w1 = cmp_w1.reshape(2, CMP_BLOCK, DH, DH)
    w1big = jnp.einsum('jlde,hk->jlhdke', w1, eye)
    w1lo = w1big[:, :half].reshape(2, half * HKV * DH, HKV * DH).astype(BF16)
    w1hi = w1big[:, half:].reshape(2, half * HKV * DH, HKV * DH).astype(BF16)
    pos = jnp.broadcast_to(cmp_pos[:, :, None, :], (2, CMP_BLOCK, HKV, DH))
    poslo = pos[:, :half].reshape(2, 1, half * HKV * DH)
    poshi = pos[:, half:].reshape(2, 1, half * HKV * DH)
    w2big = jnp.einsum('jde,hk->jhdke', cmp_w2, eye).reshape(2, HKV * DH, HKV * DH).astype(BF16)
    return w1lo, w1hi, poslo, poshi, w2big


def _compress_kernel(kr_ref, vr_ref, w1lo_ref, w1hi_ref, poslo_ref, poshi_ref, w2_ref, kc_ref, vc_ref):
    for j, (src, dst) in enumerate(((kr_ref, kc_ref), (vr_ref, vc_ref))):
        r = src[0].astype(F32)
        a = jnp.dot((r + poslo_ref[j]).astype(BF16), w1lo_ref[j], preferred_element_type=F32)
        b = jnp.dot((r + poshi_ref[j]).astype(BF16), w1hi_ref[j], preferred_element_type=F32)
        hid = a + pltpu.roll(b, b.shape[0] - 1, 0)
        out = jnp.dot(_silu(hid).astype(BF16), w2_ref[j], preferred_element_type=F32)
        if j == 0:
            dst[0] = out.astype(BF16)
        else:
            dst[0, 0] = out.astype(BF16)
            dst[0, 1] = pltpu.roll(out, DH, 1).astype(BF16)


def _compress(kr, vr, cw):
    b, ng, width = kr.shape
    w1lo, w1hi, poslo, poshi, w2big = cw
    full = lambda a: pl.BlockSpec(a.shape, lambda i: (0,) * a.ndim)
    bspec = pl.BlockSpec((1, ng, width), lambda i: (i, 0, 0))
    return pl.pallas_call(
        _compress_kernel,
        out_shape=[jax.ShapeDtypeStruct((b, ng, HKV * DH), BF16),
                   jax.ShapeDtypeStruct((b, HKV, ng, HKV * DH), BF16)],
        grid=(b,),
        in_specs=[bspec, bspec, full(w1lo), full(w1hi), full(poslo), full(poshi), full(w2big)],
        out_specs=[pl.BlockSpec((1, ng, HKV * DH), lambda i: (i, 0, 0)),
                   pl.BlockSpec((1, HKV, ng, HKV * DH), lambda i: (i, 0, 0, 0))],
        compiler_params=_cparams("parallel"),
        name="nsa_compress",
    )(kr, vr, w1lo, w1hi, poslo, poshi, w2big)


def _bucket_thresholds():
    d = np.arange(MAX_DISTANCE + 1)
    max_exact = N_BUCKETS // 2
    nf = np.maximum(d, max_exact).astype(np.float32)
    large = max_exact + (np.log(nf / np.float32(max_exact)) / np.float32(math.log(MAX_DISTANCE / max_exact))
                         * np.float32(N_BUCKETS - max_exact)).astype(np.int32)
    bucket = np.where(d < max_exact, d, np.minimum(large, N_BUCKETS - 1))
    assert np.all(np.diff(bucket) >= 0) and bucket[MAX_DISTANCE] == N_BUCKETS - 1
    return [int(np.argmax(bucket >= k)) for k in range(N_BUCKETS)]


def _bias_kernel(rb_ref, bc_ref, bw_ref, bs_ref, *, n_cmp):
    i = pl.program_id(0)
    thr = _bucket_thresholds()

    def table(dist, valid, hd, shift):
        v = jnp.full(dist.shape, rb_ref[0, hd], F32)
        for k in range(1, N_BUCKETS):
            v = jnp.where(dist >= thr[k], rb_ref[k, hd], v)
        return jnp.where(valid, (v - shift) * LOG2E, NEG)

    row = lax.broadcasted_iota(jnp.int32, (TQ, LANES), 0)
    col = lax.broadcasted_iota(jnp.int32, (TQ, LANES), 1)
    dist_c = i * TQ + row - (col * CMP_STRIDE + CMP_BLOCK - 1)
    valid_c = (dist_c >= 0) & (col < n_cmp)
    for hd in range(NSA_HEADS):
        bc_ref[hd // GQA, hd % GQA] = table(dist_c, valid_c, hd, 0.0)

    @pl.when(i == 0)
    def _():
        band = WINDOW + TQ
        qi_w = lax.broadcasted_iota(jnp.int32, (TQ, band), 0)
        dist_w = qi_w + WINDOW - lax.broadcasted_iota(jnp.int32, (TQ, band), 1)
        valid_w = (dist_w >= 0) & (dist_w < WINDOW)
        qi_s = lax.broadcasted_iota(jnp.int32, (TQ, 2 * TQ), 0)
        dist_s = qi_s + TQ - lax.broadcasted_iota(jnp.int32, (TQ, 2 * TQ), 1)
        for hd in range(NSA_HEADS):
            k, g = hd // GQA, hd % GQA
            bw_ref[k, g * TQ:(g + 1) * TQ, :] = table(dist_w, valid_w, hd, 0.0)
            bs_ref[k, g * TQ:(g + 1) * TQ, :] = table(dist_s, dist_s >= 0, hd, rb_ref[N_BUCKETS - 1, hd])


def _bias_tables(rel_bias, s):
    n_cmp = (s - CMP_BLOCK) // CMP_STRIDE + 1
    assert n_cmp < LANES and TQ >= MAX_DISTANCE
    band = WINDOW + TQ
    return pl.pallas_call(
        functools.partial(_bias_kernel, n_cmp=n_cmp),
        out_shape=[jax.ShapeDtypeStruct((HKV, GQA, s, LANES), F32),
                   jax.ShapeDtypeStruct((HKV, GQA * TQ, band), F32),
                   jax.ShapeDtypeStruct((HKV, GQA * TQ, 2 * TQ), F32)],
        grid=(s // TQ,),
        in_specs=[pl.BlockSpec(memory_space=pltpu.SMEM)],
        out_specs=[pl.BlockSpec((HKV, GQA, TQ, LANES), lambda i: (0, 0, i, 0)),
                   pl.BlockSpec((HKV, GQA * TQ, band), lambda i: (0, 0, 0)),
                   pl.BlockSpec((HKV, GQA * TQ, 2 * TQ), lambda i: (0, 0, 0))],
        compiler_params=_cparams("arbitrary"),
        name="nsa_bias_tables",
    )(rel_bias)


def _sel_tables(s):
    n_cmp = (s - CMP_BLOCK) // CMP_STRIDE + 1
    n_sel = s // SEL_BLOCK
    c_start = np.arange(LANES) * CMP_STRIDE
    s_start = np.arange(n_sel) * SEL_BLOCK
    ovl = ((c_start[None, :] < s_start[:, None] + SEL_BLOCK)
           & (c_start[None, :] + CMP_BLOCK > s_start[:, None])
           & (np.arange(LANES)[None, :] < n_cmp)).astype(np.float32)
    key_blk = np.arange(s) // SEL_BLOCK
    expand = np.where(key_blk[None, :] == np.arange(LANES)[:, None], NEG, 0.0).astype(np.float32)
    expand_wide = expand.reshape(LANES, s // (2 * TQ), 2 * TQ).transpose(1, 0, 2)
    expand = expand.reshape(LANES, s // TQ, TQ).transpose(1, 0, 2)
    return jnp.asarray(ovl, BF16), jnp.asarray(expand, BF16), jnp.asarray(expand_wide, BF16)


def _nsa_kernel(q_ref, kc_ref, vc_ref, ks_ref, vs_ref, kw_ref, vw_ref, gate_ref,
                bc_ref, bw_ref, bs_ref, ovl_ref, exp_ref, expw_ref, gn_ref, o_ref,
                qs_ref, s_ref, m_ref, acc_ref, uns_ref, oc_ref, os_ref, ob_ref):
    i = pl.program_id(1)
    rows = GQA * TQ
    n_sel = ovl_ref.shape[0]
    n_band = WINDOW // TQ + 1
    batch = range(NSA_NB)
    units = [(b, h) for b in batch for h in range(HKV)]
    nt_dims = (((1,), (1,)), ((), ()))

    for b in batch:
        for hd in range(NSA_HEADS):
            qs_ref[b, hd // GQA, (hd % GQA) * TQ:(hd % GQA + 1) * TQ, :] = q_ref[b, :, hd * LANES:(hd + 1) * LANES]

    def chunk_rows(c, n=1):
        return pl.ds(pl.multiple_of(c * TQ, TQ), n * TQ)

    def row_max_to_lanes():
        for u in units:
            m_ref[u] = jnp.broadcast_to(jnp.max(m_ref[u], axis=-1, keepdims=True), (rows, TQ))

    def normalise(acc):
        row_sum = pltpu.roll(acc, DH, 1)
        return acc / jnp.maximum(row_sum, 1e-30)

    for b, h in units:
        bias_c = bc_ref[h].reshape(rows, LANES)
        sc = lax.dot_general(qs_ref[b, h], kc_ref[b], nt_dims, preferred_element_type=F32) + bias_c
        mc = jnp.max(sc, axis=-1, keepdims=True)
        pc = jnp.where(bias_c > 0.5 * NEG, jnp.exp2(sc - mc), 0.0)
        pc = pc / jnp.maximum(jnp.sum(pc, axis=-1, keepdims=True), 1e-30)
        oc_ref[b, h] = jnp.dot(pc.astype(BF16), vc_ref[b, h], preferred_element_type=F32)

        psum = jnp.sum(pc.reshape(GQA, TQ, LANES), axis=0)
        p_hi = psum.astype(BF16)
        p_lo = (psum - p_hi.astype(F32)).astype(BF16)
        imp = (lax.dot_general(ovl_ref[...], p_hi, nt_dims, preferred_element_type=F32)
               + lax.dot_general(ovl_ref[...], p_lo, nt_dims, preferred_element_type=F32))
        blk = lax.broadcasted_iota(jnp.int32, (n_sel, TQ), 0)
        tok = lax.broadcasted_iota(jnp.int32, (n_sel, TQ), 1) + i * TQ
        blk_of_t = tok // SEL_BLOCK
        forced = (blk == 0) | (blk == blk_of_t) | (blk == blk_of_t - 1)
        score = jnp.where(forced, FORCED_SCORE, jnp.where(blk <= blk_of_t, imp, -1.0))
        rank = jnp.zeros((n_sel, TQ), F32)
        for mm in range(n_sel):
            sm = score[mm:mm + 1, :]
            ahead = (sm > score) | ((sm == score) & (blk > mm))
            rank = rank + jnp.where(ahead, 1.0, 0.0)
        unsel_t = jnp.where(rank < min(SEL_TOP, n_sel), 0.0, 1.0)
        unsel_t = jnp.concatenate([unsel_t, jnp.zeros((LANES - n_sel, TQ), F32)], axis=0).astype(BF16)
        eye = (lax.broadcasted_iota(jnp.int32, (TQ, TQ), 0)
               == lax.broadcasted_iota(jnp.int32, (TQ, TQ), 1)).astype(BF16)
        uns_ref[b, h] = lax.dot_general(eye, unsel_t, nt_dims, preferred_element_type=F32).astype(BF16)

    def sel_scores(c, bias_cols):
        for b in batch:
            k = ks_ref[b, chunk_rows(c), :]
            for h in range(HKV):
                madd = jnp.dot(uns_ref[b, h], exp_ref[c], preferred_element_type=F32)
                s = lax.dot_general(qs_ref[b, h], k, nt_dims, preferred_element_type=F32)
                s = (s.reshape(GQA, TQ, TQ) + madd[None]).reshape(rows, TQ)
                if bias_cols is not None:
                    s = s + bs_ref[h, :, bias_cols:bias_cols + TQ]
                s_ref[b, h, c] = s
                m_ref[b, h] = jnp.maximum(m_ref[b, h], s)

    def sel_scores_wide(j):
        for b in batch:
            k = ks_ref[b, chunk_rows(2 * j, 2), :]
            for h in range(HKV):
                madd = jnp.dot(uns_ref[b, h], expw_ref[j], preferred_element_type=F32)
                s = lax.dot_general(qs_ref[b, h], k, nt_dims, preferred_element_type=F32)
                s = (s.reshape(GQA, TQ, 2 * TQ) + madd[None]).reshape(rows, 2 * TQ)
                s_ref[b, h, 2 * j] = s[:, :TQ]
                s_ref[b, h, 2 * j + 1] = s[:, TQ:]
                m_ref[b, h] = jnp.maximum(m_ref[b, h], jnp.maximum(s[:, :TQ], s[:, TQ:]))

    def softmax_pv(c):
        for b, h in units:
            p = jnp.exp2(s_ref[b, h, c] - m_ref[b, h])
            acc_ref[b, h] += jnp.dot(p.astype(BF16), vs_ref[h, b, chunk_rows(c), :], preferred_element_type=F32)

    def softmax_pv_wide(j):
        for b, h in units:
            m = m_ref[b, h]
            p = jnp.concatenate([jnp.exp2(s_ref[b, h, 2 * j] - m), jnp.exp2(s_ref[b, h, 2 * j + 1] - m)], axis=1)
            acc_ref[b, h] += jnp.dot(p.astype(BF16), vs_ref[h, b, chunk_rows(2 * j, 2), :],
                                     preferred_element_type=F32)

    def loop(n, body):
        lax.fori_loop(0, n, lambda j, carry: (body(j), carry)[1], 0)

    m_ref[...] = jnp.full(m_ref.shape, NEG, F32)
    n_far = jnp.maximum(i - 1, 0)
    loop(n_far // 2, sel_scores_wide)
    pl.when(n_far % 2 == 1)(lambda: sel_scores(n_far - 1, None))
    pl.when(i >= 1)(lambda: sel_scores(i - 1, 0))
    sel_scores(i, TQ)
    row_max_to_lanes()
    acc_ref[...] = jnp.zeros(acc_ref.shape, F32)
    loop((i + 1) // 2, softmax_pv_wide)
    pl.when((i + 1) % 2 == 1)(lambda: softmax_pv(i))
    for u in units:
        os_ref[u] = normalise(acc_ref[u])

    band = WINDOW + TQ
    col = lax.broadcasted_iota(jnp.int32, (1, band), 1)
    before_start = jnp.where(col + i * TQ >= WINDOW, 0.0, NEG)
    o_win = {}
    for b, h in units:
        k = kw_ref[b, chunk_rows(i, n_band), :]
        s = lax.dot_general(qs_ref[b, h], k, nt_dims, preferred_element_type=F32) + bw_ref[h] + before_start
        p = jnp.exp2(s - jnp.max(s, axis=-1, keepdims=True))
        o_win[b, h] = normalise(jnp.dot(p.astype(BF16), vw_ref[h, b, chunk_rows(i, n_band), :],
                                        preferred_element_type=F32))

    for b in batch:
        gates = gate_ref[b]
        for hd in range(NSA_HEADS):
            h, g = hd // GQA, hd % GQA
            r = slice(g * TQ, (g + 1) * TQ)
            o = (gates[:, hd:hd + 1] * oc_ref[b, h, r, :]
                 + gates[:, NSA_HEADS + hd:NSA_HEADS + hd + 1] * os_ref[b, h, r, :]
                 + gates[:, 2 * NSA_HEADS + hd:2 * NSA_HEADS + hd + 1] * o_win[b, h][r])
            ob_ref[b, :, hd * DH:(hd + 1) * DH] = o[:, :DH]
        o = ob_ref[b]
        ms = jnp.mean(o * o, axis=-1, keepdims=True)
        o_ref[b] = (o * lax.rsqrt(ms + EPS) * gn_ref[...]).astype(BF16)


def _nsa(q, kc, vc, ks, vs, kw, vw, gates, tables, attn_out_norm):
    b, s, _ = q.shape
    bias_c, bias_w, bias_s, ovl, expand, expand_wide = tables
    gn = attn_out_norm.reshape(1, NSA_WIDTH)
    nb = NSA_NB
    assert b % nb == 0 and (s // TQ) % 2 == 0
    full = lambda a: pl.BlockSpec(a.shape, lambda bi, i: (0,) * a.ndim)
    tile = lambda w: pl.BlockSpec((nb, TQ, w), lambda bi, i: (bi, i, 0))
    kseq = lambda a: pl.BlockSpec((nb,) + a.shape[1:], lambda bi, i: (bi, 0, 0))
    vseq = lambda a: pl.BlockSpec((HKV, nb) + a.shape[2:], lambda bi, i: (0, bi, 0, 0))
    assert kw.shape[1] == s + WINDOW and vw.shape[2] == s + WINDOW
    rows = GQA * TQ
    unit = (nb, HKV)
    return pl.pallas_call(
        _nsa_kernel,
        out_shape=jax.ShapeDtypeStruct((b, s, NSA_WIDTH), BF16),
        grid=(b // nb, s // TQ),
        in_specs=[tile(NSA_HEADS * LANES),
                  pl.BlockSpec((nb,) + kc.shape[1:], lambda bi, i: (bi, 0, 0)),
                  pl.BlockSpec((nb,) + vc.shape[1:], lambda bi, i: (bi, 0, 0, 0)),
                  kseq(ks), vseq(vs), kseq(kw), vseq(vw), tile(LANES),
                  pl.BlockSpec((HKV, GQA, TQ, LANES), lambda bi, i: (0, 0, i, 0)),
                  full(bias_w), full(bias_s), full(ovl), full(expand), full(expand_wide), full(gn)],
        out_specs=tile(NSA_WIDTH),
        scratch_shapes=[pltpu.VMEM(unit + (rows, LANES), BF16),
                        pltpu.VMEM(unit + (s // TQ, rows, TQ), F32),
                        pltpu.VMEM(unit + (rows, TQ), F32),
                        pltpu.VMEM(unit + (rows, LANES), F32),
                        pltpu.VMEM(unit + (TQ, LANES), BF16),
                        pltpu.VMEM(unit + (rows, LANES), F32),
                        pltpu.VMEM(unit + (rows, LANES), F32),
                        pltpu.VMEM((nb, TQ, NSA_WIDTH), F32)],
        compiler_params=_cparams("parallel", "arbitrary"),
        name="nsa_attention",
    )(q, kc, vc, ks, vs, kw, vw, gates, bias_c, bias_w, bias_s, ovl, expand, expand_wide, gn)


def _ssd_kernel(xbc_ref, z_ref, dt_ref, cw_ref, cb_ref, dtb_ref, alog_ref, dsk_ref, ng_ref, wsrc_ref, o_ref, wdst_ref,
                xbuf_ref, state_ref, y_ref):
    c = pl.program_id(1)
    wdst_ref[...] = wsrc_ref[...].astype(BF16)
    L, P, N = SSD_L, SSD_P, SSD_N
    hpg = SSD_HEADS // SSD_GROUPS
    pad = 8

    @pl.when(c == 0)
    def _():
        xbuf_ref[:, 0:pad, :] = jnp.zeros((SSD_NB, pad, SSD_CONV_DIM), F32)
        state_ref[...] = jnp.zeros(state_ref.shape, F32)

    ri = lax.broadcasted_iota(jnp.int32, (L, L), 0)
    ci = lax.broadcasted_iota(jnp.int32, (L, L), 1)
    causal = ri >= ci
    nt_dims = (((1,), (1,)), ((), ()))
    a_neg = -jnp.exp(alog_ref[...])

    for bb in range(SSD_NB):
        xbuf_ref[bb, pad:pad + L, :] = xbc_ref[bb]
        conv = cb_ref[...]
        for k in range(SSD_CONV):
            shift = SSD_CONV - 1 - k
            conv = conv + xbuf_ref[bb, pad - shift:pad - shift + L, :] * cw_ref[k:k + 1, :]
        xbuf_ref[bb, 0:pad, :] = xbuf_ref[bb, L:L + pad, :]
        xa = _silu(conv)
        xs = xa[:, :SSD_WIDTH]
        bm = xa[:, SSD_WIDTH:SSD_WIDTH + SSD_GROUPS * N]
        cm = xa[:, SSD_WIDTH + SSD_GROUPS * N:]

        dtv = dt_ref[bb] + dtb_ref[...]
        dt = jnp.maximum(dtv, 0.0) + jnp.log1p(jnp.exp(-jnp.abs(dtv)))
        cs = jnp.dot(causal.astype(F32), dt * a_neg, preferred_element_type=F32,
                     precision=lax.Precision.HIGHEST)
        cs_t = cs.T

        for gr in range(SSD_GROUPS):
            b_g = bm[:, gr * N:(gr + 1) * N]
            c_g = cm[:, gr * N:(gr + 1) * N]
            scores = lax.dot_general(c_g.astype(BF16), b_g.astype(BF16), nt_dims, preferred_element_type=F32)
            b_gt = b_g.T
            for hh in range(hpg):
                h = gr * hpg + hh
                cs_col = cs[:, h:h + 1]
                cs_row = cs_t[h:h + 1, :]
                cs_last = cs[L - 1:L, h:h + 1]
                decay = jnp.exp(jnp.where(causal, cs_col - cs_row, NEG))
                xs_h = xs[:, h * P:(h + 1) * P]
                xc = (xs_h * dt[:, h:h + 1]).astype(BF16)
                y = jnp.dot((scores * decay).astype(BF16), xc, preferred_element_type=F32)
                prev = state_ref[bb, h]
                y = y + jnp.dot((c_g * jnp.exp(cs_col)).astype(BF16), prev.astype(BF16),
                                preferred_element_type=F32)
                contrib = jnp.dot((b_gt * jnp.exp(cs_last - cs_row)).astype(BF16), xc,
                                  preferred_element_type=F32)
                state_ref[bb, h] = jnp.exp(cs_last) * prev + contrib
                y_ref[bb, :, h * P:(h + 1) * P] = y + xs_h * dsk_ref[:, h * P:(h + 1) * P]

        y = y_ref[bb] * _silu(z_ref[bb])
        gw = SSD_WIDTH // SSD_GROUPS
        for gr in range(SSD_GROUPS):
            yg = y[:, gr * gw:(gr + 1) * gw]
            ms = jnp.mean(yg * yg, axis=-1, keepdims=True)
            o_ref[bb, :, gr * gw:(gr + 1) * gw] = (yg * lax.rsqrt(ms + EPS)
                                                   * ng_ref[:, gr * gw:(gr + 1) * gw]).astype(BF16)


def _ssd(xbc, z, dt, conv_w, conv_b, dt_bias, a_log, d_skip, norm_g, w_cast):
    b, s, _ = xbc.shape
    nb = SSD_NB
    assert b % nb == 0
    n_steps = (b // nb) * (s // SSD_L)
    wr, wc = w_cast.shape
    assert wr % n_steps == 0
    wblk = pl.BlockSpec((wr // n_steps, wc), lambda bi, c: (bi * (s // SSD_L) + c, 0))
    padl = lambda v: jnp.pad(v, (0, LANES - v.shape[0])).reshape(1, LANES)
    args = (conv_w, conv_b.reshape(1, SSD_CONV_DIM), padl(dt_bias), padl(a_log),
            jnp.repeat(d_skip, SSD_P).reshape(1, SSD_WIDTH), norm_g.reshape(1, SSD_WIDTH))
    full = lambda a: pl.BlockSpec(a.shape, lambda bi, c: (0,) * a.ndim)
    blk = lambda w: pl.BlockSpec((nb, SSD_L, w), lambda bi, c: (bi, c, 0))
    return pl.pallas_call(
        _ssd_kernel,
        out_shape=[jax.ShapeDtypeStruct((b, s, SSD_WIDTH), BF16), jax.ShapeDtypeStruct((wr, wc), BF16)],
        grid=(b // nb, s // SSD_L),
        in_specs=[blk(SSD_CONV_DIM), blk(SSD_WIDTH), blk(LANES)] + [full(a) for a in args] + [wblk],
        out_specs=[blk(SSD_WIDTH), wblk],
        scratch_shapes=[pltpu.VMEM((nb, SSD_L + 8, SSD_CONV_DIM), F32),
                        pltpu.VMEM((nb, SSD_HEADS, SSD_N, SSD_P), F32),
                        pltpu.VMEM((nb, SSD_L, SSD_WIDTH), F32)],
        compiler_params=_cparams("parallel", "arbitrary"),
        name="ssd_mixer",
    )(xbc, z, dt, *args, w_cast)


def _out_proj_kernel(x_ref, on_ref, os_ref, wo_ref, fg_ref, rwt_ref, rbt_ref, wsrc_ref,
                     h_ref, hn_ref, comb_ref, slot_ref, wdst_ref):
    wdst_ref[...] = wsrc_ref[...].astype(BF16)
    h = (x_ref[...]
         + jnp.dot(on_ref[...], wo_ref[0:NSA_WIDTH, :], preferred_element_type=F32)
         + jnp.dot(os_ref[...], wo_ref[NSA_WIDTH:, :], preferred_element_type=F32))
    tm = h.shape[0]
    for j in range(H_SLAB):
        h_ref[pl.ds(j, tm, stride=H_SLAB), :] = h[:, j * LANES:(j + 1) * LANES]
    ms = jnp.mean(h * h, axis=-1, keepdims=True)
    hn = h * lax.rsqrt(ms + EPS) * fg_ref[...]
    for j in range(H_SLAB):
        hn_ref[pl.ds(j, tm, stride=H_SLAB), :] = hn[:, j * LANES:(j + 1) * LANES]

    nt_dims = (((1,), (1,)), ((), ()))
    hn_hi = hn.astype(BF16)
    hn_lo = (hn - hn_hi.astype(F32)).astype(BF16)
    rwt = rwt_ref[...]
    rwt_hi = rwt.astype(BF16)
    rwt_lo = (rwt - rwt_hi.astype(F32)).astype(BF16)
    both = lax.dot_general(jnp.concatenate([rwt_hi, rwt_lo], axis=0), hn_hi, nt_dims, preferred_element_type=F32)
    logits = (both[:N_EXPERTS] + both[N_EXPERTS:]
              + lax.dot_general(rwt_hi, hn_lo, nt_dims, preferred_element_type=F32) + rbt_ref[...])
    row = lax.broadcasted_iota(jnp.int32, logits.shape, 0)
    work = logits
    picks = []
    for _ in range(TOP_K):
        v = jnp.max(work, axis=0, keepdims=True)
        idx = jnp.min(jnp.where(work == v, row, N_EXPERTS), axis=0, keepdims=True)
        hit = row == idx
        picks.append((v, hit))
        work = jnp.where(hit, -3e38, work)
    v0 = picks[0][0]
    es = [jnp.exp(v - v0) for v, _ in picks]
    den = es[0] + es[1] + es[2] + es[3]
    comb = jnp.zeros_like(logits)
    slot = jnp.zeros(logits.shape, jnp.int32)
    for k, (e, (_, hit)) in enumerate(zip(es, picks)):
        comb = comb + jnp.where(hit, e / den, 0.0)
        slot = jnp.where(hit, k + 1, slot)
    comb_ref[...] = comb
    slot_ref[...] = slot


def _out_proj(x2, o_nsa, o_ssd, w_out, ffn_norm, router_w, router_b, w_cast, tm=512):
    t = x2.shape[0]
    wr, wc = w_cast.shape
    assert wr % (t // tm) == 0
    wblk = pl.BlockSpec((wr // (t // tm), wc), lambda i: (i, 0))
    row = lambda w: pl.BlockSpec((tm, w), lambda i: (i, 0))
    full = lambda a: pl.BlockSpec(a.shape, lambda i: (0,) * a.ndim)
    wo = w_out.astype(BF16)
    fg = ffn_norm.reshape(1, D_MODEL)
    rwt = router_w.T
    rbt = router_b.reshape(N_EXPERTS, 1)
    tok = pl.BlockSpec((N_EXPERTS, tm), lambda i: (0, i))
    return pl.pallas_call(
        _out_proj_kernel,
        out_shape=[jax.ShapeDtypeStruct((t * H_SLAB, LANES), F32),
                   jax.ShapeDtypeStruct((t * H_SLAB, LANES), F32),
                   jax.ShapeDtypeStruct((N_EXPERTS, t), F32),
                   jax.ShapeDtypeStruct((N_EXPERTS, t), jnp.int32),
                   jax.ShapeDtypeStruct((wr, wc), BF16)],
        grid=(t // tm,),
        in_specs=[row(D_MODEL), row(NSA_WIDTH), row(SSD_WIDTH), full(wo), full(fg), full(rwt), full(rbt), wblk],
        out_specs=[pl.BlockSpec((tm * H_SLAB, LANES), lambda i: (i, 0)),
                   pl.BlockSpec((tm * H_SLAB, LANES), lambda i: (i, 0)), tok, tok, wblk],
        compiler_params=_cparams("parallel"),
        name="out_proj_router",
    )(x2, o_nsa, o_ssd, wo, fg, rwt, rbt, w_cast)


MOE_CHUNK = 4096
MOE_TILE = 192
MOE_GROUP = 256
ROUTE_TILE = 256


def _route_kernel(comb_ref, slot_ref, dest_ref, wrow_ref, starts_ref, pos_ref):
    ne, tc = comb_ref.shape
    ri = lax.broadcasted_iota(jnp.int32, (ROUTE_TILE, ROUTE_TILE), 0)
    ci = lax.broadcasted_iota(jnp.int32, (ROUTE_TILE, ROUTE_TILE), 1)
    earlier = (ri < ci).astype(BF16)
    carry = jnp.zeros((ne, 1), F32)
    for j in range(tc // ROUTE_TILE):
        cols = slice(j * ROUTE_TILE, (j + 1) * ROUTE_TILE)
        sel = jnp.where(slot_ref[:, cols] > 0, 1.0, 0.0)
        pos_ref[:, cols] = jnp.dot(sel.astype(BF16), earlier, preferred_element_type=F32) + carry
        carry = carry + jnp.sum(sel, axis=1, keepdims=True)
    ei = lax.broadcasted_iota(jnp.int32, (ne, LANES), 0)
    li = lax.broadcasted_iota(jnp.int32, (ne, LANES), 1)
    starts_row = jnp.sum(jnp.where(ei < li, carry, 0.0), axis=0, keepdims=True)
    starts_col = jnp.sum(jnp.where(ei == li, starts_row, 0.0), axis=1, keepdims=True)
    starts_ref[0] = jnp.broadcast_to(starts_row, (SUBLANES, LANES)).astype(jnp.int32)
    dest = pos_ref[...] + starts_col
    slot = slot_ref[...]
    comb = comb_ref[...]
    for k in range(TOP_K):
        hit = slot == k + 1
        dest_ref[0, :, k * tc:(k + 1) * tc] = jnp.sum(jnp.where(hit, dest, 0.0), axis=0,
                                                      keepdims=True).astype(jnp.int32)
        wrow_ref[0, :, k * tc:(k + 1) * tc] = jnp.sum(jnp.where(hit, comb, 0.0), axis=0, keepdims=True)


def _route(comb, slot, tc):
    ne, t = comb.shape
    nch = t // tc
    blk = pl.BlockSpec((ne, tc), lambda c: (0, c))
    return pl.pallas_call(
        _route_kernel,
        out_shape=[jax.ShapeDtypeStruct((nch, 1, TOP_K * tc), jnp.int32),
                   jax.ShapeDtypeStruct((nch, 1, TOP_K * tc), F32),
                   jax.ShapeDtypeStruct((nch, SUBLANES, LANES), jnp.int32)],
        grid=(nch,),
        in_specs=[blk, blk],
        out_specs=[pl.BlockSpec((1, 1, TOP_K * tc), lambda c: (c, 0, 0)),
                   pl.BlockSpec((1, 1, TOP_K * tc), lambda c: (c, 0, 0)),
                   pl.BlockSpec((1, SUBLANES, LANES), lambda c: (c, 0, 0))],
        scratch_shapes=[pltpu.VMEM((ne, tc), F32)],
        compiler_params=_cparams("parallel"),
        name="moe_route",
    )(comb, slot)


def _moe_kernel(starts_ref, dest_hbm, wrow_hbm, hn_hbm, h_hbm, wgu_ref, bgu_ref, wd_ref, bd_ref, fn_ref, o_hbm,
                x_ref, acc_ref, xs0_ref, xs1_ref, y0_ref, y1_ref, ob_ref, dest_s, wrow_s, tok_s, wt_s, st_s,
                sem, osem):
    c = pl.program_id(0)
    e = pl.program_id(1)
    tc = x_ref.shape[0] // H_SLAB
    n_rows = TOP_K * tc

    def slab(i, n, width):
        return pl.ds(pl.multiple_of(i * width, width), n * width)

    half_len = n_rows + MOE_TILE
    cur = (c % 2) * half_len
    nxt = half_len - cur

    def invert(j0, u, half):
        r = half + dest_s[j0 + u]
        tok_s[r] = (j0 & (tc - 1)) + u
        wt_s[r] = wrow_s[j0 + u]

    def gather(r0, xs_ref):
        i0 = cur + r0
        for r in range(MOE_TILE):
            xs_ref[slab(r, 1, H_SLAB), :] = x_ref[slab(tok_s[i0 + r], 1, H_SLAB), :]

    def ffn(xs_ref, y_ref):
        x = jnp.concatenate([xs_ref[pl.ds(j, MOE_TILE, stride=H_SLAB), :].astype(BF16) for j in range(H_SLAB)],
                            axis=1)
        gu = jnp.dot(x, wgu_ref[0], preferred_element_type=F32) + bgu_ref[0]
        gate = jnp.minimum(gu[:, :D_FF], SWIGLU_LIMIT)
        up = jnp.clip(gu[:, D_FF:], -SWIGLU_LIMIT, SWIGLU_LIMIT)
        act = (up + 1.0) * gate / (1.0 + jnp.exp(-SWIGLU_ALPHA * gate))
        y = jnp.dot(act.astype(BF16), wd_ref[0], preferred_element_type=F32) + bd_ref[0]
        for j in range(H_SLAB):
            y_ref[pl.ds(j, MOE_TILE, stride=H_SLAB), :] = y[:, j * LANES:(j + 1) * LANES]

    def scatter(r0, n_valid, y_ref):
        i0 = cur + r0
        for g in range(MOE_TILE // SUBLANES):
            rows = []
            for u in range(SUBLANES):
                r = g * SUBLANES + u
                tok = jnp.where(r < n_valid, tok_s[i0 + r], tc)
                rows.append((tok, acc_ref[slab(tok, 1, H_SLAB), :] + wt_s[i0 + r] * y_ref[slab(r, 1, H_SLAB), :]))
            for tok, val in rows:
                acc_ref[slab(tok, 1, H_SLAB), :] = val

    chunk_copies = lambda: (pltpu.make_async_copy(hn_hbm.at[slab(c * tc, tc, H_SLAB), :], x_ref, sem.at[0]),
                            pltpu.make_async_copy(h_hbm.at[slab(c * tc, tc, H_SLAB), :],
                                                  acc_ref.at[pl.ds(0, tc * H_SLAB), :], sem.at[1]))
    map_copies = lambda cc: (pltpu.make_async_copy(dest_hbm.at[cc, 0], dest_s, sem.at[2]),
                             pltpu.make_async_copy(wrow_hbm.at[cc, 0], wrow_s, sem.at[3]))

    @pl.when(e == 0)
    def _():
        for cp in chunk_copies():
            cp.start()

        @pl.when(c == 0)
        def _():
            for cp in map_copies(0):
                cp.start()
            for cp in map_copies(0):
                cp.wait()

            def invert_slice(g, carry):
                for u in range(MOE_GROUP):
                    invert(g * MOE_GROUP, u, cur)
                return carry

            lax.fori_loop(0, n_rows // MOE_GROUP, invert_slice, 0)
            for half in (0, half_len):
                for u in range(MOE_TILE):
                    tok_s[half + n_rows + u] = 0
                    wt_s[half + n_rows + u] = 0.0

        @pl.when(c + 1 < pl.num_programs(0))
        def _():
            for cp in map_copies(c + 1):
                cp.start()
            for cp in map_copies(c + 1):
                cp.wait()

        for cp in chunk_copies():
            cp.wait()
        acc_ref[pl.ds(tc * H_SLAB, H_SLAB), :] = jnp.zeros((H_SLAB, LANES), F32)
        st_s[0] = 0
        st_s[1] = 0
        st_s[2] = 0
        for y_ref in (y0_ref, y1_ref):
            y_ref[...] = jnp.zeros(y_ref.shape, F32)
        gather(0, xs0_ref)

    base = starts_ref[c * LANES + e]
    n_e = starts_ref[c * LANES + e + 1] - base
    xs_bufs = (xs0_ref, xs1_ref)
    y_bufs = (y0_ref, y1_ref)

    def tile_step(p, k, r0_next, prev_r0, prev_nv):
        first = jnp.minimum(k, n_rows // MOE_GROUP - 1) * MOE_GROUP
        for u in range(MOE_GROUP):
            invert(first, u, nxt)
        gather(r0_next, xs_bufs[1 - p])
        scatter(prev_r0, prev_nv, y_bufs[1 - p])
        ffn(xs_bufs[p], y_bufs[p])

    def tile(j, carry):
        r0 = base + j * MOE_TILE
        r0_next = jnp.minimum(r0 + MOE_TILE, base + n_e)
        k = st_s[0]
        prev_r0 = st_s[1]
        prev_nv = st_s[2]
        for p in range(2):
            pl.when(k % 2 == p)(functools.partial(tile_step, p, k, r0_next, prev_r0, prev_nv))
        st_s[0] = k + 1
        st_s[1] = r0
        st_s[2] = jnp.minimum(n_e - j * MOE_TILE, MOE_TILE)
        return carry

    lax.fori_loop(0, (n_e + MOE_TILE - 1) // MOE_TILE, tile, 0)

    @pl.when(e == pl.num_programs(1) - 1)
    def _():
        for p in range(2):
            pl.when(st_s[0] % 2 == p)(functools.partial(scatter, st_s[1], st_s[2], y_bufs[1 - p]))
        n_groups = tc // MOE_GROUP
        out_copy = lambda g, buf: pltpu.make_async_copy(
            ob_ref.at[buf], o_hbm.at[pl.ds(pl.multiple_of(c * tc + g * MOE_GROUP, MOE_GROUP), MOE_GROUP), :],
            osem.at[buf])

        def norm(g, carry):
            buf = g % 2

            @pl.when(g >= 2)
            def _():
                out_copy(g - 2, buf).wait()

            first = pl.multiple_of(g * MOE_GROUP * H_SLAB, MOE_GROUP * H_SLAB)
            hs = [acc_ref[pl.ds(first + j, MOE_GROUP, stride=H_SLAB), :] for j in range(H_SLAB)]
            ss = hs[0] * hs[0]
            for hj in hs[1:]:
                ss = ss + hj * hj
            inv = lax.rsqrt(jnp.sum(ss, axis=-1, keepdims=True) / D_MODEL + EPS)
            for j, hj in enumerate(hs):
                ob_ref[buf, :, j * LANES:(j + 1) * LANES] = hj * inv * fn_ref[:, j * LANES:(j + 1) * LANES]
            out_copy(g, buf).start()
            return carry

        lax.fori_loop(0, n_groups, norm, 0)
        for g in range(max(n_groups - 2, 0), n_groups):
            out_copy(g, g % 2).wait()


def _moe(hnp, comb, slot, h1, w_gate_up, b_gate_up, w_down, b_down, final_norm, tc=MOE_CHUNK):
    t = comb.shape[1]
    tc = min(tc, t)
    assert tc & (tc - 1) == 0 and tc % MOE_GROUP == 0 and MOE_TILE <= MOE_GROUP
    nch = t // tc
    dest, wrow, starts = _route(comb, slot, tc)
    starts = starts[:, 0, :].reshape(nch * LANES)
    wgu = w_gate_up.astype(BF16)
    wd = w_down.astype(BF16)
    bgu = b_gate_up.reshape(N_EXPERTS, 1, 2 * D_FF)
    bd = b_down.reshape(N_EXPERTS, 1, D_MODEL)
    fn = final_norm.reshape(1, D_MODEL)
    anyspace = pl.BlockSpec(memory_space=pl.ANY)
    exp = lambda a: pl.BlockSpec((1,) + a.shape[1:], lambda c, e, st: (e, 0, 0))
    return pl.pallas_call(
        _moe_kernel,
        out_shape=jax.ShapeDtypeStruct((t, D_MODEL), F32),
        grid_spec=pltpu.PrefetchScalarGridSpec(
            num_scalar_prefetch=1,
            grid=(nch, N_EXPERTS),
            in_specs=[anyspace, anyspace, anyspace, anyspace, exp(wgu), exp(bgu), exp(wd), exp(bd),
                      pl.BlockSpec(fn.shape, lambda c, e, st: (0, 0))],
            out_specs=anyspace,
            scratch_shapes=[pltpu.VMEM((tc * H_SLAB, LANES), F32),
                            pltpu.VMEM(((tc + 1) * H_SLAB, LANES), F32),
                            pltpu.VMEM((MOE_TILE * H_SLAB, LANES), F32),
                            pltpu.VMEM((MOE_TILE * H_SLAB, LANES), F32),
                            pltpu.VMEM((MOE_TILE * H_SLAB, LANES), F32),
                            pltpu.VMEM((MOE_TILE * H_SLAB, LANES), F32),
                            pltpu.VMEM((2, MOE_GROUP, D_MODEL), F32),
                            pltpu.SMEM((TOP_K * tc,), jnp.int32),
                            pltpu.SMEM((TOP_K * tc,), F32),
                            pltpu.SMEM((2 * (TOP_K * tc + MOE_TILE),), jnp.int32),
                            pltpu.SMEM((2 * (TOP_K * tc + MOE_TILE),), F32),
                            pltpu.SMEM((4,), jnp.int32),
                            pltpu.SemaphoreType.DMA((4,)),
                            pltpu.SemaphoreType.DMA((2,))]),
        compiler_params=_cparams("arbitrary", "arbitrary", vmem=MOE_VMEM_LIMIT),
        name="moe_experts",
    )(starts, dest, wrow, hnp, h1, wgu, bgu, wd, bd, fn)


def kernel(x, attn_norm, w_in, rel_bias, cmp_pos, cmp_w1, cmp_w2, attn_out_norm, conv_w, conv_b,
           dt_bias, a_log, d_skip, ssm_out_norm, w_out, ffn_norm, router_w, router_b,
           w_gate_up, b_gate_up, w_down, b_down, final_norm):
    b, s, d = x.shape
    t = b * s
    depth = w_in.shape[0]
    tables = tuple(_bias_tables(rel_bias, s)) + _sel_tables(s)
    h = x.reshape(t, d)
    for l in range(depth):
        (q, kc_raw, vc_raw, ks, vs, kw, vw, gates, z, xbc, dt) = _in_proj(h, attn_norm[l], _pad_w_in(w_in[l]))
        grp = CMP_STRIDE * HKV * DH
        kc, vc = _compress(kc_raw.reshape(b, s // CMP_STRIDE, grp), vc_raw.reshape(b, s // CMP_STRIDE, grp),
                           _compress_weights(cmp_pos[l], cmp_w1[l], cmp_w2[l]))
        seq = lambda a: a.reshape(a.shape[:-2] + (b, s, a.shape[-1]))
        front = lambda a: jnp.pad(a, [(0, 0)] * (a.ndim - 2) + [(WINDOW, 0), (0, 0)])
        o_nsa = _nsa(seq(q), kc, vc, seq(ks), seq(vs), front(seq(kw)), front(seq(vw)), seq(gates), tables,
                     attn_out_norm[l])
        o_ssd, wgu = _ssd(seq(xbc), seq(z), seq(dt), conv_w[l], conv_b[l], dt_bias[l], a_log[l], d_skip[l],
                          ssm_out_norm[l], w_gate_up[l].reshape(N_EXPERTS * D_MODEL, 2 * D_FF))
        h1, hnp, comb, slot, wd = _out_proj(h, o_nsa.reshape(t, NSA_WIDTH), o_ssd.reshape(t, SSD_WIDTH), w_out[l],
                                            ffn_norm[l], router_w[l], router_b[l],
                                            w_down[l].reshape(N_EXPERTS * D_FF, D_MODEL))
        assert depth == 1
        h = _moe(hnp, comb, slot, h1, wgu.reshape(N_EXPERTS, D_MODEL, 2 * D_FF), b_gate_up[l],
                 wd.reshape(N_EXPERTS, D_FF, D_MODEL), b_down[l], final_norm)
    return h.reshape(b, s, d)
```

```python
import functools
import math

import numpy as np
import jax
import jax.numpy as jnp
from jax import lax
from jax.experimental import pallas as pl
from jax.experimental.pallas import tpu as pltpu

F32 = jnp.float32
BF16 = jnp.bfloat16

D_MODEL = 1024
NSA_HEADS = 8
HKV = 2
GQA = NSA_HEADS // HKV
DH = 64
NSA_WIDTH = NSA_HEADS * DH
CMP_BLOCK = 32
CMP_STRIDE = 16
SEL_BLOCK = 64
SEL_TOP = 16
WINDOW = 512
TQ = 128
NSA_NB = 2
SSD_HEADS = 8
SSD_P = 64
SSD_WIDTH = SSD_HEADS * SSD_P
SSD_GROUPS = 2
SSD_N = 128
SSD_CONV = 4
SSD_L = 128
SSD_NB = 2
SSD_CONV_DIM = SSD_WIDTH + 2 * SSD_GROUPS * SSD_N
N_BUCKETS = 32
MAX_DISTANCE = 128
N_EXPERTS = 32
TOP_K = 4
D_FF = 1024
SWIGLU_LIMIT = 7.0
SWIGLU_ALPHA = 1.702

EPS = 1e-6
NEG = -1e30
FORCED_SCORE = 1e4
LOG2E = 1.4426950408889634
LANES = 128
SUBLANES = 8
H_SLAB = D_MODEL // LANES
VMEM_LIMIT = 56 * 1024 * 1024
MOE_VMEM_LIMIT = (2 * 4096 * D_MODEL * 4 + 2 * 3 * D_MODEL * D_FF * 2 + 12 * 1024 * 1024)


def _cparams(*sem, vmem=VMEM_LIMIT):
    return pltpu.CompilerParams(dimension_semantics=sem, vmem_limit_bytes=vmem)


def _silu(v):
    return v / (1.0 + jnp.exp(-v))


_Q0, _Q1 = 0, NSA_WIDTH
_KV0 = _Q1
_G0 = _KV0 + 6 * LANES
_DT0 = _G0 + LANES
_Z0 = _DT0 + LANES
_X0 = _Z0 + SSD_WIDTH
_WCOLS = _X0 + SSD_CONV_DIM
MXU_N = 256


def _pad_w_in(w_in):
    d = w_in.shape[0]
    nsa_cols = NSA_WIDTH + 6 * HKV * DH + 3 * NSA_HEADS
    wq_pad = w_in[:, :NSA_WIDTH]
    wkv = w_in[:, NSA_WIDTH:NSA_WIDTH + 6 * HKV * DH]
    wg = w_in[:, NSA_WIDTH + 6 * HKV * DH:nsa_cols]
    wg = jnp.pad(wg, ((0, 0), (0, LANES - wg.shape[1])))
    wz = w_in[:, nsa_cols:nsa_cols + SSD_WIDTH]
    wx = w_in[:, nsa_cols + SSD_WIDTH:nsa_cols + SSD_WIDTH + SSD_CONV_DIM]
    wdt = w_in[:, nsa_cols + SSD_WIDTH + SSD_CONV_DIM:]
    wdt = jnp.pad(wdt, ((0, 0), (0, LANES - wdt.shape[1])))
    return jnp.concatenate([wq_pad, wkv, wg, wdt, wz, wx], axis=1).astype(BF16)


def _in_proj_kernel(x_ref, g_ref, w_ref, q_ref, kc_ref, vc_ref, ks_ref, vs_ref, kw_ref, vw_ref,
                    gate_ref, z_ref, xbc_ref, dt_ref):
    x = x_ref[...]
    ms = jnp.mean(x * x, axis=-1, keepdims=True)
    xn = (x * lax.rsqrt(ms + EPS) * g_ref[...]).astype(BF16)

    def seg(lo, hi):
        return jnp.dot(xn, w_ref[:, lo:hi], preferred_element_type=F32)

    q_ref[...] = (seg(_Q0, _Q1) * (DH ** -0.5 * LOG2E)).astype(BF16)
    ones = jnp.ones((x.shape[0], LANES - DH), F32)
    kv_refs = (kc_ref, vc_ref, ks_ref, vs_ref, kw_ref, vw_ref)
    for j in range(0, len(kv_refs), MXU_N // LANES):
        pair = seg(_KV0 + j * LANES, _KV0 + j * LANES + MXU_N)
        kv_refs[j][...] = pair[:, :LANES].astype(BF16)
        val = pair[:, LANES:]
        if j == 0:
            kv_refs[j + 1][...] = val.astype(BF16)
        else:
            for h in range(HKV):
                kv_refs[j + 1][h] = jnp.concatenate([val[:, h * DH:(h + 1) * DH], ones], axis=1).astype(BF16)
    gate_dt = seg(_G0, _Z0)
    gate_ref[...] = 1.0 / (1.0 + jnp.exp(-gate_dt[:, :LANES]))
    dt_ref[...] = gate_dt[:, LANES:]
    z_ref[...] = seg(_Z0, _X0)
    xbc_ref[...] = seg(_X0, _WCOLS)


def _in_proj(x2, attn_norm, w_pad, tm=512):
    t = x2.shape[0]
    row = lambda w: pl.BlockSpec((tm, w), lambda i: (i, 0))
    full = lambda a: pl.BlockSpec(a.shape, lambda i: (0,) * a.ndim)
    g = attn_norm.reshape(1, D_MODEL)
    kv = jax.ShapeDtypeStruct((t, LANES), BF16)
    val = jax.ShapeDtypeStruct((HKV, t, LANES), BF16)
    outs = ([jax.ShapeDtypeStruct((t, NSA_WIDTH), BF16), kv, kv, kv, val, kv, val]
            + [jax.ShapeDtypeStruct((t, LANES), F32),
               jax.ShapeDtypeStruct((t, SSD_WIDTH), F32),
               jax.ShapeDtypeStruct((t, SSD_CONV_DIM), F32),
               jax.ShapeDtypeStruct((t, LANES), F32)])
    spec = lambda s: (row(s.shape[1]) if len(s.shape) == 2
                      else pl.BlockSpec((HKV, tm, LANES), lambda i: (0, i, 0)))
    return pl.pallas_call(
        _in_proj_kernel,
        out_shape=outs,
        grid=(t // tm,),
        in_specs=[row(D_MODEL), full(g), full(w_pad)],
        out_specs=[spec(s) for s in outs],
        compiler_params=_cparams("parallel"),
        name="in_proj",
    )(x2, g, w_pad)


def _compress_weights(cmp_pos, cmp_w1, cmp_w2):
    half = CMP_BLOCK // 2
    eye = jnp.eye(HKV, dtype=F32)
    w1 = cmp_w1.reshape(2, CMP_BLOCK, DH, DH)
    w1big = jnp.einsum('jlde,hk->jlhdke', w1, eye)
    w1lo = w1big[:, :half].reshape(2, half * HKV * DH, HKV * DH).astype(BF16)
    w1hi = w1big[:, half:].reshape(2, half * HKV * DH, HKV * DH).astype(BF16)
    pos = jnp.broadcast_to(cmp_pos[:, :, None, :], (2, CMP_BLOCK, HKV, DH))
    poslo = pos[:, :half].reshape(2, 1, half * HKV * DH)
    poshi = pos[:, half:].reshape(2, 1, half * HKV * DH)
    w2big = jnp.einsum('jde,hk->jhdke', cmp_w2, eye).reshape(2, HKV * DH, HKV * DH).astype(BF16)
    return w1lo, w1hi, poslo, poshi, w2big


def _compress_kernel(kr_ref, vr_ref, w1lo_ref, w1hi_ref, poslo_ref, poshi_ref, w2_ref, kc_ref, vc_ref):
    for j, (src, dst) in enumerate(((kr_ref, kc_ref), (vr_ref, vc_ref))):
        r = src[0].astype(F32)
        a = jnp.dot((r + poslo_ref[j]).astype(BF16), w1lo_ref[j], preferred_element_type=F32)
        b = jnp.dot((r + poshi_ref[j]).astype(BF16), w1hi_ref[j], preferred_element_type=F32)
        hid = a + pltpu.roll(b, b.shape[0] - 1, 0)
        out = jnp.dot(_silu(hid).astype(BF16), w2_ref[j], preferred_element_type=F32)
        if j == 0:
            dst[0] = out.astype(BF16)
        else:
            dst[0, 0] = out.astype(BF16)
            dst[0, 1] = pltpu.roll(out, DH, 1).astype(BF16)


def _compress(kr, vr, cw):
    b, ng, width = kr.shape
    w1lo, w1hi, poslo, poshi, w2big = cw
    full = lambda a: pl.BlockSpec(a.shape, lambda i: (0,) * a.ndim)
    bspec = pl.BlockSpec((1, ng, width), lambda i: (i, 0, 0))
    return pl.pallas_call(
        _compress_kernel,
        out_shape=[jax.ShapeDtypeStruct((b, ng, HKV * DH), BF16),
                   jax.ShapeDtypeStruct((b, HKV, ng, HKV * DH), BF16)],
        grid=(b,),
        in_specs=[bspec, bspec, full(w1lo), full(w1hi), full(poslo), full(poshi), full(w2big)],
        out_specs=[pl.BlockSpec((1, ng, HKV * DH), lambda i: (i, 0, 0)),
                   pl.BlockSpec((1, HKV, ng, HKV * DH), lambda i: (i, 0, 0, 0))],
        compiler_params=_cparams("parallel"),
        name="nsa_compress",
    )(kr, vr, w1lo, w1hi, poslo, poshi, w2big)


def _bucket_thresholds():
    d = np.arange(MAX_DISTANCE + 1)
    max_exact = N_BUCKETS // 2
    nf = np.maximum(d, max_exact).astype(np.float32)
    large = max_exact + (np.log(nf / np.float32(max_exact)) / np.float32(math.log(MAX_DISTANCE / max_exact))
                         * np.float32(N_BUCKETS - max_exact)).astype(np.int32)
    bucket = np.where(d < max_exact, d, np.minimum(large, N_BUCKETS - 1))
    assert np.all(np.diff(bucket) >= 0) and bucket[MAX_DISTANCE] == N_BUCKETS - 1
    return [int(np.argmax(bucket >= k)) for k in range(N_BUCKETS)]


def _bias_kernel(rb_ref, bc_ref, bw_ref, bs_ref, *, n_cmp):
    i = pl.program_id(0)
    thr = _bucket_thresholds()

    def table(dist, valid, hd, shift):
        v = jnp.full(dist.shape, rb_ref[0, hd], F32)
        for k in range(1, N_BUCKETS):
            v = jnp.where(dist >= thr[k], rb_ref[k, hd], v)
        return jnp.where(valid, (v - shift) * LOG2E, NEG)

    row = lax.broadcasted_iota(jnp.int32, (TQ, LANES), 0)
    col = lax.broadcasted_iota(jnp.int32, (TQ, LANES), 1)
    dist_c = i * TQ + row - (col * CMP_STRIDE + CMP_BLOCK - 1)
    valid_c = (dist_c >= 0) & (col < n_cmp)
    for hd in range(NSA_HEADS):
        bc_ref[hd // GQA, hd % GQA] = table(dist_c, valid_c, hd, 0.0)

    @pl.when(i == 0)
    def _():
        band = WINDOW + TQ
        qi_w = lax.broadcasted_iota(jnp.int32, (TQ, band), 0)
        dist_w = qi_w + WINDOW - lax.broadcasted_iota(jnp.int32, (TQ, band), 1)
        valid_w = (dist_w >= 0) & (dist_w < WINDOW)
        qi_s = lax.broadcasted_iota(jnp.int32, (TQ, 2 * TQ), 0)
        dist_s = qi_s + TQ - lax.broadcasted_iota(jnp.int32, (TQ, 2 * TQ), 1)
        for hd in range(NSA_HEADS):
            k, g = hd // GQA, hd % GQA
            bw_ref[k, g * TQ:(g + 1) * TQ, :] = table(dist_w, valid_w, hd, 0.0)
            bs_ref[k, g * TQ:(g + 1) * TQ, :] = table(dist_s, dist_s >= 0, hd, rb_ref[N_BUCKETS - 1, hd])


def _bias_tables(rel_bias, s):
    n_cmp = (s - CMP_BLOCK) // CMP_STRIDE + 1
    assert n_cmp < LANES and TQ >= MAX_DISTANCE
    band = WINDOW + TQ
    return pl.pallas_call(
        functools.partial(_bias_kernel, n_cmp=n_cmp),
        out_shape=[jax.ShapeDtypeStruct((HKV, GQA, s, LANES), F32),
                   jax.ShapeDtypeStruct((HKV, GQA * TQ, band), F32),
                   jax.ShapeDtypeStruct((HKV, GQA * TQ, 2 * TQ), F32)],
        grid=(s // TQ,),
        in_specs=[pl.BlockSpec(memory_space=pltpu.SMEM)],
        out_specs=[pl.BlockSpec((HKV, GQA, TQ, LANES), lambda i: (0, 0, i, 0)),
                   pl.BlockSpec((HKV, GQA * TQ, band), lambda i: (0, 0, 0)),
                   pl.BlockSpec((HKV, GQA * TQ, 2 * TQ), lambda i: (0, 0, 0))],
        compiler_params=_cparams("arbitrary"),
        name="nsa_bias_tables",
    )(rel_bias)


def _sel_tables(s):
    n_cmp = (s - CMP_BLOCK) // CMP_STRIDE + 1
    n_sel = s // SEL_BLOCK
    c_start = np.arange(LANES) * CMP_STRIDE
    s_start = np.arange(n_sel) * SEL_BLOCK
    ovl = ((c_start[None, :] < s_start[:, None] + SEL_BLOCK)
           & (c_start[None, :] + CMP_BLOCK > s_start[:, None])
           & (np.arange(LANES)[None, :] < n_cmp)).astype(np.float32)
    key_blk = np.arange(s) // SEL_BLOCK
    expand = np.where(key_blk[None, :] == np.arange(LANES)[:, None], NEG, 0.0).astype(np.float32)
    expand_wide = expand.reshape(LANES, s // (2 * TQ), 2 * TQ).transpose(1, 0, 2)
    expand = expand.reshape(LANES, s // TQ, TQ).transpose(1, 0, 2)
    return jnp.asarray(ovl, BF16), jnp.asarray(expand, BF16), jnp.asarray(expand_wide, BF16)


def _nsa_kernel(q_ref, kc_ref, vc_ref, ks_ref, vs_ref, kw_ref, vw_ref, gate_ref,
                bc_ref, bw_ref, bs_ref, ovl_ref, exp_ref, expw_ref, gn_ref, o_ref,
                qs_ref, s_ref, m_ref, acc_ref, uns_ref, oc_ref, os_ref, ob_ref):
    i = pl.program_id(1)
    rows = GQA * TQ
    n_sel = ovl_ref.shape[0]
    n_band = WINDOW // TQ + 1
    batch = range(NSA_NB)
    units = [(b, h) for b in batch for h in range(HKV)]
    nt_dims = (((1,), (1,)), ((), ()))

    zeros = jnp.zeros((TQ, DH), F32)
    for b in batch:
        for hd in range(NSA_HEADS):
            h, g = hd // GQA, hd % GQA
            piece = q_ref[b, :, hd * DH:(hd + 1) * DH].astype(F32)
            padded = jnp.concatenate([piece, zeros] if h == 0 else [zeros, piece], axis=1)
            qs_ref[b, h, g * TQ:(g + 1) * TQ, :] = padded.astype(BF16)

    def chunk_rows(c, n=1):
        return pl.ds(pl.multiple_of(c * TQ, TQ), n * TQ)

    def row_max_to_lanes():
        for u in units:
            m_ref[u] = jnp.broadcast_to(jnp.max(m_ref[u], axis=-1, keepdims=True), (rows, TQ))

    def normalise(acc):
        row_sum = pltpu.roll(acc, DH, 1)
        return acc / jnp.maximum(row_sum, 1e-30)

    for b, h in units:
        bias_c = bc_ref[h].reshape(rows, LANES)
        sc = lax.dot_general(qs_ref[b, h], kc_ref[b], nt_dims, preferred_element_type=F32) + bias_c
        mc = jnp.max(sc, axis=-1, keepdims=True)
        pc = jnp.where(bias_c > 0.5 * NEG, jnp.exp2(sc - mc), 0.0)
        pc = pc / jnp.maximum(jnp.sum(pc, axis=-1, keepdims=True), 1e-30)
        oc_ref[b, h] = jnp.dot(pc.astype(BF16), vc_ref[b, h], preferred_element_type=F32)

        psum = jnp.sum(pc.reshape(GQA, TQ, LANES), axis=0)
        p_hi = psum.astype(BF16)
        p_lo = (psum - p_hi.astype(F32)).astype(BF16)
        imp = (lax.dot_general(ovl_ref[...], p_hi, nt_dims, preferred_element_type=F32)
               + lax.dot_general(ovl_ref[...], p_lo, nt_dims, preferred_element_type=F32))
        blk = lax.broadcasted_iota(jnp.int32, (n_sel, TQ), 0)
        tok = lax.broadcasted_iota(jnp.int32, (n_sel, TQ), 1) + i * TQ
        blk_of_t = tok // SEL_BLOCK
        forced = (blk == 0) | (blk == blk_of_t) | (blk == blk_of_t - 1)
        score = jnp.where(forced, FORCED_SCORE, jnp.where(blk <= blk_of_t, imp, -1.0))
        rank = jnp.zeros((n_sel, TQ), F32)
        for mm in range(n_sel):
            sm = score[mm:mm + 1, :]
            ahead = (sm > score) | ((sm == score) & (blk > mm))
            rank = rank + jnp.where(ahead, 1.0, 0.0)
        unsel_t = jnp.where(rank < min(SEL_TOP, n_sel), 0.0, 1.0)
        unsel_t = jnp.concatenate([unsel_t, jnp.zeros((LANES - n_sel, TQ), F32)], axis=0).astype(BF16)
        eye = (lax.broadcasted_iota(jnp.int32, (TQ, TQ), 0)
               == lax.broadcasted_iota(jnp.int32, (TQ, TQ), 1)).astype(BF16)
        uns_ref[b, h] = lax.dot_general(eye, unsel_t, nt_dims, preferred_element_type=F32).astype(BF16)

    def sel_scores(c, bias_cols):
        for b in batch:
            k = ks_ref[b, chunk_rows(c), :]
            for h in range(HKV):
                madd = jnp.dot(uns_ref[b, h], exp_ref[c], preferred_element_type=F32)
                s = lax.dot_general(qs_ref[b, h], k, nt_dims, preferred_element_type=F32)
                s = (s.reshape(GQA, TQ, TQ) + madd[None]).reshape(rows, TQ)
                if bias_cols is not None:
                    s = s + bs_ref[h, :, bias_cols:bias_cols + TQ]
                s_ref[b, h, c] = s
                m_ref[b, h] = jnp.maximum(m_ref[b, h], s)

    def sel_scores_pair(c0, expand, near):
        for b in batch:
            k = ks_ref[b, chunk_rows(c0, 2), :]
            for h in range(HKV):
                madd = jnp.dot(uns_ref[b, h], expand, preferred_element_type=F32)
                s = lax.dot_general(qs_ref[b, h], k, nt_dims, preferred_element_type=F32)
                s = (s.reshape(GQA, TQ, 2 * TQ) + madd[None]).reshape(rows, 2 * TQ)
                if near:
                    s = s + bs_ref[h]
                s_ref[b, h, c0] = s[:, :TQ]
                s_ref[b, h, c0 + 1] = s[:, TQ:]
                m_ref[b, h] = jnp.maximum(m_ref[b, h], jnp.maximum(s[:, :TQ], s[:, TQ:]))

    def sel_scores_wide(j):
        sel_scores_pair(2 * j, expw_ref[j], False)

    def sel_scores_near():
        expand = jnp.concatenate([exp_ref[i - 1], exp_ref[i]], axis=1)
        sel_scores_pair(i - 1, expand, True)

    def softmax_pv(c):
        for b, h in units:
            p = jnp.exp2(s_ref[b, h, c] - m_ref[b, h])
            acc_ref[b, h] += jnp.dot(p.astype(BF16), vs_ref[h, b, chunk_rows(c), :], preferred_element_type=F32)

    def softmax_pv_wide(j):
        for b, h in units:
            m = m_ref[b, h]
            p = jnp.concatenate([jnp.exp2(s_ref[b, h, 2 * j] - m), jnp.exp2(s_ref[b, h, 2 * j + 1] - m)], axis=1)
            acc_ref[b, h] += jnp.dot(p.astype(BF16), vs_ref[h, b, chunk_rows(2 * j, 2), :],
                                     preferred_element_type=F32)

    def loop(n, body):
        lax.fori_loop(0, n, lambda j, carry: (body(j), carry)[1], 0)

    m_ref[...] = jnp.full(m_ref.shape, NEG, F32)
    n_far = jnp.maximum(i - 1, 0)
    loop(n_far // 2, sel_scores_wide)
    pl.when(n_far % 2 == 1)(lambda: sel_scores(n_far - 1, None))
    pl.when(i >= 1)(sel_scores_near)
    pl.when(i == 0)(lambda: sel_scores(0, TQ))
    row_max_to_lanes()
    acc_ref[...] = jnp.zeros(acc_ref.shape, F32)
    loop((i + 1) // 2, softmax_pv_wide)
    pl.when((i + 1) % 2 == 1)(lambda: softmax_pv(i))
    for u in units:
        os_ref[u] = normalise(acc_ref[u])

    band = WINDOW + TQ
    col = lax.broadcasted_iota(jnp.int32, (1, band), 1)
    before_start = jnp.where(col + i * TQ >= WINDOW, 0.0, NEG)
    o_win = {}
    for b, h in units:
        k = kw_ref[b, chunk_rows(i, n_band), :]
        s = lax.dot_general(qs_ref[b, h], k, nt_dims, preferred_element_type=F32) + bw_ref[h] + before_start
        p = jnp.exp2(s - jnp.max(s, axis=-1, keepdims=True))
        o_win[b, h] = normalise(jnp.dot(p.astype(BF16), vw_ref[h, b, chunk_rows(i, n_band), :],
                                        preferred_element_type=F32))

    for b in batch:
        gates = gate_ref[b]
        for hd in range(NSA_HEADS):
            h, g = hd // GQA, hd % GQA
            r = slice(g * TQ, (g + 1) * TQ)
            o = (gates[:, hd:hd + 1] * oc_ref[b, h, r, :]
                 + gates[:, NSA_HEADS + hd:NSA_HEADS + hd + 1] * os_ref[b, h, r, :]
                 + gates[:, 2 * NSA_HEADS + hd:2 * NSA_HEADS + hd + 1] * o_win[b, h][r])
            ob_ref[b, :, hd * DH:(hd + 1) * DH] = o[:, :DH]
        o = ob_ref[b]
        ms = jnp.mean(o * o, axis=-1, keepdims=True)
        o_ref[b] = (o * lax.rsqrt(ms + EPS) * gn_ref[...]).astype(BF16)


def _nsa(q, kc, vc, ks, vs, kw, vw, gates, tables, attn_out_norm):
    b, s, _ = q.shape
    bias_c, bias_w, bias_s, ovl, expand, expand_wide = tables
    gn = attn_out_norm.reshape(1, NSA_WIDTH)
    nb = NSA_NB
    assert b % nb == 0 and (s // TQ) % 2 == 0
    full = lambda a: pl.BlockSpec(a.shape, lambda bi, i: (0,) * a.ndim)
    tile = lambda w: pl.BlockSpec((nb, TQ, w), lambda bi, i: (bi, i, 0))
    kseq = lambda a: pl.BlockSpec((nb,) + a.shape[1:], lambda bi, i: (bi, 0, 0))
    vseq = lambda a: pl.BlockSpec((HKV, nb) + a.shape[2:], lambda bi, i: (0, bi, 0, 0))
    assert kw.shape[1] == s + WINDOW and vw.shape[2] == s + WINDOW
    rows = GQA * TQ
    unit = (nb, HKV)
    return pl.pallas_call(
        _nsa_kernel,
        out_shape=jax.ShapeDtypeStruct((b, s, NSA_WIDTH), BF16),
        grid=(b // nb, s // TQ),
        in_specs=[tile(NSA_WIDTH),
                  pl.BlockSpec((nb,) + kc.shape[1:], lambda bi, i: (bi, 0, 0)),
                  pl.BlockSpec((nb,) + vc.shape[1:], lambda bi, i: (bi, 0, 0, 0)),
                  kseq(ks), vseq(vs), kseq(kw), vseq(vw), tile(LANES),
                  pl.BlockSpec((HKV, GQA, TQ, LANES), lambda bi, i: (0, 0, i, 0)),
                  full(bias_w), full(bias_s), full(ovl), full(expand), full(expand_wide), full(gn)],
        out_specs=tile(NSA_WIDTH),
        scratch_shapes=[pltpu.VMEM(unit + (rows, LANES), BF16),
                        pltpu.VMEM(unit + (s // TQ, rows, TQ), F32),
                        pltpu.VMEM(unit + (rows, TQ), F32),
                        pltpu.VMEM(unit + (rows, LANES), F32),
                        pltpu.VMEM(unit + (TQ, LANES), BF16),
                        pltpu.VMEM(unit + (rows, LANES), F32),
                        pltpu.VMEM(unit + (rows, LANES), F32),
                        pltpu.VMEM((nb, TQ, NSA_WIDTH), F32)],
        compiler_params=_cparams("parallel", "arbitrary"),
        name="nsa_attention",
    )(q, kc, vc, ks, vs, kw, vw, gates, bias_c, bias_w, bias_s, ovl, expand, expand_wide, gn)


def _ssd_kernel(xbc_ref, z_ref, dt_ref, cw_ref, cb_ref, dtb_ref, alog_ref, dsk_ref, ng_ref, wsrc_ref, o_ref, wdst_ref,
                xbuf_ref, state_ref, y_ref):
    c = pl.program_id(1)
    wdst_ref[...] = wsrc_ref[...].astype(BF16)
    L, P, N = SSD_L, SSD_P, SSD_N
    hpg = SSD_HEADS // SSD_GROUPS
    pad = 8

    @pl.when(c == 0)
    def _():
        xbuf_ref[:, 0:pad, :] = jnp.zeros((SSD_NB, pad, SSD_CONV_DIM), F32)
        state_ref[...] = jnp.zeros(state_ref.shape, F32)

    ri = lax.broadcasted_iota(jnp.int32, (L, L), 0)
    ci = lax.broadcasted_iota(jnp.int32, (L, L), 1)
    causal = ri >= ci
    nt_dims = (((1,), (1,)), ((), ()))
    a_neg = -jnp.exp(alog_ref[...])

    for bb in range(SSD_NB):
        xbuf_ref[bb, pad:pad + L, :] = xbc_ref[bb]
        conv = cb_ref[...]
        for k in range(SSD_CONV):
            shift = SSD_CONV - 1 - k
            conv = conv + xbuf_ref[bb, pad - shift:pad - shift + L, :] * cw_ref[k:k + 1, :]
        xbuf_ref[bb, 0:pad, :] = xbuf_ref[bb, L:L + pad, :]
        xa = _silu(conv)
        xs = xa[:, :SSD_WIDTH]
        bm = xa[:, SSD_WIDTH:SSD_WIDTH + SSD_GROUPS * N]
        cm = xa[:, SSD_WIDTH + SSD_GROUPS * N:]

        dtv = dt_ref[bb] + dtb_ref[...]
        dt = jnp.maximum(dtv, 0.0) + jnp.log1p(jnp.exp(-jnp.abs(dtv)))
        cs = jnp.dot(causal.astype(F32), dt * a_neg, preferred_element_type=F32,
                     precision=lax.Precision.HIGHEST)
        cs_t = cs.T

        for gr in range(SSD_GROUPS):
            b_g = bm[:, gr * N:(gr + 1) * N]
            c_g = cm[:, gr * N:(gr + 1) * N]
            scores = lax.dot_general(c_g.astype(BF16), b_g.astype(BF16), nt_dims, preferred_element_type=F32)
            b_gt = b_g.T
            for hh in range(hpg):
                h = gr * hpg + hh
                cs_col = cs[:, h:h + 1]
                cs_row = cs_t[h:h + 1, :]
                cs_last = cs[L - 1:L, h:h + 1]
                decay = jnp.exp(jnp.where(causal, cs_col - cs_row, NEG))
                xs_h = xs[:, h * P:(h + 1) * P]
                xc = (xs_h * dt[:, h:h + 1]).astype(BF16)
                y = jnp.dot((scores * decay).astype(BF16), xc, preferred_element_type=F32)
                prev = state_ref[bb, h]
                y = y + jnp.dot((c_g * jnp.exp(cs_col)).astype(BF16), prev.astype(BF16),
                                preferred_element_type=F32)
                contrib = jnp.dot((b_gt * jnp.exp(cs_last - cs_row)).astype(BF16), xc,
                                  preferred_element_type=F32)
                state_ref[bb, h] = jnp.exp(cs_last) * prev + contrib
                y_ref[bb, :, h * P:(h + 1) * P] = y + xs_h * dsk_ref[:, h * P:(h + 1) * P]

        y = y_ref[bb] * _silu(z_ref[bb])
        gw = SSD_WIDTH // SSD_GROUPS
        for gr in range(SSD_GROUPS):
            yg = y[:, gr * gw:(gr + 1) * gw]
            ms = jnp.mean(yg * yg, axis=-1, keepdims=True)
            o_ref[bb, :, gr * gw:(gr + 1) * gw] = (yg * lax.rsqrt(ms + EPS)
                                                   * ng_ref[:, gr * gw:(gr + 1) * gw]).astype(BF16)


def _ssd(xbc, z, dt, conv_w, conv_b, dt_bias, a_log, d_skip, norm_g, w_cast):
    b, s, _ = xbc.shape
    nb = SSD_NB
    assert b % nb == 0
    n_steps = (b // nb) * (s // SSD_L)
    wr, wc = w_cast.shape
    assert wr % n_steps == 0
    wblk = pl.BlockSpec((wr // n_steps, wc), lambda bi, c: (bi * (s // SSD_L) + c, 0))
    padl = lambda v: jnp.pad(v, (0, LANES - v.shape[0])).reshape(1, LANES)
    args = (conv_w, conv_b.reshape(1, SSD_CONV_DIM), padl(dt_bias), padl(a_log),
            jnp.repeat(d_skip, SSD_P).reshape(1, SSD_WIDTH), norm_g.reshape(1, SSD_WIDTH))
    full = lambda a: pl.BlockSpec(a.shape, lambda bi, c: (0,) * a.ndim)
    blk = lambda w: pl.BlockSpec((nb, SSD_L, w), lambda bi, c: (bi, c, 0))
    return pl.pallas_call(
        _ssd_kernel,
        out_shape=[jax.ShapeDtypeStruct((b, s, SSD_WIDTH), BF16), jax.ShapeDtypeStruct((wr, wc), BF16)],
        grid=(b // nb, s // SSD_L),
        in_specs=[blk(SSD_CONV_DIM), blk(SSD_WIDTH), blk(LANES)] + [full(a) for a in args] + [wblk],
        out_specs=[blk(SSD_WIDTH), wblk],
        scratch_shapes=[pltpu.VMEM((nb, SSD_L + 8, SSD_CONV_DIM), F32),
                        pltpu.VMEM((nb, SSD_HEADS, SSD_N, SSD_P), F32),
                        pltpu.VMEM((nb, SSD_L, SSD_WIDTH), F32)],
        compiler_params=_cparams("parallel", "arbitrary"),
        name="ssd_mixer",
    )(xbc, z, dt, *args, w_cast)


def _out_proj_kernel(x_ref, on_ref, os_ref, wo_ref, fg_ref, rwt_ref, rbt_ref, wsrc_ref,
                     h_ref, hn_ref, comb_ref, slot_ref, wdst_ref):
    wdst_ref[...] = wsrc_ref[...].astype(BF16)
    h = (x_ref[...]
         + jnp.dot(on_ref[...], wo_ref[0:NSA_WIDTH, :], preferred_element_type=F32)
         + jnp.dot(os_ref[...], wo_ref[NSA_WIDTH:, :], preferred_element_type=F32))
    tm = h.shape[0]
    for j in range(H_SLAB):
        h_ref[pl.ds(j, tm, stride=H_SLAB), :] = h[:, j * LANES:(j + 1) * LANES]
    ms = jnp.mean(h * h, axis=-1, keepdims=True)
    hn = h * lax.rsqrt(ms + EPS) * fg_ref[...]
    for j in range(H_SLAB):
        hn_ref[pl.ds(j, tm, stride=H_SLAB), :] = hn[:, j * LANES:(j + 1) * LANES]

    nt_dims = (((1,), (1,)), ((), ()))
    hn_hi = hn.astype(BF16)
    hn_lo = (hn - hn_hi.astype(F32)).astype(BF16)
    rwt = rwt_ref[...]
    rwt_hi = rwt.astype(BF16)
    rwt_lo = (rwt - rwt_hi.astype(F32)).astype(BF16)
    both = lax.dot_general(jnp.concatenate([rwt_hi, rwt_lo], axis=0), hn_hi, nt_dims, preferred_element_type=F32)
    logits = (both[:N_EXPERTS] + both[N_EXPERTS:]
              + lax.dot_general(rwt_hi, hn_lo, nt_dims, preferred_element_type=F32) + rbt_ref[...])
    row = lax.broadcasted_iota(jnp.int32, logits.shape, 0)
    work = logits
    picks = []
    for _ in range(TOP_K):
        v = jnp.max(work, axis=0, keepdims=True)
        idx = jnp.min(jnp.where(work == v, row, N_EXPERTS), axis=0, keepdims=True)
        hit = row == idx
        picks.append((v, hit))
        work = jnp.where(hit, -3e38, work)
    v0 = picks[0][0]
    es = [jnp.exp(v - v0) for v, _ in picks]
    den = es[0] + es[1] + es[2] + es[3]
    comb = jnp.zeros_like(logits)
    slot = jnp.zeros(logits.shape, jnp.int32)
    for k, (e, (_, hit)) in enumerate(zip(es, picks)):
        comb = comb + jnp.where(hit, e / den, 0.0)
        slot = jnp.where(hit, k + 1, slot)
    comb_ref[...] = comb
    slot_ref[...] = slot


def _out_proj(x2, o_nsa, o_ssd, w_out, ffn_norm, router_w, router_b, w_cast, tm=512):
    t = x2.shape[0]
    wr, wc = w_cast.shape
    assert wr % (t // tm) == 0
    wblk = pl.BlockSpec((wr // (t // tm), wc), lambda i: (i, 0))
    row = lambda w: pl.BlockSpec((tm, w), lambda i: (i, 0))
    full = lambda a: pl.BlockSpec(a.shape, lambda i: (0,) * a.ndim)
    wo = w_out.astype(BF16)
    fg = ffn_norm.reshape(1, D_MODEL)
    rwt = router_w.T
    rbt = router_b.reshape(N_EXPERTS, 1)
    tok = pl.BlockSpec((N_EXPERTS, tm), lambda i: (0, i))
    return pl.pallas_call(
        _out_proj_kernel,
        out_shape=[jax.ShapeDtypeStruct((t * H_SLAB, LANES), F32),
                   jax.ShapeDtypeStruct((t * H_SLAB, LANES), F32),
                   jax.ShapeDtypeStruct((N_EXPERTS, t), F32),
                   jax.ShapeDtypeStruct((N_EXPERTS, t), jnp.int32),
                   jax.ShapeDtypeStruct((wr, wc), BF16)],
        grid=(t // tm,),
        in_specs=[row(D_MODEL), row(NSA_WIDTH), row(SSD_WIDTH), full(wo), full(fg), full(rwt), full(rbt), wblk],
        out_specs=[pl.BlockSpec((tm * H_SLAB, LANES), lambda i: (i, 0)),
                   pl.BlockSpec((tm * H_SLAB, LANES), lambda i: (i, 0)), tok, tok, wblk],
        compiler_params=_cparams("parallel"),
        name="out_proj_router",
    )(x2, o_nsa, o_ssd, wo, fg, rwt, rbt, w_cast)


MOE_CHUNK = 4096
MOE_TILE = 192
MOE_GROUP = 256
ROUTE_TILE = 256


def _route_kernel(comb_ref, slot_ref, dest_ref, wrow_ref, starts_ref, pos_ref):
    ne, tc = comb_ref.shape
    ri = lax.broadcasted_iota(jnp.int32, (ROUTE_TILE, ROUTE_TILE), 0)
    ci = lax.broadcasted_iota(jnp.int32, (ROUTE_TILE, ROUTE_TILE), 1)
    earlier = (ri < ci).astype(BF16)
    carry = jnp.zeros((ne, 1), F32)
    for j in range(tc // ROUTE_TILE):
        cols = slice(j * ROUTE_TILE, (j + 1) * ROUTE_TILE)
        sel = jnp.where(slot_ref[:, cols] > 0, 1.0, 0.0)
        pos_ref[:, cols] = jnp.dot(sel.astype(BF16), earlier, preferred_element_type=F32) + carry
        carry = carry + jnp.sum(sel, axis=1, keepdims=True)
    ei = lax.broadcasted_iota(jnp.int32, (ne, LANES), 0)
    li = lax.broadcasted_iota(jnp.int32, (ne, LANES), 1)
    starts_row = jnp.sum(jnp.where(ei < li, carry, 0.0), axis=0, keepdims=True)
    starts_col = jnp.sum(jnp.where(ei == li, starts_row, 0.0), axis=1, keepdims=True)
    starts_ref[0] = jnp.broadcast_to(starts_row, (SUBLANES, LANES)).astype(jnp.int32)
    dest = pos_ref[...] + starts_col
    slot = slot_ref[...]
    comb = comb_ref[...]
    for k in range(TOP_K):
        hit = slot == k + 1
        dest_ref[0, :, k * tc:(k + 1) * tc] = jnp.sum(jnp.where(hit, dest, 0.0), axis=0,
                                                      keepdims=True).astype(jnp.int32)
        wrow_ref[0, :, k * tc:(k + 1) * tc] = jnp.sum(jnp.where(hit, comb, 0.0), axis=0, keepdims=True)


def _route(comb, slot, tc):
    ne, t = comb.shape
    nch = t // tc
    blk = pl.BlockSpec((ne, tc), lambda c: (0, c))
    return pl.pallas_call(
        _route_kernel,
        out_shape=[jax.ShapeDtypeStruct((nch, 1, TOP_K * tc), jnp.int32),
                   jax.ShapeDtypeStruct((nch, 1, TOP_K * tc), F32),
                   jax.ShapeDtypeStruct((nch, SUBLANES, LANES), jnp.int32)],
        grid=(nch,),
        in_specs=[blk, blk],
        out_specs=[pl.BlockSpec((1, 1, TOP_K * tc), lambda c: (c, 0, 0)),
                   pl.BlockSpec((1, 1, TOP_K * tc), lambda c: (c, 0, 0)),
                   pl.BlockSpec((1, SUBLANES, LANES), lambda c: (c, 0, 0))],
        scratch_shapes=[pltpu.VMEM((ne, tc), F32)],
        compiler_params=_cparams("parallel"),
        name="moe_route",
    )(comb, slot)


def _moe_kernel(starts_ref, dest_hbm, wrow_hbm, hn_hbm, h_hbm, wgu_ref, bgu_ref, wd_ref, bd_ref, fn_ref, o_hbm,
                x_ref, acc_ref, xs0_ref, xs1_ref, y0_ref, y1_ref, ob_ref, dest_s, wrow_s, tok_s, wt_s, st_s,
                sem, osem):
    c = pl.program_id(0)
    e = pl.program_id(1)
    tc = x_ref.shape[0] // H_SLAB
    n_rows = TOP_K * tc

    def slab(i, n, width):
        return pl.ds(pl.multiple_of(i * width, width), n * width)

    half_len = n_rows + MOE_TILE
    cur = (c % 2) * half_len
    nxt = half_len - cur

    def invert(j0, u, half):
        r = half + dest_s[j0 + u]
        tok_s[r] = (j0 & (tc - 1)) + u
        wt_s[r] = wrow_s[j0 + u]

    def gather(r0, xs_ref):
        i0 = cur + r0
        for r in range(MOE_TILE):
            xs_ref[slab(r, 1, H_SLAB), :] = x_ref[slab(tok_s[i0 + r], 1, H_SLAB), :]

    def ffn(xs_ref, y_ref):
        x = jnp.concatenate([xs_ref[pl.ds(j, MOE_TILE, stride=H_SLAB), :].astype(BF16) for j in range(H_SLAB)],
                            axis=1)
        gu = jnp.dot(x, wgu_ref[0], preferred_element_type=F32) + bgu_ref[0]
        gate = jnp.minimum(gu[:, :D_FF], SWIGLU_LIMIT)
        up = jnp.clip(gu[:, D_FF:], -SWIGLU_LIMIT, SWIGLU_LIMIT)
        act = (up + 1.0) * gate / (1.0 + jnp.exp(-SWIGLU_ALPHA * gate))
        y = jnp.dot(act.astype(BF16), wd_ref[0], preferred_element_type=F32) + bd_ref[0]
        for j in range(H_SLAB):
            y_ref[pl.ds(j, MOE_TILE, stride=H_SLAB), :] = y[:, j * LANES:(j + 1) * LANES]

    def scatter(r0, n_valid, y_ref):
        i0 = cur + r0
        for g in range(MOE_TILE // SUBLANES):
            rows = []
            for u in range(SUBLANES):
                r = g * SUBLANES + u
                tok = jnp.where(r < n_valid, tok_s[i0 + r], tc)
                rows.append((tok, acc_ref[slab(tok, 1, H_SLAB), :] + wt_s[i0 + r] * y_ref[slab(r, 1, H_SLAB), :]))
            for tok, val in rows:
                acc_ref[slab(tok, 1, H_SLAB), :] = val

    chunk_copies = lambda: (pltpu.make_async_copy(hn_hbm.at[slab(c * tc, tc, H_SLAB), :], x_ref, sem.at[0]),
                            pltpu.make_async_copy(h_hbm.at[slab(c * tc, tc, H_SLAB), :],
                                                  acc_ref.at[pl.ds(0, tc * H_SLAB), :], sem.at[1]))
    map_copies = lambda cc: (pltpu.make_async_copy(dest_hbm.at[cc, 0], dest_s, sem.at[2]),
                             pltpu.make_async_copy(wrow_hbm.at[cc, 0], wrow_s, sem.at[3]))

    @pl.when(e == 0)
    def _():
        for cp in chunk_copies():
            cp.start()

        @pl.when(c == 0)
        def _():
            for cp in map_copies(0):
                cp.start()
            for cp in map_copies(0):
                cp.wait()

            def invert_slice(g, carry):
                for u in range(MOE_GROUP):
                    invert(g * MOE_GROUP, u, cur)
                return carry

            lax.fori_loop(0, n_rows // MOE_GROUP, invert_slice, 0)
            for half in (0, half_len):
                for u in range(MOE_TILE):
                    tok_s[half + n_rows + u] = 0
                    wt_s[half + n_rows + u] = 0.0

        @pl.when(c + 1 < pl.num_programs(0))
        def _():
            for cp in map_copies(c + 1):
                cp.start()
            for cp in map_copies(c + 1):
                cp.wait()

        for cp in chunk_copies():
            cp.wait()
        acc_ref[pl.ds(tc * H_SLAB, H_SLAB), :] = jnp.zeros((H_SLAB, LANES), F32)
        st_s[0] = 0
        st_s[1] = 0
        st_s[2] = 0
        for y_ref in (y0_ref, y1_ref):
            y_ref[...] = jnp.zeros(y_ref.shape, F32)
        gather(0, xs0_ref)

    base = starts_ref[c * LANES + e]
    n_e = starts_ref[c * LANES + e + 1] - base
    xs_bufs = (xs0_ref, xs1_ref)
    y_bufs = (y0_ref, y1_ref)

    def tile_step(p, k, r0_next, prev_r0, prev_nv):
        first = jnp.minimum(k, n_rows // MOE_GROUP - 1) * MOE_GROUP
        for u in range(MOE_GROUP):
            invert(first, u, nxt)
        gather(r0_next, xs_bufs[1 - p])
        scatter(prev_r0, prev_nv, y_bufs[1 - p])
        ffn(xs_bufs[p], y_bufs[p])

    def tile(j, carry):
        r0 = base + j * MOE_TILE
        r0_next = jnp.minimum(r0 + MOE_TILE, base + n_e)
        k = st_s[0]
        prev_r0 = st_s[1]
        prev_nv = st_s[2]
        for p in range(2):
            pl.when(k % 2 == p)(functools.partial(tile_step, p, k, r0_next, prev_r0, prev_nv))
        st_s[0] = k + 1
        st_s[1] = r0
        st_s[2] = jnp.minimum(n_e - j * MOE_TILE, MOE_TILE)
        return carry

    lax.fori_loop(0, (n_e + MOE_TILE - 1) // MOE_TILE, tile, 0)

    @pl.when(e == pl.num_programs(1) - 1)
    def _():
        for p in range(2):
            pl.when(st_s[0] % 2 == p)(functools.partial(scatter, st_s[1], st_s[2], y_bufs[1 - p]))
        n_groups = tc // MOE_GROUP
        out_copy = lambda g, buf: pltpu.make_async_copy(
            ob_ref.at[buf], o_hbm.at[pl.ds(pl.multiple_of(c * tc + g * MOE_GROUP, MOE_GROUP), MOE_GROUP), :],
            osem.at[buf])

        def norm(g, carry):
            buf = g % 2

            @pl.when(g >= 2)
            def _():
                out_copy(g - 2, buf).wait()

            first = pl.multiple_of(g * MOE_GROUP * H_SLAB, MOE_GROUP * H_SLAB)
            hs = [acc_ref[pl.ds(first + j, MOE_GROUP, stride=H_SLAB), :] for j in range(H_SLAB)]
            ss = hs[0] * hs[0]
            for hj in hs[1:]:
                ss = ss + hj * hj
            inv = lax.rsqrt(jnp.sum(ss, axis=-1, keepdims=True) / D_MODEL + EPS)
            for j, hj in enumerate(hs):
                ob_ref[buf, :, j * LANES:(j + 1) * LANES] = hj * inv * fn_ref[:, j * LANES:(j + 1) * LANES]
            out_copy(g, buf).start()
            return carry

        lax.fori_loop(0, n_groups, norm, 0)
        for g in range(max(n_groups - 2, 0), n_groups):
            out_copy(g, g % 2).wait()


def _moe(hnp, comb, slot, h1, w_gate_up, b_gate_up, w_down, b_down, final_norm, tc=MOE_CHUNK):
    t = comb.shape[1]
    tc = min(tc, t)
    assert tc & (tc - 1) == 0 and tc % MOE_GROUP == 0 and MOE_TILE <= MOE_GROUP
    nch = t // tc
    dest, wrow, starts = _route(comb, slot, tc)
    starts = starts[:, 0, :].reshape(nch * LANES)
    wgu = w_gate_up.astype(BF16)
    wd = w_down.astype(BF16)
    bgu = b_gate_up.reshape(N_EXPERTS, 1, 2 * D_FF)
    bd = b_down.reshape(N_EXPERTS, 1, D_MODEL)
    fn = final_norm.reshape(1, D_MODEL)
    anyspace = pl.BlockSpec(memory_space=pl.ANY)
    exp = lambda a: pl.BlockSpec((1,) + a.shape[1:], lambda c, e, st: (e, 0, 0))
    return pl.pallas_call(
        _moe_kernel,
        out_shape=jax.ShapeDtypeStruct((t, D_MODEL), F32),
        grid_spec=pltpu.PrefetchScalarGridSpec(
            num_scalar_prefetch=1,
            grid=(nch, N_EXPERTS),
            in_specs=[anyspace, anyspace, anyspace, anyspace, exp(wgu), exp(bgu), exp(wd), exp(bd),
                      pl.BlockSpec(fn.shape, lambda c, e, st: (0, 0))],
            out_specs=anyspace,
            scratch_shapes=[pltpu.VMEM((tc * H_SLAB, LANES), F32),
                            pltpu.VMEM(((tc + 1) * H_SLAB, LANES), F32),
                            pltpu.VMEM((MOE_TILE * H_SLAB, LANES), F32),
                            pltpu.VMEM((MOE_TILE * H_SLAB, LANES), F32),
                            pltpu.VMEM((MOE_TILE * H_SLAB, LANES), F32),
                            pltpu.VMEM((MOE_TILE * H_SLAB, LANES), F32),
                            pltpu.VMEM((2, MOE_GROUP, D_MODEL), F32),
                            pltpu.SMEM((TOP_K * tc,), jnp.int32),
                            pltpu.SMEM((TOP_K * tc,), F32),
                            pltpu.SMEM((2 * (TOP_K * tc + MOE_TILE),), jnp.int32),
                            pltpu.SMEM((2 * (TOP_K * tc + MOE_TILE),), F32),
                            pltpu.SMEM((4,), jnp.int32),
                            pltpu.SemaphoreType.DMA((4,)),
                            pltpu.SemaphoreType.DMA((2,))]),
        compiler_params=_cparams("arbitrary", "arbitrary", vmem=MOE_VMEM_LIMIT),
        name="moe_experts",
    )(starts, dest, wrow, hnp, h1, wgu, bgu, wd, bd, fn)


def kernel(x, attn_norm, w_in, rel_bias, cmp_pos, cmp_w1, cmp_w2, attn_out_norm, conv_w, conv_b,
           dt_bias, a_log, d_skip, ssm_out_norm, w_out, ffn_norm, router_w, router_b,
           w_gate_up, b_gate_up, w_down, b_down, final_norm):
    b, s, d = x.shape
    t = b * s
    depth = w_in.shape[0]
    tables = tuple(_bias_tables(rel_bias, s)) + _sel_tables(s)
    h = x.reshape(t, d)
    for l in range(depth):
        (q, kc_raw, vc_raw, ks, vs, kw, vw, gates, z, xbc, dt) = _in_proj(h, attn_norm[l], _pad_w_in(w_in[l]))
        grp = CMP_STRIDE * HKV * DH
        kc, vc = _compress(kc_raw.reshape(b, s // CMP_STRIDE, grp), vc_raw.reshape(b, s // CMP_STRIDE, grp),
                           _compress_weights(cmp_pos[l], cmp_w1[l], cmp_w2[l]))
        seq = lambda a: a.reshape(a.shape[:-2] + (b, s, a.shape[-1]))
        front = lambda a: jnp.pad(a, [(0, 0)] * (a.ndim - 2) + [(WINDOW, 0), (0, 0)])
        o_nsa = _nsa(seq(q), kc, vc, seq(ks), seq(vs), front(seq(kw)), front(seq(vw)), seq(gates), tables,
                     attn_out_norm[l])
        o_ssd, wgu = _ssd(seq(xbc), seq(z), seq(dt), conv_w[l], conv_b[l], dt_bias[l], a_log[l], d_skip[l],
                          ssm_out_norm[l], w_gate_up[l].reshape(N_EXPERTS * D_MODEL, 2 * D_FF))
        h1, hnp, comb, slot, wd = _out_proj(h, o_nsa.reshape(t, NSA_WIDTH), o_ssd.reshape(t, SSD_WIDTH), w_out[l],
                                            ffn_norm[l], router_w[l], router_b[l],
                                            w_down[l].reshape(N_EXPERTS * D_FF, D_MODEL))
        assert depth == 1
        h = _moe(hnp, comb, slot, h1, wgu.reshape(N_EXPERTS, D_MODEL, 2 * D_FF), b_gate_up[l],
                 wd.reshape(N_EXPERTS, D_FF, D_MODEL), b_down[l], final_norm)
    return h.reshape(b, s, d)
```

```python
import functools
import math

import numpy as np
import jax
import jax.numpy as jnp
from jax import lax
from jax.experimental import pallas as pl
from jax.experimental.pallas import tpu as pltpu

F32 = jnp.float32
BF16 = jnp.bfloat16

D_MODEL = 1024
NSA_HEADS = 8
HKV = 2
GQA = NSA_HEADS // HKV
DH = 64
NSA_WIDTH = NSA_HEADS * DH
CMP_BLOCK = 32
CMP_STRIDE = 16
SEL_BLOCK = 64
SEL_TOP = 16
WINDOW = 512
TQ = 128
NSA_NB = 2
SSD_HEADS = 8
SSD_P = 64
SSD_WIDTH = SSD_HEADS * SSD_P
SSD_GROUPS = 2
SSD_N = 128
SSD_CONV = 4
SSD_L = 128
SSD_NB = 2
SSD_CONV_DIM = SSD_WIDTH + 2 * SSD_GROUPS * SSD_N
N_BUCKETS = 32
MAX_DISTANCE = 128
N_EXPERTS = 32
TOP_K = 4
D_FF = 1024
SWIGLU_LIMIT = 7.0
SWIGLU_ALPHA = 1.702

EPS = 1e-6
NEG = -1e30
FORCED_SCORE = 1e4
LOG2E = 1.4426950408889634
LANES = 128
SUBLANES = 8
H_SLAB = D_MODEL // LANES
VMEM_LIMIT = 56 * 1024 * 1024
MOE_VMEM_LIMIT = (2 * 4096 * D_MODEL * 4 + 2 * 3 * D_MODEL * D_FF * 2 + 12 * 1024 * 1024)


def _cparams(*sem, vmem=VMEM_LIMIT):
    return pltpu.CompilerParams(dimension_semantics=sem, vmem_limit_bytes=vmem)


def _silu(v):
    return v / (1.0 + jnp.exp(-v))


_Q0, _Q1 = 0, NSA_WIDTH
_KV0 = _Q1
_G0 = _KV0 + 6 * LANES
_DT0 = _G0 + LANES
_Z0 = _DT0 + LANES
_X0 = _Z0 + SSD_WIDTH
_WCOLS = _X0 + SSD_CONV_DIM
MXU_N = 256


def _pad_w_in(w_in):
    d = w_in.shape[0]
    nsa_cols = NSA_WIDTH + 6 * HKV * DH + 3 * NSA_HEADS
    wq_pad = w_in[:, :NSA_WIDTH]
    wkv = w_in[:, NSA_WIDTH:NSA_WIDTH + 6 * HKV * DH]
    wg = w_in[:, NSA_WIDTH + 6 * HKV * DH:nsa_cols]
    wg = jnp.pad(wg, ((0, 0), (0, LANES - wg.shape[1])))
    wz = w_in[:, nsa_cols:nsa_cols + SSD_WIDTH]
    wx = w_in[:, nsa_cols + SSD_WIDTH:nsa_cols + SSD_WIDTH + SSD_CONV_DIM]
    wdt = w_in[:, nsa_cols + SSD_WIDTH + SSD_CONV_DIM:]
    wdt = jnp.pad(wdt, ((0, 0), (0, LANES - wdt.shape[1])))
    return jnp.concatenate([wq_pad, wkv, wg, wdt, wz, wx], axis=1).astype(BF16)


def _in_proj_kernel(x_ref, g_ref, w_ref, q_ref, kc_ref, vc_ref, ks_ref, vs_ref, kw_ref, vw_ref,
                    gate_ref, z_ref, xbc_ref, dt_ref):
    x = x_ref[...]
    ms = jnp.mean(x * x, axis=-1, keepdims=True)
    xn = (x * lax.rsqrt(ms + EPS) * g_ref[...]).astype(BF16)

    def seg(lo, hi):
        return jnp.dot(xn, w_ref[:, lo:hi], preferred_element_type=F32)

    q_ref[...] = (seg(_Q0, _Q1) * (DH ** -0.5 * LOG2E)).astype(BF16)
    ones = jnp.ones((x.shape[0], LANES - DH), F32)
    kv_refs = (kc_ref, vc_ref, ks_ref, vs_ref, kw_ref, vw_ref)
    for j in range(0, len(kv_refs), MXU_N // LANES):
        pair = seg(_KV0 + j * LANES, _KV0 + j * LANES + MXU_N)
        kv_refs[j][...] = pair[:, :LANES].astype(BF16)
        val = pair[:, LANES:]
        if j == 0:
            kv_refs[j + 1][...] = val.astype(BF16)
        else:
            for h in range(HKV):
                kv_refs[j + 1][h] = jnp.concatenate([val[:, h * DH:(h + 1) * DH], ones], axis=1).astype(BF16)
    gate_dt = seg(_G0, _Z0)
    gate_ref[...] = 1.0 / (1.0 + jnp.exp(-gate_dt[:, :LANES]))
    dt_ref[...] = gate_dt[:, LANES:]
    z_ref[...] = seg(_Z0, _X0)
    xbc_ref[...] = seg(_X0, _WCOLS)


def _in_proj(x2, attn_norm, w_pad, tm=512):
    t = x2.shape[0]
    row = lambda w: pl.BlockSpec((tm, w), lambda i: (i, 0))
    full = lambda a: pl.BlockSpec(a.shape, lambda i: (0,) * a.ndim)
    g = attn_norm.reshape(1, D_MODEL)
    kv = jax.ShapeDtypeStruct((t, LANES), BF16)
    val = jax.ShapeDtypeStruct((HKV, t, LANES), BF16)
    outs = ([jax.ShapeDtypeStruct((t, NSA_WIDTH), BF16), kv, kv, kv, val, kv, val]
            + [jax.ShapeDtypeStruct((t, LANES), F32),
               jax.ShapeDtypeStruct((t, SSD_WIDTH), F32),
               jax.ShapeDtypeStruct((t, SSD_CONV_DIM), F32),
               jax.ShapeDtypeStruct((t, LANES), F32)])
    spec = lambda s: (row(s.shape[1]) if len(s.shape) == 2
                      else pl.BlockSpec((HKV, tm, LANES), lambda i: (0, i, 0)))
    return pl.pallas_call(
        _in_proj_kernel,
        out_shape=outs,
        grid=(t // tm,),
        in_specs=[row(D_MODEL), full(g), full(w_pad)],
        out_specs=[spec(s) for s in outs],
        compiler_params=_cparams("parallel"),
        name="in_proj",
    )(x2, g, w_pad)


def _compress_weights(cmp_pos, cmp_w1, cmp_w2):
    half = CMP_BLOCK // 2
    eye = jnp.eye(HKV, dtype=F32)
    w1 = cmp_w1.reshape(2, CMP_BLOCK, DH, DH)
    w1big = jnp.einsum('jlde,hk->jlhdke', w1, eye)
    w1lo = w1big[:, :half].reshape(2, half * HKV * DH, HKV * DH).astype(BF16)
    w1hi = w1big[:, half:].reshape(2, half * HKV * DH, HKV * DH).astype(BF16)
    pos = jnp.broadcast_to(cmp_pos[:, :, None, :], (2, CMP_BLOCK, HKV, DH))
    poslo = pos[:, :half].reshape(2, 1, half * HKV * DH)
    poshi = pos[:, half:].reshape(2, 1, half * HKV * DH)
    w2big = jnp.einsum('jde,hk->jhdke', cmp_w2, eye).reshape(2, HKV * DH, HKV * DH).astype(BF16)
    return w1lo, w1hi, poslo, poshi, w2big


def _compress_kernel(kr_ref, vr_ref, w1lo_ref, w1hi_ref, poslo_ref, poshi_ref, w2_ref, kc_ref, vc_ref):
    for j, (src, dst) in enumerate(((kr_ref, kc_ref), (vr_ref, vc_ref))):
        r = src[0].astype(F32)
        a = jnp.dot((r + poslo_ref[j]).astype(BF16), w1lo_ref[j], preferred_element_type=F32)
        b = jnp.dot((r + poshi_ref[j]).astype(BF16), w1hi_ref[j], preferred_element_type=F32)
        hid = a + pltpu.roll(b, b.shape[0] - 1, 0)
        out = jnp.dot(_silu(hid).astype(BF16), w2_ref[j], preferred_element_type=F32)
        if j == 0:
            dst[0] = out.astype(BF16)
        else:
            dst[0, 0] = out.astype(BF16)
            dst[0, 1] = pltpu.roll(out, DH, 1).astype(BF16)


def _compress(kr, vr, cw):
    b, ng, width = kr.shape
    w1lo, w1hi, poslo, poshi, w2big = cw
    full = lambda a: pl.BlockSpec(a.shape, lambda i: (0,) * a.ndim)
    bspec = pl.BlockSpec((1, ng, width), lambda i: (i, 0, 0))
    return pl.pallas_call(
        _compress_kernel,
        out_shape=[jax.ShapeDtypeStruct((b, ng, HKV * DH), BF16),
                   jax.ShapeDtypeStruct((b, HKV, ng, HKV * DH), BF16)],
        grid=(b,),
        in_specs=[bspec, bspec, full(w1lo), full(w1hi), full(poslo), full(poshi), full(w2big)],
        out_specs=[pl.BlockSpec((1, ng, HKV * DH), lambda i: (i, 0, 0)),
                   pl.BlockSpec((1, HKV, ng, HKV * DH), lambda i: (i, 0, 0, 0))],
        compiler_params=_cparams("parallel"),
        name="nsa_compress",
    )(kr, vr, w1lo, w1hi, poslo, poshi, w2big)


def _bucket_thresholds():
    d = np.arange(MAX_DISTANCE + 1)
    max_exact = N_BUCKETS // 2
    nf = np.maximum(d, max_exact).astype(np.float32)
    large = max_exact + (np.log(nf / np.float32(max_exact)) / np.float32(math.log(MAX_DISTANCE / max_exact))
                         * np.float32(N_BUCKETS - max_exact)).astype(np.int32)
    bucket = np.where(d < max_exact, d, np.minimum(large, N_BUCKETS - 1))
    assert np.all(np.diff(bucket) >= 0) and bucket[MAX_DISTANCE] == N_BUCKETS - 1
    return [int(np.argmax(bucket >= k)) for k in range(N_BUCKETS)]


def _bias_kernel(rb_ref, bc_ref, bw_ref, bs_ref, *, n_cmp):
    i = pl.program_id(0)
    thr = _bucket_thresholds()

    def table(dist, valid, hd, shift):
        v = jnp.full(dist.shape, rb_ref[0, hd], F32)
        for k in range(1, N_BUCKETS):
            v = jnp.where(dist >= thr[k], rb_ref[k, hd], v)
        return jnp.where(valid, (v - shift) * LOG2E, NEG)

    row = lax.broadcasted_iota(jnp.int32, (TQ, LANES), 0)
    col = lax.broadcasted_iota(jnp.int32, (TQ, LANES), 1)
    dist_c = i * TQ + row - (col * CMP_STRIDE + CMP_BLOCK - 1)
    valid_c = (dist_c >= 0) & (col < n_cmp)
    for hd in range(NSA_HEADS):
        bc_ref[hd // GQA, hd % GQA] = table(dist_c, valid_c, hd, 0.0)

    @pl.when(i == 0)
    def _():
        band = WINDOW + TQ
        qi_w = lax.broadcasted_iota(jnp.int32, (TQ, band), 0)
        dist_w = qi_w + WINDOW - lax.broadcasted_iota(jnp.int32, (TQ, band), 1)
        valid_w = (dist_w >= 0) & (dist_w < WINDOW)
        qi_s = lax.broadcasted_iota(jnp.int32, (TQ, 2 * TQ), 0)
        dist_s = qi_s + TQ - lax.broadcasted_iota(jnp.int32, (TQ, 2 * TQ), 1)
        for hd in range(NSA_HEADS):
            k, g = hd // GQA, hd % GQA
            bw_ref[k, g * TQ:(g + 1) * TQ, :] = table(dist_w, valid_w, hd, 0.0)
            bs_ref[k, g * TQ:(g + 1) * TQ, :] = table(dist_s, dist_s >= 0, hd, rb_ref[N_BUCKETS - 1, hd])


def _bias_tables(rel_bias, s):
    n_cmp = (s - CMP_BLOCK) // CMP_STRIDE + 1
    assert n_cmp < LANES and TQ >= MAX_DISTANCE
    band = WINDOW + TQ
    return pl.pallas_call(
        functools.partial(_bias_kernel, n_cmp=n_cmp),
        out_shape=[jax.ShapeDtypeStruct((HKV, GQA, s, LANES), F32),
                   jax.ShapeDtypeStruct((HKV, GQA * TQ, band), F32),
                   jax.ShapeDtypeStruct((HKV, GQA * TQ, 2 * TQ), F32)],
        grid=(s // TQ,),
        in_specs=[pl.BlockSpec(memory_space=pltpu.SMEM)],
        out_specs=[pl.BlockSpec((HKV, GQA, TQ, LANES), lambda i: (0, 0, i, 0)),
                   pl.BlockSpec((HKV, GQA * TQ, band), lambda i: (0, 0, 0)),
                   pl.BlockSpec((HKV, GQA * TQ, 2 * TQ), lambda i: (0, 0, 0))],
        compiler_params=_cparams("arbitrary"),
        name="nsa_bias_tables",
    )(rel_bias)


def _sel_tables(s):
    n_cmp = (s - CMP_BLOCK) // CMP_STRIDE + 1
    n_sel = s // SEL_BLOCK
    c_start = np.arange(LANES) * CMP_STRIDE
    s_start = np.arange(n_sel) * SEL_BLOCK
    ovl = ((c_start[None, :] < s_start[:, None] + SEL_BLOCK)
           & (c_start[None, :] + CMP_BLOCK > s_start[:, None])
           & (np.arange(LANES)[None, :] < n_cmp)).astype(np.float32)
    key_blk = np.arange(s) // SEL_BLOCK
    expand = np.where(key_blk[None, :] == np.arange(LANES)[:, None], NEG, 0.0).astype(np.float32)
    expand_wide = expand.reshape(LANES, s // (2 * TQ), 2 * TQ).transpose(1, 0, 2)
    expand = expand.reshape(LANES, s // TQ, TQ).transpose(1, 0, 2)
    return jnp.asarray(ovl, BF16), jnp.asarray(expand, BF16), jnp.asarray(expand_wide, BF16)


def _nsa_kernel(q_ref, kc_ref, vc_ref, ks_ref, vs_ref, kw_ref, vw_ref, gate_ref,
                bc_ref, bw_ref, bs_ref, ovl_ref, exp_ref, expw_ref, gn_ref, o_ref,
                qs_ref, s_ref, m_ref, acc_ref, uns_ref, oc_ref, os_ref, ob_ref):
    i = pl.program_id(1)
    rows = GQA * TQ
    n_sel = ovl_ref.shape[0]
    n_band = WINDOW // TQ + 1
    batch = range(NSA_NB)
    units = [(b, h) for b in batch for h in range(HKV)]
    nt_dims = (((1,), (1,)), ((), ()))

    zeros = jnp.zeros((TQ, DH), F32)
    for b in batch:
        for hd in range(NSA_HEADS):
            h, g = hd // GQA, hd % GQA
            piece = q_ref[b, :, hd * DH:(hd + 1) * DH].astype(F32)
            padded = jnp.concatenate([piece, zeros] if h == 0 else [zeros, piece], axis=1)
            qs_ref[b, h, g * TQ:(g + 1) * TQ, :] = padded.astype(BF16)

    def chunk_rows(c, n=1):
        return pl.ds(pl.multiple_of(c * TQ, TQ), n * TQ)

    def row_max_to_lanes():
        for u in units:
            m_ref[u] = jnp.broadcast_to(jnp.max(m_ref[u], axis=-1, keepdims=True), (rows, TQ))

    def normalise(acc):
        row_sum = pltpu.roll(acc, DH, 1)
        return acc / jnp.maximum(row_sum, 1e-30)

    sc_all = [lax.dot_general(qs_ref[b].reshape(HKV * rows, LANES), kc_ref[b], nt_dims,
                              preferred_element_type=F32) for b in batch]
    for b, h in units:
        bias_c = bc_ref[h].reshape(rows, LANES)
        sc = sc_all[b][h * rows:(h + 1) * rows] + bias_c
        mc = jnp.max(sc, axis=-1, keepdims=True)
        pc = jnp.where(bias_c > 0.5 * NEG, jnp.exp2(sc - mc), 0.0)
        pc = pc / jnp.maximum(jnp.sum(pc, axis=-1, keepdims=True), 1e-30)
        oc_ref[b, h] = jnp.dot(pc.astype(BF16), vc_ref[b, h], preferred_element_type=F32)

        psum = jnp.sum(pc.reshape(GQA, TQ, LANES), axis=0)
        p_hi = psum.astype(BF16)
        p_lo = (psum - p_hi.astype(F32)).astype(BF16)
        imp = (lax.dot_general(ovl_ref[...], p_hi, nt_dims, preferred_element_type=F32)
               + lax.dot_general(ovl_ref[...], p_lo, nt_dims, preferred_element_type=F32))
        blk = lax.broadcasted_iota(jnp.int32, (n_sel, TQ), 0)
        tok = lax.broadcasted_iota(jnp.int32, (n_sel, TQ), 1) + i * TQ
        blk_of_t = tok // SEL_BLOCK
        forced = (blk == 0) | (blk == blk_of_t) | (blk == blk_of_t - 1)
        score = jnp.where(forced, FORCED_SCORE, jnp.where(blk <= blk_of_t, imp, -1.0))
        rank = jnp.zeros((n_sel, TQ), F32)
        for mm in range(n_sel):
            sm = score[mm:mm + 1, :]
            ahead = (sm > score) | ((sm == score) & (blk > mm))
            rank = rank + jnp.where(ahead, 1.0, 0.0)
        unsel_t = jnp.where(rank < min(SEL_TOP, n_sel), 0.0, 1.0)
        unsel_t = jnp.concatenate([unsel_t, jnp.zeros((LANES - n_sel, TQ), F32)], axis=0).astype(BF16)
        eye = (lax.broadcasted_iota(jnp.int32, (TQ, TQ), 0)
               == lax.broadcasted_iota(jnp.int32, (TQ, TQ), 1)).astype(BF16)
        uns_ref[b, h] = lax.dot_general(eye, unsel_t, nt_dims, preferred_element_type=F32).astype(BF16)

    def sel_scores(c, bias_cols):
        madd_all = jnp.dot(uns_ref[...].reshape(NSA_NB * HKV * TQ, LANES), exp_ref[c],
                           preferred_element_type=F32)
        for b in batch:
            k = ks_ref[b, chunk_rows(c), :]
            s_all = lax.dot_general(qs_ref[b].reshape(HKV * rows, LANES), k, nt_dims, preferred_element_type=F32)
            for h in range(HKV):
                madd = madd_all[(b * HKV + h) * TQ:(b * HKV + h + 1) * TQ]
                s = s_all[h * rows:(h + 1) * rows]
                s = (s.reshape(GQA, TQ, TQ) + madd[None]).reshape(rows, TQ)
                if bias_cols is not None:
                    s = s + bs_ref[h, :, bias_cols:bias_cols + TQ]
                s_ref[b, h, c] = s
                m_ref[b, h] = jnp.maximum(m_ref[b, h], s)

    def sel_scores_pair(c0, expand, near):
        madd_all = jnp.dot(uns_ref[...].reshape(NSA_NB * HKV * TQ, LANES), expand, preferred_element_type=F32)
        for b in batch:
            k = ks_ref[b, chunk_rows(c0, 2), :]
            s_all = lax.dot_general(qs_ref[b].reshape(HKV * rows, LANES), k, nt_dims, preferred_element_type=F32)
            for h in range(HKV):
                madd = madd_all[(b * HKV + h) * TQ:(b * HKV + h + 1) * TQ]
                s = s_all[h * rows:(h + 1) * rows]
                s = (s.reshape(GQA, TQ, 2 * TQ) + madd[None]).reshape(rows, 2 * TQ)
                if near:
                    s = s + bs_ref[h]
                s_ref[b, h, c0] = s[:, :TQ]
                s_ref[b, h, c0 + 1] = s[:, TQ:]
                m_ref[b, h] = jnp.maximum(m_ref[b, h], jnp.maximum(s[:, :TQ], s[:, TQ:]))

    def sel_scores_wide(j):
        sel_scores_pair(2 * j, expw_ref[j], False)

    def sel_scores_near():
        expand = jnp.concatenate([exp_ref[i - 1], exp_ref[i]], axis=1)
        sel_scores_pair(i - 1, expand, True)

    def softmax_pv(c):
        for b, h in units:
            p = jnp.exp2(s_ref[b, h, c] - m_ref[b, h])
            acc_ref[b, h] += jnp.dot(p.astype(BF16), vs_ref[h, b, chunk_rows(c), :], preferred_element_type=F32)

    def softmax_pv_wide(j):
        for b, h in units:
            m = m_ref[b, h]
            p = jnp.concatenate([jnp.exp2(s_ref[b, h, 2 * j] - m), jnp.exp2(s_ref[b, h, 2 * j + 1] - m)], axis=1)
            acc_ref[b, h] += jnp.dot(p.astype(BF16), vs_ref[h, b, chunk_rows(2 * j, 2), :],
                                     preferred_element_type=F32)

    def loop(n, body):
        lax.fori_loop(0, n, lambda j, carry: (body(j), carry)[1], 0)

    m_ref[...] = jnp.full(m_ref.shape, NEG, F32)
    n_far = jnp.maximum(i - 1, 0)
    loop(n_far // 2, sel_scores_wide)
    pl.when(n_far % 2 == 1)(lambda: sel_scores(n_far - 1, None))
    pl.when(i >= 1)(sel_scores_near)
    pl.when(i == 0)(lambda: sel_scores(0, TQ))
    row_max_to_lanes()
    acc_ref[...] = jnp.zeros(acc_ref.shape, F32)
    loop((i + 1) // 2, softmax_pv_wide)
    pl.when((i + 1) % 2 == 1)(lambda: softmax_pv(i))
    for u in units:
        os_ref[u] = normalise(acc_ref[u])

    band = WINDOW + TQ
    col = lax.broadcasted_iota(jnp.int32, (1, band), 1)
    before_start = jnp.where(col + i * TQ >= WINDOW, 0.0, NEG)
    o_win = {}
    for b in batch:
        k = kw_ref[b, chunk_rows(i, n_band), :]
        s_all = lax.dot_general(qs_ref[b].reshape(HKV * rows, LANES), k, nt_dims, preferred_element_type=F32)
        for h in range(HKV):
            s = s_all[h * rows:(h + 1) * rows] + bw_ref[h] + before_start
            p = jnp.exp2(s - jnp.max(s, axis=-1, keepdims=True))
            o_win[b, h] = normalise(jnp.dot(p.astype(BF16), vw_ref[h, b, chunk_rows(i, n_band), :],
                                            preferred_element_type=F32))

    for b in batch:
        gates = gate_ref[b]
        for hd in range(NSA_HEADS):
            h, g = hd // GQA, hd % GQA
            r = slice(g * TQ, (g + 1) * TQ)
            o = (gates[:, hd:hd + 1] * oc_ref[b, h, r, :]
                 + gates[:, NSA_HEADS + hd:NSA_HEADS + hd + 1] * os_ref[b, h, r, :]
                 + gates[:, 2 * NSA_HEADS + hd:2 * NSA_HEADS + hd + 1] * o_win[b, h][r])
            ob_ref[b, :, hd * DH:(hd + 1) * DH] = o[:, :DH]
        o = ob_ref[b]
        ms = jnp.mean(o * o, axis=-1, keepdims=True)
        o_ref[b] = (o * lax.rsqrt(ms + EPS) * gn_ref[...]).astype(BF16)


def _nsa(q, kc, vc, ks, vs, kw, vw, gates, tables, attn_out_norm):
    b, s, _ = q.shape
    bias_c, bias_w, bias_s, ovl, expand, expand_wide = tables
    gn = attn_out_norm.reshape(1, NSA_WIDTH)
    nb = NSA_NB
    assert b % nb == 0 and (s // TQ) % 2 == 0
    full = lambda a: pl.BlockSpec(a.shape, lambda bi, i: (0,) * a.ndim)
    tile = lambda w: pl.BlockSpec((nb, TQ, w), lambda bi, i: (bi, i, 0))
    kseq = lambda a: pl.BlockSpec((nb,) + a.shape[1:], lambda bi, i: (bi, 0, 0))
    vseq = lambda a: pl.BlockSpec((HKV, nb) + a.shape[2:], lambda bi, i: (0, bi, 0, 0))
    assert kw.shape[1] == s + WINDOW and vw.shape[2] == s + WINDOW
    rows = GQA * TQ
    unit = (nb, HKV)
    return pl.pallas_call(
        _nsa_kernel,
        out_shape=jax.ShapeDtypeStruct((b, s, NSA_WIDTH), BF16),
        grid=(b // nb, s // TQ),
        in_specs=[tile(NSA_WIDTH),
                  pl.BlockSpec((nb,) + kc.shape[1:], lambda bi, i: (bi, 0, 0)),
                  pl.BlockSpec((nb,) + vc.shape[1:], lambda bi, i: (bi, 0, 0, 0)),
                  kseq(ks), vseq(vs), kseq(kw), vseq(vw), tile(LANES),
                  pl.BlockSpec((HKV, GQA, TQ, LANES), lambda bi, i: (0, 0, i, 0)),
                  full(bias_w), full(bias_s), full(ovl), full(expand), full(expand_wide), full(gn)],
        out_specs=tile(NSA_WIDTH),
        scratch_shapes=[pltpu.VMEM(unit + (rows, LANES), BF16),
                        pltpu.VMEM(unit + (s // TQ, rows, TQ), F32),
                        pltpu.VMEM(unit + (rows, TQ), F32),
                        pltpu.VMEM(unit + (rows, LANES), F32),
                        pltpu.VMEM(unit + (TQ, LANES), BF16),
                        pltpu.VMEM(unit + (rows, LANES), F32),
                        pltpu.VMEM(unit + (rows, LANES), F32),
                        pltpu.VMEM((nb, TQ, NSA_WIDTH), F32)],
        compiler_params=_cparams("parallel", "arbitrary"),
        name="nsa_attention",
    )(q, kc, vc, ks, vs, kw, vw, gates, bias_c, bias_w, bias_s, ovl, expand, expand_wide, gn)


def _ssd_kernel(xbc_ref, z_ref, dt_ref, cw_ref, cb_ref, dtb_ref, alog_ref, dsk_ref, ng_ref, wsrc_ref, o_ref, wdst_ref,
                xbuf_ref, state_ref, y_ref):
    c = pl.program_id(1)
    wdst_ref[...] = wsrc_ref[...].astype(BF16)
    L, P, N = SSD_L, SSD_P, SSD_N
    hpg = SSD_HEADS // SSD_GROUPS
    pad = 8

    @pl.when(c == 0)
    def _():
        xbuf_ref[:, 0:pad, :] = jnp.zeros((SSD_NB, pad, SSD_CONV_DIM), F32)
        state_ref[...] = jnp.zeros(state_ref.shape, F32)

    ri = lax.broadcasted_iota(jnp.int32, (L, L), 0)
    ci = lax.broadcasted_iota(jnp.int32, (L, L), 1)
    causal = ri >= ci
    nt_dims = (((1,), (1,)), ((), ()))
    a_neg = -jnp.exp(alog_ref[...])

    for bb in range(SSD_NB):
        xbuf_ref[bb, pad:pad + L, :] = xbc_ref[bb]
        conv = cb_ref[...]
        for k in range(SSD_CONV):
            shift = SSD_CONV - 1 - k
            conv = conv + xbuf_ref[bb, pad - shift:pad - shift + L, :] * cw_ref[k:k + 1, :]
        xbuf_ref[bb, 0:pad, :] = xbuf_ref[bb, L:L + pad, :]
        xa = _silu(conv)
        xs = xa[:, :SSD_WIDTH]
        bm = xa[:, SSD_WIDTH:SSD_WIDTH + SSD_GROUPS * N]
        cm = xa[:, SSD_WIDTH + SSD_GROUPS * N:]

        dtv = dt_ref[bb] + dtb_ref[...]
        dt = jnp.maximum(dtv, 0.0) + jnp.log1p(jnp.exp(-jnp.abs(dtv)))
        cs = jnp.dot(causal.astype(F32), dt * a_neg, preferred_element_type=F32,
                     precision=lax.Precision.HIGHEST)
        cs_t = cs.T

        for gr in range(SSD_GROUPS):
            b_g = bm[:, gr * N:(gr + 1) * N]
            c_g = cm[:, gr * N:(gr + 1) * N]
            scores = lax.dot_general(c_g.astype(BF16), b_g.astype(BF16), nt_dims, preferred_element_type=F32)
            b_gt = b_g.T
            for hh in range(hpg):
                h = gr * hpg + hh
                cs_col = cs[:, h:h + 1]
                cs_row = cs_t[h:h + 1, :]
                cs_last = cs[L - 1:L, h:h + 1]
                decay = jnp.exp(jnp.where(causal, cs_col - cs_row, NEG))
                xs_h = xs[:, h * P:(h + 1) * P]
                xc = (xs_h * dt[:, h:h + 1]).astype(BF16)
                y = jnp.dot((scores * decay).astype(BF16), xc, preferred_element_type=F32)
                prev = state_ref[bb, h]
                y = y + jnp.dot((c_g * jnp.exp(cs_col)).astype(BF16), prev.astype(BF16),
                                preferred_element_type=F32)
                contrib = jnp.dot((b_gt * jnp.exp(cs_last - cs_row)).astype(BF16), xc,
                                  preferred_element_type=F32)
                state_ref[bb, h] = jnp.exp(cs_last) * prev + contrib
                y_ref[bb, :, h * P:(h + 1) * P] = y + xs_h * dsk_ref[:, h * P:(h + 1) * P]

        y = y_ref[bb] * _silu(z_ref[bb])
        gw = SSD_WIDTH // SSD_GROUPS
        for gr in range(SSD_GROUPS):
            yg = y[:, gr * gw:(gr + 1) * gw]
            ms = jnp.mean(yg * yg, axis=-1, keepdims=True)
            o_ref[bb, :, gr * gw:(gr + 1) * gw] = (yg * lax.rsqrt(ms + EPS)
                                                   * ng_ref[:, gr * gw:(gr + 1) * gw]).astype(BF16)


def _ssd(xbc, z, dt, conv_w, conv_b, dt_bias, a_log, d_skip, norm_g, w_cast):
    b, s, _ = xbc.shape
    nb = SSD_NB
    assert b % nb == 0
    n_steps = (b // nb) * (s // SSD_L)
    wr, wc = w_cast.shape
    assert wr % n_steps == 0
    wblk = pl.BlockSpec((wr // n_steps, wc), lambda bi, c: (bi * (s // SSD_L) + c, 0))
    padl = lambda v: jnp.pad(v, (0, LANES - v.shape[0])).reshape(1, LANES)
    args = (conv_w, conv_b.reshape(1, SSD_CONV_DIM), padl(dt_bias), padl(a_log),
            jnp.repeat(d_skip, SSD_P).reshape(1, SSD_WIDTH), norm_g.reshape(1, SSD_WIDTH))
    full = lambda a: pl.BlockSpec(a.shape, lambda bi, c: (0,) * a.ndim)
    blk = lambda w: pl.BlockSpec((nb, SSD_L, w), lambda bi, c: (bi, c, 0))
    return pl.pallas_call(
        _ssd_kernel,
        out_shape=[jax.ShapeDtypeStruct((b, s, SSD_WIDTH), BF16), jax.ShapeDtypeStruct((wr, wc), BF16)],
        grid=(b // nb, s // SSD_L),
        in_specs=[blk(SSD_CONV_DIM), blk(SSD_WIDTH), blk(LANES)] + [full(a) for a in args] + [wblk],
        out_specs=[blk(SSD_WIDTH), wblk],
        scratch_shapes=[pltpu.VMEM((nb, SSD_L + 8, SSD_CONV_DIM), F32),
                        pltpu.VMEM((nb, SSD_HEADS, SSD_N, SSD_P), F32),
                        pltpu.VMEM((nb, SSD_L, SSD_WIDTH), F32)],
        compiler_params=_cparams("parallel", "arbitrary"),
        name="ssd_mixer",
    )(xbc, z, dt, *args, w_cast)


def _out_proj_kernel(x_ref, on_ref, os_ref, wo_ref, fg_ref, rwt_ref, rbt_ref, wsrc_ref,
                     h_ref, hn_ref, comb_ref, slot_ref, wdst_ref):
    wdst_ref[...] = wsrc_ref[...].astype(BF16)
    h = (x_ref[...]
         + jnp.dot(on_ref[...], wo_ref[0:NSA_WIDTH, :], preferred_element_type=F32)
         + jnp.dot(os_ref[...], wo_ref[NSA_WIDTH:, :], preferred_element_type=F32))
    tm = h.shape[0]
    for j in range(H_SLAB):
        h_ref[pl.ds(j, tm, stride=H_SLAB), :] = h[:, j * LANES:(j + 1) * LANES]
    ms = jnp.mean(h * h, axis=-1, keepdims=True)
    hn = h * lax.rsqrt(ms + EPS) * fg_ref[...]
    for j in range(H_SLAB):
        hn_ref[pl.ds(j, tm, stride=H_SLAB), :] = hn[:, j * LANES:(j + 1) * LANES]

    nt_dims = (((1,), (1,)), ((), ()))
    hn_hi = hn.astype(BF16)
    hn_lo = (hn - hn_hi.astype(F32)).astype(BF16)
    rwt = rwt_ref[...]
    rwt_hi = rwt.astype(BF16)
    rwt_lo = (rwt - rwt_hi.astype(F32)).astype(BF16)
    both = lax.dot_general(jnp.concatenate([rwt_hi, rwt_lo], axis=0), hn_hi, nt_dims, preferred_element_type=F32)
    logits = (both[:N_EXPERTS] + both[N_EXPERTS:]
              + lax.dot_general(rwt_hi, hn_lo, nt_dims, preferred_element_type=F32) + rbt_ref[...])
    row = lax.broadcasted_iota(jnp.int32, logits.shape, 0)
    work = logits
    picks = []
    for _ in range(TOP_K):
        v = jnp.max(work, axis=0, keepdims=True)
        idx = jnp.min(jnp.where(work == v, row, N_EXPERTS), axis=0, keepdims=True)
        hit = row == idx
        picks.append((v, hit))
        work = jnp.where(hit, -3e38, work)
    v0 = picks[0][0]
    es = [jnp.exp(v - v0) for v, _ in picks]
    den = es[0] + es[1] + es[2] + es[3]
    comb = jnp.zeros_like(logits)
    slot = jnp.zeros(logits.shape, jnp.int32)
    for k, (e, (_, hit)) in enumerate(zip(es, picks)):
        comb = comb + jnp.where(hit, e / den, 0.0)
        slot = jnp.where(hit, k + 1, slot)
    comb_ref[...] = comb
    slot_ref[...] = slot


def _out_proj(x2, o_nsa, o_ssd, w_out, ffn_norm, router_w, router_b, w_cast, tm=512):
    t = x2.shape[0]
    wr, wc = w_cast.shape
    assert wr % (t // tm) == 0
    wblk = pl.BlockSpec((wr // (t // tm), wc), lambda i: (i, 0))
    row = lambda w: pl.BlockSpec((tm, w), lambda i: (i, 0))
    full = lambda a: pl.BlockSpec(a.shape, lambda i: (0,) * a.ndim)
    wo = w_out.astype(BF16)
    fg = ffn_norm.reshape(1, D_MODEL)
    rwt = router_w.T
    rbt = router_b.reshape(N_EXPERTS, 1)
    tok = pl.BlockSpec((N_EXPERTS, tm), lambda i: (0, i))
    return pl.pallas_call(
        _out_proj_kernel,
        out_shape=[jax.ShapeDtypeStruct((t * H_SLAB, LANES), F32),
                   jax.ShapeDtypeStruct((t * H_SLAB, LANES), F32),
                   jax.ShapeDtypeStruct((N_EXPERTS, t), F32),
                   jax.ShapeDtypeStruct((N_EXPERTS, t), jnp.int32),
                   jax.ShapeDtypeStruct((wr, wc), BF16)],
        grid=(t // tm,),
        in_specs=[row(D_MODEL), row(NSA_WIDTH), row(SSD_WIDTH), full(wo), full(fg), full(rwt), full(rbt), wblk],
        out_specs=[pl.BlockSpec((tm * H_SLAB, LANES), lambda i: (i, 0)),
                   pl.BlockSpec((tm * H_SLAB, LANES), lambda i: (i, 0)), tok, tok, wblk],
        compiler_params=_cparams("parallel"),
        name="out_proj_router",
    )(x2, o_nsa, o_ssd, wo, fg, rwt, rbt, w_cast)


MOE_CHUNK = 4096
MOE_TILE = 192
MOE_GROUP = 256
ROUTE_TILE = 256


def _route_kernel(comb_ref, slot_ref, dest_ref, wrow_ref, starts_ref, pos_ref):
    ne, tc = comb_ref.shape
    ri = lax.broadcasted_iota(jnp.int32, (ROUTE_TILE, ROUTE_TILE), 0)
    ci = lax.broadcasted_iota(jnp.int32, (ROUTE_TILE, ROUTE_TILE), 1)
    earlier = (ri < ci).astype(BF16)
    carry = jnp.zeros((ne, 1), F32)
    for j in range(tc // ROUTE_TILE):
        cols = slice(j * ROUTE_TILE, (j + 1) * ROUTE_TILE)
        sel = jnp.where(slot_ref[:, cols] > 0, 1.0, 0.0)
        pos_ref[:, cols] = jnp.dot(sel.astype(BF16), earlier, preferred_element_type=F32) + carry
        carry = carry + jnp.sum(sel, axis=1, keepdims=True)
    ei = lax.broadcasted_iota(jnp.int32, (ne, LANES), 0)
    li = lax.broadcasted_iota(jnp.int32, (ne, LANES), 1)
    starts_row = jnp.sum(jnp.where(ei < li, carry, 0.0), axis=0, keepdims=True)
    starts_col = jnp.sum(jnp.where(ei == li, starts_row, 0.0), axis=1, keepdims=True)
    starts_ref[0] = jnp.broadcast_to(starts_row, (SUBLANES, LANES)).astype(jnp.int32)
    dest = pos_ref[...] + starts_col
    slot = slot_ref[...]
    comb = comb_ref[...]
    for k in range(TOP_K):
        hit = slot == k + 1
        dest_ref[0, :, k * tc:(k + 1) * tc] = jnp.sum(jnp.where(hit, dest, 0.0), axis=0,
                                                      keepdims=True).astype(jnp.int32)
        wrow_ref[0, :, k * tc:(k + 1) * tc] = jnp.sum(jnp.where(hit, comb, 0.0), axis=0, keepdims=True)


def _route(comb, slot, tc):
    ne, t = comb.shape
    nch = t // tc
    blk = pl.BlockSpec((ne, tc), lambda c: (0, c))
    return pl.pallas_call(
        _route_kernel,
        out_shape=[jax.ShapeDtypeStruct((nch, 1, TOP_K * tc), jnp.int32),
                   jax.ShapeDtypeStruct((nch, 1, TOP_K * tc), F32),
                   jax.ShapeDtypeStruct((nch, SUBLANES, LANES), jnp.int32)],
        grid=(nch,),
        in_specs=[blk, blk],
        out_specs=[pl.BlockSpec((1, 1, TOP_K * tc), lambda c: (c, 0, 0)),
                   pl.BlockSpec((1, 1, TOP_K * tc), lambda c: (c, 0, 0)),
                   pl.BlockSpec((1, SUBLANES, LANES), lambda c: (c, 0, 0))],
        scratch_shapes=[pltpu.VMEM((ne, tc), F32)],
        compiler_params=_cparams("parallel"),
        name="moe_route",
    )(comb, slot)


def _moe_kernel(starts_ref, dest_hbm, wrow_hbm, hn_hbm, h_hbm, wgu_ref, bgu_ref, wd_ref, bd_ref, fn_ref, o_hbm,
                x_ref, acc_ref, xs0_ref, xs1_ref, y0_ref, y1_ref, ob_ref, dest_s, wrow_s, tok_s, wt_s, st_s,
                sem, osem):
    c = pl.program_id(0)
    e = pl.program_id(1)
    tc = x_ref.shape[0] // H_SLAB
    n_rows = TOP_K * tc

    def slab(i, n, width):
        return pl.ds(pl.multiple_of(i * width, width), n * width)

    half_len = n_rows + MOE_TILE
    cur = (c % 2) * half_len
    nxt = half_len - cur

    def invert(j0, u, half):
        r = half + dest_s[j0 + u]
        tok_s[r] = (j0 & (tc - 1)) + u
        wt_s[r] = wrow_s[j0 + u]

    def gather(r0, xs_ref):
        i0 = cur + r0
        for r in range(MOE_TILE):
            xs_ref[slab(r, 1, H_SLAB), :] = x_ref[slab(tok_s[i0 + r], 1, H_SLAB), :]

    def ffn(xs_ref, y_ref):
        x = jnp.concatenate([xs_ref[pl.ds(j, MOE_TILE, stride=H_SLAB), :].astype(BF16) for j in range(H_SLAB)],
                            axis=1)
        gu = jnp.dot(x, wgu_ref[0], preferred_element_type=F32) + bgu_ref[0]
        gate = jnp.minimum(gu[:, :D_FF], SWIGLU_LIMIT)
        up = jnp.clip(gu[:, D_FF:], -SWIGLU_LIMIT, SWIGLU_LIMIT)
        act = (up + 1.0) * gate / (1.0 + jnp.exp(-SWIGLU_ALPHA * gate))
        y = jnp.dot(act.astype(BF16), wd_ref[0], preferred_element_type=F32) + bd_ref[0]
        for j in range(H_SLAB):
            y_ref[pl.ds(j, MOE_TILE, stride=H_SLAB), :] = y[:, j * LANES:(j + 1) * LANES]

    def scatter(r0, n_valid, y_ref):
        i0 = cur + r0
        for g in range(MOE_TILE // SUBLANES):
            rows = []
            for u in range(SUBLANES):
                r = g * SUBLANES + u
                tok = jnp.where(r < n_valid, tok_s[i0 + r], tc)
                rows.append((tok, acc_ref[slab(tok, 1, H_SLAB), :] + wt_s[i0 + r] * y_ref[slab(r, 1, H_SLAB), :]))
            for tok, val in rows:
                acc_ref[slab(tok, 1, H_SLAB), :] = val

    chunk_copies = lambda: (pltpu.make_async_copy(hn_hbm.at[slab(c * tc, tc, H_SLAB), :], x_ref, sem.at[0]),
                            pltpu.make_async_copy(h_hbm.at[slab(c * tc, tc, H_SLAB), :],
                                                  acc_ref.at[pl.ds(0, tc * H_SLAB), :], sem.at[1]))
    map_copies = lambda cc: (pltpu.make_async_copy(dest_hbm.at[cc, 0], dest_s, sem.at[2]),
                             pltpu.make_async_copy(wrow_hbm.at[cc, 0], wrow_s, sem.at[3]))

    @pl.when(e == 0)
    def _():
        for cp in chunk_copies():
            cp.start()

        @pl.when(c == 0)
        def _():
            for cp in map_copies(0):
                cp.start()
            for cp in map_copies(0):
                cp.wait()

            def invert_slice(g, carry):
                for u in range(MOE_GROUP):
                    invert(g * MOE_GROUP, u, cur)
                return carry

            lax.fori_loop(0, n_rows // MOE_GROUP, invert_slice, 0)
            for half in (0, half_len):
                for u in range(MOE_TILE):
                    tok_s[half + n_rows + u] = 0
                    wt_s[half + n_rows + u] = 0.0

        @pl.when(c + 1 < pl.num_programs(0))
        def _():
            for cp in map_copies(c + 1):
                cp.start()
            for cp in map_copies(c + 1):
                cp.wait()

        for cp in chunk_copies():
            cp.wait()
        acc_ref[pl.ds(tc * H_SLAB, H_SLAB), :] = jnp.zeros((H_SLAB, LANES), F32)
        st_s[0] = 0
        st_s[1] = 0
        st_s[2] = 0
        for y_ref in (y0_ref, y1_ref):
            y_ref[...] = jnp.zeros(y_ref.shape, F32)
        gather(0, xs0_ref)

    base = starts_ref[c * LANES + e]
    n_e = starts_ref[c * LANES + e + 1] - base
    xs_bufs = (xs0_ref, xs1_ref)
    y_bufs = (y0_ref, y1_ref)

    def tile_step(p, k, r0_next, prev_r0, prev_nv):
        first = jnp.minimum(k, n_rows // MOE_GROUP - 1) * MOE_GROUP
        for u in range(MOE_GROUP):
            invert(first, u, nxt)
        gather(r0_next, xs_bufs[1 - p])
        scatter(prev_r0, prev_nv, y_bufs[1 - p])
        ffn(xs_bufs[p], y_bufs[p])

    def tile(j, carry):
        r0 = base + j * MOE_TILE
        r0_next = jnp.minimum(r0 + MOE_TILE, base + n_e)
        k = st_s[0]
        prev_r0 = st_s[1]
        prev_nv = st_s[2]
        for p in range(2):
            pl.when(k % 2 == p)(functools.partial(tile_step, p, k, r0_next, prev_r0, prev_nv))
        st_s[0] = k + 1
        st_s[1] = r0
        st_s[2] = jnp.minimum(n_e - j * MOE_TILE, MOE_TILE)
        return carry

    lax.fori_loop(0, (n_e + MOE_TILE - 1) // MOE_TILE, tile, 0)

    @pl.when(e == pl.num_programs(1) - 1)
    def _():
        for p in range(2):
            pl.when(st_s[0] % 2 == p)(functools.partial(scatter, st_s[1], st_s[2], y_bufs[1 - p]))
        n_groups = tc // MOE_GROUP
        out_copy = lambda g, buf: pltpu.make_async_copy(
            ob_ref.at[buf], o_hbm.at[pl.ds(pl.multiple_of(c * tc + g * MOE_GROUP, MOE_GROUP), MOE_GROUP), :],
            osem.at[buf])

        def norm(g, carry):
            buf = g % 2

            @pl.when(g >= 2)
            def _():
                out_copy(g - 2, buf).wait()

            first = pl.multiple_of(g * MOE_GROUP * H_SLAB, MOE_GROUP * H_SLAB)
            hs = [acc_ref[pl.ds(first + j, MOE_GROUP, stride=H_SLAB), :] for j in range(H_SLAB)]
            ss = hs[0] * hs[0]
            for hj in hs[1:]:
                ss = ss + hj * hj
            inv = lax.rsqrt(jnp.sum(ss, axis=-1, keepdims=True) / D_MODEL + EPS)
            for j, hj in enumerate(hs):
                ob_ref[buf, :, j * LANES:(j + 1) * LANES] = hj * inv * fn_ref[:, j * LANES:(j + 1) * LANES]
            out_copy(g, buf).start()
            return carry

        lax.fori_loop(0, n_groups, norm, 0)
        for g in range(max(n_groups - 2, 0), n_groups):
            out_copy(g, g % 2).wait()


def _moe(hnp, comb, slot, h1, w_gate_up, b_gate_up, w_down, b_down, final_norm, tc=MOE_CHUNK):
    t = comb.shape[1]
    tc = min(tc, t)
    assert tc & (tc - 1) == 0 and tc % MOE_GROUP == 0 and MOE_TILE <= MOE_GROUP
    nch = t // tc
    dest, wrow, starts = _route(comb, slot, tc)
    starts = starts[:, 0, :].reshape(nch * LANES)
    wgu = w_gate_up.astype(BF16)
    wd = w_down.astype(BF16)
    bgu = b_gate_up.reshape(N_EXPERTS, 1, 2 * D_FF)
    bd = b_down.reshape(N_EXPERTS, 1, D_MODEL)
    fn = final_norm.reshape(1, D_MODEL)
    anyspace = pl.BlockSpec(memory_space=pl.ANY)
    exp = lambda a: pl.BlockSpec((1,) + a.shape[1:], lambda c, e, st: (e, 0, 0))
    return pl.pallas_call(
        _moe_kernel,
        out_shape=jax.ShapeDtypeStruct((t, D_MODEL), F32),
        grid_spec=pltpu.PrefetchScalarGridSpec(
            num_scalar_prefetch=1,
            grid=(nch, N_EXPERTS),
            in_specs=[anyspace, anyspace, anyspace, anyspace, exp(wgu), exp(bgu), exp(wd), exp(bd),
                      pl.BlockSpec(fn.shape, lambda c, e, st: (0, 0))],
            out_specs=anyspace,
            scratch_shapes=[pltpu.VMEM((tc * H_SLAB, LANES), F32),
                            pltpu.VMEM(((tc + 1) * H_SLAB, LANES), F32),
                            pltpu.VMEM((MOE_TILE * H_SLAB, LANES), F32),
                            pltpu.VMEM((MOE_TILE * H_SLAB, LANES), F32),
                            pltpu.VMEM((MOE_TILE * H_SLAB, LANES), F32),
                            pltpu.VMEM((MOE_TILE * H_SLAB, LANES), F32),
                            pltpu.VMEM((2, MOE_GROUP, D_MODEL), F32),
                            pltpu.SMEM((TOP_K * tc,), jnp.int32),
                            pltpu.SMEM((TOP_K * tc,), F32),
                            pltpu.SMEM((2 * (TOP_K * tc + MOE_TILE),), jnp.int32),
                            pltpu.SMEM((2 * (TOP_K * tc + MOE_TILE),), F32),
                            pltpu.SMEM((4,), jnp.int32),
                            pltpu.SemaphoreType.DMA((4,)),
                            pltpu.SemaphoreType.DMA((2,))]),
        compiler_params=_cparams("arbitrary", "arbitrary", vmem=MOE_VMEM_LIMIT),
        name="moe_experts",
    )(starts, dest, wrow, hnp, h1, wgu, bgu, wd, bd, fn)


def kernel(x, attn_norm, w_in, rel_bias, cmp_pos, cmp_w1, cmp_w2, attn_out_norm, conv_w, conv_b,
           dt_bias, a_log, d_skip, ssm_out_norm, w_out, ffn_norm, router_w, router_b,
           w_gate_up, b_gate_up, w_down, b_down, final_norm):
    b, s, d = x.shape
    t = b * s
    depth = w_in.shape[0]
    tables = tuple(_bias_tables(rel_bias, s)) + _sel_tables(s)
    h = x.reshape(t, d)
    for l in range(depth):
        (q, kc_raw, vc_raw, ks, vs, kw, vw, gates, z, xbc, dt) = _in_proj(h, attn_norm[l], _pad_w_in(w_in[l]))
        grp = CMP_STRIDE * HKV * DH
        kc, vc = _compress(kc_raw.reshape(b, s // CMP_STRIDE, grp), vc_raw.reshape(b, s // CMP_STRIDE, grp),
                           _compress_weights(cmp_pos[l], cmp_w1[l], cmp_w2[l]))
        seq = lambda a: a.reshape(a.shape[:-2] + (b, s, a.shape[-1]))
        front = lambda a: jnp.pad(a, [(0, 0)] * (a.ndim - 2) + [(WINDOW, 0), (0, 0)])
        o_nsa = _nsa(seq(q), kc, vc, seq(ks), seq(vs), front(seq(kw)), front(seq(vw)), seq(gates), tables,
                     attn_out_norm[l])
        o_ssd, wgu = _ssd(seq(xbc), seq(z), seq(dt), conv_w[l], conv_b[l], dt_bias[l], a_log[l], d_skip[l],
                          ssm_out_norm[l], w_gate_up[l].reshape(N_EXPERTS * D_MODEL, 2 * D_FF))
        h1, hnp, comb, slot, wd = _out_proj(h, o_nsa.reshape(t, NSA_WIDTH), o_ssd.reshape(t, SSD_WIDTH), w_out[l],
                                            ffn_norm[l], router_w[l], router_b[l],
                                            w_down[l].reshape(N_EXPERTS * D_FF, D_MODEL))
        assert depth == 1
        h = _moe(hnp, comb, slot, h1, wgu.reshape(N_EXPERTS, D_MODEL, 2 * D_FF), b_gate_up[l],
                 wd.reshape(N_EXPERTS, D_FF, D_MODEL), b_down[l], final_norm)
    return h.reshape(b, s, d)
```

```python
import functools
import math

import numpy as np
import jax
import jax.numpy as jnp
from jax import lax
from jax.experimental import pallas as pl
from jax.experimental.pallas import tpu as pltpu

F32 = jnp.float32
BF16 = jnp.bfloat16

D_MODEL = 1024
NSA_HEADS = 8
HKV = 2
GQA = NSA_HEADS // HKV
DH = 64
NSA_WIDTH = NSA_HEADS * DH
CMP_BLOCK = 32
CMP_STRIDE = 16
SEL_BLOCK = 64
SEL_TOP = 16
WINDOW = 512
TQ = 128
NSA_NB = 2
SSD_HEADS = 8
SSD_P = 64
SSD_WIDTH = SSD_HEADS * SSD_P
SSD_GROUPS = 2
SSD_N = 128
SSD_CONV = 4
SSD_L = 128
SSD_NB = 2
SSD_CONV_DIM = SSD_WIDTH + 2 * SSD_GROUPS * SSD_N
N_BUCKETS = 32
MAX_DISTANCE = 128
N_EXPERTS = 32
TOP_K = 4
D_FF = 1024
SWIGLU_LIMIT = 7.0
SWIGLU_ALPHA = 1.702

EPS = 1e-6
NEG = -1e30
FORCED_SCORE = 1e4
LOG2E = 1.4426950408889634
LANES = 128
SUBLANES = 8
H_SLAB = D_MODEL // LANES
VMEM_LIMIT = 56 * 1024 * 1024
MOE_VMEM_LIMIT = (2 * 4096 * D_MODEL * 4 + 2 * 3 * D_MODEL * D_FF * 2 + 12 * 1024 * 1024)


def _cparams(*sem, vmem=VMEM_LIMIT):
    return pltpu.CompilerParams(dimension_semantics=sem, vmem_limit_bytes=vmem)


def _silu(v):
    return v / (1.0 + jnp.exp(-v))


_Q0, _Q1 = 0, NSA_WIDTH
_KV0 = _Q1
_G0 = _KV0 + 6 * LANES
_DT0 = _G0 + LANES
_Z0 = _DT0 + LANES
_X0 = _Z0 + SSD_WIDTH
_WCOLS = _X0 + SSD_CONV_DIM
MXU_N = 256


def _pad_w_in(w_in):
    d = w_in.shape[0]
    nsa_cols = NSA_WIDTH + 6 * HKV * DH + 3 * NSA_HEADS
    wq_pad = w_in[:, :NSA_WIDTH]
    wkv = w_in[:, NSA_WIDTH:NSA_WIDTH + 6 * HKV * DH]
    wg = w_in[:, NSA_WIDTH + 6 * HKV * DH:nsa_cols]
    wg = jnp.pad(wg, ((0, 0), (0, LANES - wg.shape[1])))
    wz = w_in[:, nsa_cols:nsa_cols + SSD_WIDTH]
    wx = w_in[:, nsa_cols + SSD_WIDTH:nsa_cols + SSD_WIDTH + SSD_CONV_DIM]
    wdt = w_in[:, nsa_cols + SSD_WIDTH + SSD_CONV_DIM:]
    wdt = jnp.pad(wdt, ((0, 0), (0, LANES - wdt.shape[1])))
    return jnp.concatenate([wq_pad, wkv, wg, wdt, wz, wx], axis=1).astype(BF16)


def _in_proj_kernel(x_ref, g_ref, w_ref, q_ref, kc_ref, vc_ref, ks_ref, vs_ref, kw_ref, vw_ref,
                    gate_ref, z_ref, xbc_ref, dt_ref):
    x = x_ref[...]
    ms = jnp.mean(x * x, axis=-1, keepdims=True)
    xn = (x * lax.rsqrt(ms + EPS) * g_ref[...]).astype(BF16)

    def seg(lo, hi):
        return jnp.dot(xn, w_ref[:, lo:hi], preferred_element_type=F32)

    q_ref[...] = (seg(_Q0, _Q1) * (DH ** -0.5 * LOG2E)).astype(BF16)
    ones = jnp.ones((x.shape[0], LANES - DH), F32)
    kv_refs = (kc_ref, vc_ref, ks_ref, vs_ref, kw_ref, vw_ref)
    for j in range(0, len(kv_refs), MXU_N // LANES):
        pair = seg(_KV0 + j * LANES, _KV0 + j * LANES + MXU_N)
        kv_refs[j][...] = pair[:, :LANES].astype(BF16)
        val = pair[:, LANES:]
        if j == 0:
            kv_refs[j + 1][...] = val.astype(BF16)
        else:
            for h in range(HKV):
                kv_refs[j + 1][h] = jnp.concatenate([val[:, h * DH:(h + 1) * DH], ones], axis=1).astype(BF16)
    gate_dt = seg(_G0, _Z0)
    gate_ref[...] = 1.0 / (1.0 + jnp.exp(-gate_dt[:, :LANES]))
    dt_ref[...] = gate_dt[:, LANES:]
    z_ref[...] = seg(_Z0, _X0)
    xbc_ref[...] = seg(_X0, _WCOLS)


def _in_proj(x2, attn_norm, w_pad, tm=512):
    t = x2.shape[0]
    row = lambda w: pl.BlockSpec((tm, w), lambda i: (i, 0))
    full = lambda a: pl.BlockSpec(a.shape, lambda i: (0,) * a.ndim)
    g = attn_norm.reshape(1, D_MODEL)
    kv = jax.ShapeDtypeStruct((t, LANES), BF16)
    val = jax.ShapeDtypeStruct((HKV, t, LANES), BF16)
    outs = ([jax.ShapeDtypeStruct((t, NSA_WIDTH), BF16), kv, kv, kv, val, kv, val]
            + [jax.ShapeDtypeStruct((t, LANES), F32),
               jax.ShapeDtypeStruct((t, SSD_WIDTH), F32),
               jax.ShapeDtypeStruct((t, SSD_CONV_DIM), F32),
               jax.ShapeDtypeStruct((t, LANES), F32)])
    spec = lambda s: (row(s.shape[1]) if len(s.shape) == 2
                      else pl.BlockSpec((HKV, tm, LANES), lambda i: (0, i, 0)))
    return pl.pallas_call(
        _in_proj_kernel,
        out_shape=outs,
        grid=(t // tm,),
        in_specs=[row(D_MODEL), full(g), full(w_pad)],
        out_specs=[spec(s) for s in outs],
        compiler_params=_cparams("parallel"),
        name="in_proj",
    )(x2, g, w_pad)


def _compress_weights(cmp_pos, cmp_w1, cmp_w2):
    half = CMP_BLOCK // 2
    eye = jnp.eye(HKV, dtype=F32)
    w1 = cmp_w1.reshape(2, CMP_BLOCK, DH, DH)
    w1big = jnp.einsum('jlde,hk->jlhdke', w1, eye)
    w1lo = w1big[:, :half].reshape(2, half * HKV * DH, HKV * DH).astype(BF16)
    w1hi = w1big[:, half:].reshape(2, half * HKV * DH, HKV * DH).astype(BF16)
    pos = jnp.broadcast_to(cmp_pos[:, :, None, :], (2, CMP_BLOCK, HKV, DH))
    poslo = pos[:, :half].reshape(2, 1, half * HKV * DH)
    poshi = pos[:, half:].reshape(2, 1, half * HKV * DH)
    w2big = jnp.einsum('jde,hk->jhdke', cmp_w2, eye).reshape(2, HKV * DH, HKV * DH).astype(BF16)
    return w1lo, w1hi, poslo, poshi, w2big


def _compress_kernel(kr_ref, vr_ref, w1lo_ref, w1hi_ref, poslo_ref, poshi_ref, w2_ref, kc_ref, vc_ref):
    for j, (src, dst) in enumerate(((kr_ref, kc_ref), (vr_ref, vc_ref))):
        r = src[0].astype(F32)
        a = jnp.dot((r + poslo_ref[j]).astype(BF16), w1lo_ref[j], preferred_element_type=F32)
        b = jnp.dot((r + poshi_ref[j]).astype(BF16), w1hi_ref[j], preferred_element_type=F32)
        hid = a + pltpu.roll(b, b.shape[0] - 1, 0)
        out = jnp.dot(_silu(hid).astype(BF16), w2_ref[j], preferred_element_type=F32)
        if j == 0:
            dst[0] = out.astype(BF16)
        else:
            dst[0, 0] = out.astype(BF16)
            dst[0, 1] = pltpu.roll(out, DH, 1).astype(BF16)


def _compress(kr, vr, cw):
    b, ng, width = kr.shape
    w1lo, w1hi, poslo, poshi, w2big = cw
    full = lambda a: pl.BlockSpec(a.shape, lambda i: (0,) * a.ndim)
    bspec = pl.BlockSpec((1, ng, width), lambda i: (i, 0, 0))
    return pl.pallas_call(
        _compress_kernel,
        out_shape=[jax.ShapeDtypeStruct((b, ng, HKV * DH), BF16),
                   jax.ShapeDtypeStruct((b, HKV, ng, HKV * DH), BF16)],
        grid=(b,),
        in_specs=[bspec, bspec, full(w1lo), full(w1hi), full(poslo), full(poshi), full(w2big)],
        out_specs=[pl.BlockSpec((1, ng, HKV * DH), lambda i: (i, 0, 0)),
                   pl.BlockSpec((1, HKV, ng, HKV * DH), lambda i: (i, 0, 0, 0))],
        compiler_params=_cparams("parallel"),
        name="nsa_compress",
    )(kr, vr, w1lo, w1hi, poslo, poshi, w2big)


def _bucket_thresholds():
    d = np.arange(MAX_DISTANCE + 1)
    max_exact = N_BUCKETS // 2
    nf = np.maximum(d, max_exact).astype(np.float32)
    large = max_exact + (np.log(nf / np.float32(max_exact)) / np.float32(math.log(MAX_DISTANCE / max_exact))
                         * np.float32(N_BUCKETS - max_exact)).astype(np.int32)
    bucket = np.where(d < max_exact, d, np.minimum(large, N_BUCKETS - 1))
    assert np.all(np.diff(bucket) >= 0) and bucket[MAX_DISTANCE] == N_BUCKETS - 1
    return [int(np.argmax(bucket >= k)) for k in range(N_BUCKETS)]


def _bias_kernel(rb_ref, bc_ref, bw_ref, bs_ref, *, n_cmp):
    i = pl.program_id(0)
    thr = _bucket_thresholds()

    def table(dist, valid, hd, shift):
        v = jnp.full(dist.shape, rb_ref[0, hd], F32)
        for k in range(1, N_BUCKETS):
            v = jnp.where(dist >= thr[k], rb_ref[k, hd], v)
        return jnp.where(valid, (v - shift) * LOG2E, NEG)

    row = lax.broadcasted_iota(jnp.int32, (TQ, LANES), 0)
    col = lax.broadcasted_iota(jnp.int32, (TQ, LANES), 1)
    dist_c = i * TQ + row - (col * CMP_STRIDE + CMP_BLOCK - 1)
    valid_c = (dist_c >= 0) & (col < n_cmp)
    for hd in range(NSA_HEADS):
        bc_ref[hd // GQA, hd % GQA] = table(dist_c, valid_c, hd, 0.0)

    @pl.when(i == 0)
    def _():
        band = WINDOW + TQ
        qi_w = lax.broadcasted_iota(jnp.int32, (TQ, band), 0)
        dist_w = qi_w + WINDOW - lax.broadcasted_iota(jnp.int32, (TQ, band), 1)
        valid_w = (dist_w >= 0) & (dist_w < WINDOW)
        qi_s = lax.broadcasted_iota(jnp.int32, (TQ, 2 * TQ), 0)
        dist_s = qi_s + TQ - lax.broadcasted_iota(jnp.int32, (TQ, 2 * TQ), 1)
        for hd in range(NSA_HEADS):
            k, g = hd // GQA, hd % GQA
            bw_ref[k, g * TQ:(g + 1) * TQ, :] = table(dist_w, valid_w, hd, 0.0)
            bs_ref[k, g * TQ:(g + 1) * TQ, :] = table(dist_s, dist_s >= 0, hd, rb_ref[N_BUCKETS - 1, hd])


def _bias_tables(rel_bias, s):
    n_cmp = (s - CMP_BLOCK) // CMP_STRIDE + 1
    assert n_cmp < LANES and TQ >= MAX_DISTANCE
    band = WINDOW + TQ
    return pl.pallas_call(
        functools.partial(_bias_kernel, n_cmp=n_cmp),
        out_shape=[jax.ShapeDtypeStruct((HKV, GQA, s, LANES), F32),
                   jax.ShapeDtypeStruct((HKV, GQA * TQ, band), F32),
                   jax.ShapeDtypeStruct((HKV, GQA * TQ, 2 * TQ), F32)],
        grid=(s // TQ,),
        in_specs=[pl.BlockSpec(memory_space=pltpu.SMEM)],
        out_specs=[pl.BlockSpec((HKV, GQA, TQ, LANES), lambda i: (0, 0, i, 0)),
                   pl.BlockSpec((HKV, GQA * TQ, band), lambda i: (0, 0, 0)),
                   pl.BlockSpec((HKV, GQA * TQ, 2 * TQ), lambda i: (0, 0, 0))],
        compiler_params=_cparams("arbitrary"),
        name="nsa_bias_tables",
    )(rel_bias)


def _sel_tables(s):
    n_cmp = (s - CMP_BLOCK) // CMP_STRIDE + 1
    n_sel = s // SEL_BLOCK
    c_start = np.arange(LANES) * CMP_STRIDE
    s_start = np.arange(n_sel) * SEL_BLOCK
    ovl = ((c_start[None, :] < s_start[:, None] + SEL_BLOCK)
           & (c_start[None, :] + CMP_BLOCK > s_start[:, None])
           & (np.arange(LANES)[None, :] < n_cmp)).astype(np.float32)
    key_blk = np.arange(s) // SEL_BLOCK
    expand = np.where(key_blk[None, :] == np.arange(LANES)[:, None], NEG, 0.0).astype(np.float32)
    expand_wide = expand.reshape(LANES, s // (2 * TQ), 2 * TQ).transpose(1, 0, 2)
    expand = expand.reshape(LANES, s // TQ, TQ).transpose(1, 0, 2)
    return jnp.asarray(ovl, BF16), jnp.asarray(expand, BF16), jnp.asarray(expand_wide, BF16)


def _nsa_kernel(q_ref, kc_ref, vc_ref, ks_ref, vs_ref, kw_ref, vw_ref, gate_ref,
                bc_ref, bw_ref, bs_ref, ovl_ref, exp_ref, expw_ref, gn_ref, o_ref,
                qs_ref, s_ref, m_ref, acc_ref, uns_ref, oc_ref, os_ref, ob_ref):
    i = pl.program_id(1)
    rows = GQA * TQ
    n_sel = ovl_ref.shape[0]
    n_band = WINDOW // TQ + 1
    batch = range(NSA_NB)
    units = [(b, h) for b in batch for h in range(HKV)]
    nt_dims = (((1,), (1,)), ((), ()))

    zeros = jnp.zeros((TQ, DH), F32)
    for b in batch:
        for hd in range(NSA_HEADS):
            h, g = hd // GQA, hd % GQA
            piece = q_ref[b, :, hd * DH:(hd + 1) * DH].astype(F32)
            padded = jnp.concatenate([piece, zeros] if h == 0 else [zeros, piece], axis=1)
            qs_ref[b, h, g * TQ:(g + 1) * TQ, :] = padded.astype(BF16)

    def chunk_rows(c, n=1):
        return pl.ds(pl.multiple_of(c * TQ, TQ), n * TQ)

    def row_max_to_lanes():
        for u in units:
            m_ref[u] = jnp.broadcast_to(jnp.max(m_ref[u], axis=-1, keepdims=True), (rows, TQ))

    def normalise(acc):
        row_sum = pltpu.roll(acc, DH, 1)
        return acc / jnp.maximum(row_sum, 1e-30)

    sc_all = [lax.dot_general(qs_ref[b].reshape(HKV * rows, LANES), kc_ref[b], nt_dims,
                              preferred_element_type=F32) for b in batch]
    for b, h in units:
        bias_c = bc_ref[h].reshape(rows, LANES)
        sc = sc_all[b][h * rows:(h + 1) * rows] + bias_c
        mc = jnp.max(sc, axis=-1, keepdims=True)
        pc = jnp.where(bias_c > 0.5 * NEG, jnp.exp2(sc - mc), 0.0)
        pc = pc / jnp.maximum(jnp.sum(pc, axis=-1, keepdims=True), 1e-30)
        oc_ref[b, h] = jnp.dot(pc.astype(BF16), vc_ref[b, h], preferred_element_type=F32)

        psum = jnp.sum(pc.reshape(GQA, TQ, LANES), axis=0)
        p_hi = psum.astype(BF16)
        p_lo = (psum - p_hi.astype(F32)).astype(BF16)
        imp = (lax.dot_general(ovl_ref[...], p_hi, nt_dims, preferred_element_type=F32)
               + lax.dot_general(ovl_ref[...], p_lo, nt_dims, preferred_element_type=F32))
        blk = lax.broadcasted_iota(jnp.int32, (n_sel, TQ), 0)
        tok = lax.broadcasted_iota(jnp.int32, (n_sel, TQ), 1) + i * TQ
        blk_of_t = tok // SEL_BLOCK
        forced = (blk == 0) | (blk == blk_of_t) | (blk == blk_of_t - 1)
        score = jnp.where(forced, FORCED_SCORE, jnp.where(blk <= blk_of_t, imp, -1.0))
        rank = jnp.zeros((n_sel, TQ), F32)
        for mm in range(n_sel):
            sm = score[mm:mm + 1, :]
            ahead = (sm > score) | ((sm == score) & (blk > mm))
            rank = rank + jnp.where(ahead, 1.0, 0.0)
        unsel_t = jnp.where(rank < min(SEL_TOP, n_sel), 0.0, 1.0)
        unsel_t = jnp.concatenate([unsel_t, jnp.zeros((LANES - n_sel, TQ), F32)], axis=0).astype(BF16)
        eye = (lax.broadcasted_iota(jnp.int32, (TQ, TQ), 0)
               == lax.broadcasted_iota(jnp.int32, (TQ, TQ), 1)).astype(BF16)
        uns_ref[b, h] = lax.dot_general(eye, unsel_t, nt_dims, preferred_element_type=F32).astype(BF16)

    def sel_scores(c, bias_cols):
        madd_all = jnp.dot(uns_ref[...].reshape(NSA_NB * HKV * TQ, LANES), exp_ref[c],
                           preferred_element_type=F32)
        for b in batch:
            k = ks_ref[b, chunk_rows(c), :]
            s_all = lax.dot_general(qs_ref[b].reshape(HKV * rows, LANES), k, nt_dims, preferred_element_type=F32)
            for h in range(HKV):
                madd = madd_all[(b * HKV + h) * TQ:(b * HKV + h + 1) * TQ]
                s = s_all[h * rows:(h + 1) * rows]
                s = (s.reshape(GQA, TQ, TQ) + madd[None]).reshape(rows, TQ)
                if bias_cols is not None:
                    s = s + bs_ref[h, :, bias_cols:bias_cols + TQ]
                s_ref[b, h, c] = s
                m_ref[b, h] = jnp.maximum(m_ref[b, h], s)

    def sel_scores_group(c0, n, expand, near):
        madd_all = jnp.dot(uns_ref[...].reshape(NSA_NB * HKV * TQ, LANES), expand, preferred_element_type=F32)
        for b in batch:
            k = ks_ref[b, chunk_rows(c0, n), :]
            s_all = lax.dot_general(qs_ref[b].reshape(HKV * rows, LANES), k, nt_dims, preferred_element_type=F32)
            for h in range(HKV):
                madd = madd_all[(b * HKV + h) * TQ:(b * HKV + h + 1) * TQ]
                s = s_all[h * rows:(h + 1) * rows]
                s = (s.reshape(GQA, TQ, n * TQ) + madd[None]).reshape(rows, n * TQ)
                if near:
                    s = s + bs_ref[h]
                m = m_ref[b, h]
                for u in range(n):
                    s_ref[b, h, c0 + u] = s[:, u * TQ:(u + 1) * TQ]
                    m = jnp.maximum(m, s[:, u * TQ:(u + 1) * TQ])
                m_ref[b, h] = m

    def wide_expand(j, n):
        return jnp.concatenate([expw_ref[j + u] for u in range(n)], axis=1) if n > 1 else expw_ref[j]

    def sel_scores_near():
        expand = jnp.concatenate([exp_ref[i - 1], exp_ref[i]], axis=1)
        sel_scores_group(i - 1, 2, expand, True)

    def softmax_pv_group(c0, n):
        for b, h in units:
            m = m_ref[b, h]
            p = [jnp.exp2(s_ref[b, h, c0 + u] - m) for u in range(n)]
            p = jnp.concatenate(p, axis=1) if n > 1 else p[0]
            acc_ref[b, h] += jnp.dot(p.astype(BF16), vs_ref[h, b, chunk_rows(c0, n), :],
                                     preferred_element_type=F32)

    def loop(n, body):
        lax.fori_loop(0, n, lambda j, carry: (body(j), carry)[1], 0)

    m_ref[...] = jnp.full(m_ref.shape, NEG, F32)
    n_far = jnp.maximum(i - 1, 0)
    loop(n_far // 8, lambda j: sel_scores_group(8 * j, 8, wide_expand(4 * j, 4), False))
    pl.when(n_far % 8 >= 4)(lambda: sel_scores_group(n_far // 8 * 8, 4, wide_expand(n_far // 8 * 4, 2), False))
    pl.when(n_far % 4 >= 2)(lambda: sel_scores_group(n_far // 4 * 4, 2, wide_expand(n_far // 4 * 2, 1), False))
    pl.when(n_far % 2 == 1)(lambda: sel_scores(n_far - 1, None))
    pl.when(i >= 1)(sel_scores_near)
    pl.when(i == 0)(lambda: sel_scores(0, TQ))
    row_max_to_lanes()
    acc_ref[...] = jnp.zeros(acc_ref.shape, F32)
    n_all = i + 1
    loop(n_all // 8, lambda j: softmax_pv_group(8 * j, 8))
    pl.when(n_all % 8 >= 4)(lambda: softmax_pv_group(n_all // 8 * 8, 4))
    pl.when(n_all % 4 >= 2)(lambda: softmax_pv_group(n_all // 4 * 4, 2))
    pl.when(n_all % 2 == 1)(lambda: softmax_pv_group(i, 1))
    for u in units:
        os_ref[u] = normalise(acc_ref[u])

    band = WINDOW + TQ
    col = lax.broadcasted_iota(jnp.int32, (1, band), 1)
    before_start = jnp.where(col + i * TQ >= WINDOW, 0.0, NEG)
    o_win = {}
    for b in batch:
        k = kw_ref[b, chunk_rows(i, n_band), :]
        s_all = lax.dot_general(qs_ref[b].reshape(HKV * rows, LANES), k, nt_dims, preferred_element_type=F32)
        for h in range(HKV):
            s = s_all[h * rows:(h + 1) * rows] + bw_ref[h] + before_start
            p = jnp.exp2(s - jnp.max(s, axis=-1, keepdims=True))
            o_win[b, h] = normalise(jnp.dot(p.astype(BF16), vw_ref[h, b, chunk_rows(i, n_band), :],
                                            preferred_element_type=F32))

    for b in batch:
        gates = gate_ref[b]
        for hd in range(NSA_HEADS):
            h, g = hd // GQA, hd % GQA
            r = slice(g * TQ, (g + 1) * TQ)
            o = (gates[:, hd:hd + 1] * oc_ref[b, h, r, :]
                 + gates[:, NSA_HEADS + hd:NSA_HEADS + hd + 1] * os_ref[b, h, r, :]
                 + gates[:, 2 * NSA_HEADS + hd:2 * NSA_HEADS + hd + 1] * o_win[b, h][r])
            ob_ref[b, :, hd * DH:(hd + 1) * DH] = o[:, :DH]
        o = ob_ref[b]
        ms = jnp.mean(o * o, axis=-1, keepdims=True)
        o_ref[b] = (o * lax.rsqrt(ms + EPS) * gn_ref[...]).astype(BF16)


def _nsa(q, kc, vc, ks, vs, kw, vw, gates, tables, attn_out_norm):
    b, s, _ = q.shape
    bias_c, bias_w, bias_s, ovl, expand, expand_wide = tables
    gn = attn_out_norm.reshape(1, NSA_WIDTH)
    nb = NSA_NB
    assert b % nb == 0 and (s // TQ) % 2 == 0
    full = lambda a: pl.BlockSpec(a.shape, lambda bi, i: (0,) * a.ndim)
    tile = lambda w: pl.BlockSpec((nb, TQ, w), lambda bi, i: (bi, i, 0))
    kseq = lambda a: pl.BlockSpec((nb,) + a.shape[1:], lambda bi, i: (bi, 0, 0))
    vseq = lambda a: pl.BlockSpec((HKV, nb) + a.shape[2:], lambda bi, i: (0, bi, 0, 0))
    assert kw.shape[1] == s + WINDOW and vw.shape[2] == s + WINDOW
    rows = GQA * TQ
    unit = (nb, HKV)
    return pl.pallas_call(
        _nsa_kernel,
        out_shape=jax.ShapeDtypeStruct((b, s, NSA_WIDTH), BF16),
        grid=(b // nb, s // TQ),
        in_specs=[tile(NSA_WIDTH),
                  pl.BlockSpec((nb,) + kc.shape[1:], lambda bi, i: (bi, 0, 0)),
                  pl.BlockSpec((nb,) + vc.shape[1:], lambda bi, i: (bi, 0, 0, 0)),
                  kseq(ks), vseq(vs), kseq(kw), vseq(vw), tile(LANES),
                  pl.BlockSpec((HKV, GQA, TQ, LANES), lambda bi, i: (0, 0, i, 0)),
                  full(bias_w), full(bias_s), full(ovl), full(expand), full(expand_wide), full(gn)],
        out_specs=tile(NSA_WIDTH),
        scratch_shapes=[pltpu.VMEM(unit + (rows, LANES), BF16),
                        pltpu.VMEM(unit + (s // TQ, rows, TQ), F32),
                        pltpu.VMEM(unit + (rows, TQ), F32),
                        pltpu.VMEM(unit + (rows, LANES), F32),
                        pltpu.VMEM(unit + (TQ, LANES), BF16),
                        pltpu.VMEM(unit + (rows, LANES), F32),
                        pltpu.VMEM(unit + (rows, LANES), F32),
                        pltpu.VMEM((nb, TQ, NSA_WIDTH), F32)],
        compiler_params=_cparams("parallel", "arbitrary"),
        name="nsa_attention",
    )(q, kc, vc, ks, vs, kw, vw, gates, bias_c, bias_w, bias_s, ovl, expand, expand_wide, gn)


def _ssd_kernel(xbc_ref, z_ref, dt_ref, cw_ref, cb_ref, dtb_ref, alog_ref, dsk_ref, ng_ref, wsrc_ref, o_ref, wdst_ref,
                xbuf_ref, state_ref, y_ref):
    c = pl.program_id(1)
    wdst_ref[...] = wsrc_ref[...].astype(BF16)
    L, P, N = SSD_L, SSD_P, SSD_N
    hpg = SSD_HEADS // SSD_GROUPS
    pad = 8

    @pl.when(c == 0)
    def _():
        xbuf_ref[:, 0:pad, :] = jnp.zeros((SSD_NB, pad, SSD_CONV_DIM), F32)
        state_ref[...] = jnp.zeros(state_ref.shape, F32)

    ri = lax.broadcasted_iota(jnp.int32, (L, L), 0)
    ci = lax.broadcasted_iota(jnp.int32, (L, L), 1)
    causal = ri >= ci
    nt_dims = (((1,), (1,)), ((), ()))
    a_neg = -jnp.exp(alog_ref[...])

    for bb in range(SSD_NB):
        xbuf_ref[bb, pad:pad + L, :] = xbc_ref[bb]
        conv = cb_ref[...]
        for k in range(SSD_CONV):
            shift = SSD_CONV - 1 - k
            conv = conv + xbuf_ref[bb, pad - shift:pad - shift + L, :] * cw_ref[k:k + 1, :]
        xbuf_ref[bb, 0:pad, :] = xbuf_ref[bb, L:L + pad, :]
        xa = _silu(conv)
        xs = xa[:, :SSD_WIDTH]
        bm = xa[:, SSD_WIDTH:SSD_WIDTH + SSD_GROUPS * N]
        cm = xa[:, SSD_WIDTH + SSD_GROUPS * N:]

        dtv = dt_ref[bb] + dtb_ref[...]
        dt = jnp.maximum(dtv, 0.0) + jnp.log1p(jnp.exp(-jnp.abs(dtv)))
        cs = jnp.dot(causal.astype(F32), dt * a_neg, preferred_element_type=F32,
                     precision=lax.Precision.HIGHEST)
        cs_t = cs.T

        for gr in range(SSD_GROUPS):
            b_g = bm[:, gr * N:(gr + 1) * N]
            c_g = cm[:, gr * N:(gr + 1) * N]
            scores = lax.dot_general(c_g.astype(BF16), b_g.astype(BF16), nt_dims, preferred_element_type=F32)
            b_gt = b_g.T
            for hh in range(hpg):
                h = gr * hpg + hh
                cs_col = cs[:, h:h + 1]
                cs_row = cs_t[h:h + 1, :]
                cs_last = cs[L - 1:L, h:h + 1]
                decay = jnp.exp(jnp.where(causal, cs_col - cs_row, NEG))
                xs_h = xs[:, h * P:(h + 1) * P]
                xc = (xs_h * dt[:, h:h + 1]).astype(BF16)
                y = jnp.dot((scores * decay).astype(BF16), xc, preferred_element_type=F32)
                prev = state_ref[bb, h]
                y = y + jnp.dot((c_g * jnp.exp(cs_col)).astype(BF16), prev.astype(BF16),
                                preferred_element_type=F32)
                contrib = jnp.dot((b_gt * jnp.exp(cs_last - cs_row)).astype(BF16), xc,
                                  preferred_element_type=F32)
                state_ref[bb, h] = jnp.exp(cs_last) * prev + contrib
                y_ref[bb, :, h * P:(h + 1) * P] = y + xs_h * dsk_ref[:, h * P:(h + 1) * P]

        y = y_ref[bb] * _silu(z_ref[bb])
        gw = SSD_WIDTH // SSD_GROUPS
        for gr in range(SSD_GROUPS):
            yg = y[:, gr * gw:(gr + 1) * gw]
            ms = jnp.mean(yg * yg, axis=-1, keepdims=True)
            o_ref[bb, :, gr * gw:(gr + 1) * gw] = (yg * lax.rsqrt(ms + EPS)
                                                   * ng_ref[:, gr * gw:(gr + 1) * gw]).astype(BF16)


def _ssd(xbc, z, dt, conv_w, conv_b, dt_bias, a_log, d_skip, norm_g, w_cast):
    b, s, _ = xbc.shape
    nb = SSD_NB
    assert b % nb == 0
    n_steps = (b // nb) * (s // SSD_L)
    wr, wc = w_cast.shape
    assert wr % n_steps == 0
    wblk = pl.BlockSpec((wr // n_steps, wc), lambda bi, c: (bi * (s // SSD_L) + c, 0))
    padl = lambda v: jnp.pad(v, (0, LANES - v.shape[0])).reshape(1, LANES)
    args = (conv_w, conv_b.reshape(1, SSD_CONV_DIM), padl(dt_bias), padl(a_log),
            jnp.repeat(d_skip, SSD_P).reshape(1, SSD_WIDTH), norm_g.reshape(1, SSD_WIDTH))
    full = lambda a: pl.BlockSpec(a.shape, lambda bi, c: (0,) * a.ndim)
    blk = lambda w: pl.BlockSpec((nb, SSD_L, w), lambda bi, c: (bi, c, 0))
    return pl.pallas_call(
        _ssd_kernel,
        out_shape=[jax.ShapeDtypeStruct((b, s, SSD_WIDTH), BF16), jax.ShapeDtypeStruct((wr, wc), BF16)],
        grid=(b // nb, s // SSD_L),
        in_specs=[blk(SSD_CONV_DIM), blk(SSD_WIDTH), blk(LANES)] + [full(a) for a in args] + [wblk],
        out_specs=[blk(SSD_WIDTH), wblk],
        scratch_shapes=[pltpu.VMEM((nb, SSD_L + 8, SSD_CONV_DIM), F32),
                        pltpu.VMEM((nb, SSD_HEADS, SSD_N, SSD_P), F32),
                        pltpu.VMEM((nb, SSD_L, SSD_WIDTH), F32)],
        compiler_params=_cparams("parallel", "arbitrary"),
        name="ssd_mixer",
    )(xbc, z, dt, *args, w_cast)


def _out_proj_kernel(x_ref, on_ref, os_ref, wo_ref, fg_ref, rwt_ref, rbt_ref, wsrc_ref,
                     h_ref, hn_ref, comb_ref, slot_ref, wdst_ref):
    wdst_ref[...] = wsrc_ref[...].astype(BF16)
    h = (x_ref[...]
         + jnp.dot(on_ref[...], wo_ref[0:NSA_WIDTH, :], preferred_element_type=F32)
         + jnp.dot(os_ref[...], wo_ref[NSA_WIDTH:, :], preferred_element_type=F32))
    tm = h.shape[0]
    for j in range(H_SLAB):
        h_ref[pl.ds(j, tm, stride=H_SLAB), :] = h[:, j * LANES:(j + 1) * LANES]
    ms = jnp.mean(h * h, axis=-1, keepdims=True)
    hn = h * lax.rsqrt(ms + EPS) * fg_ref[...]
    for j in range(H_SLAB):
        hn_ref[pl.ds(j, tm, stride=H_SLAB), :] = hn[:, j * LANES:(j + 1) * LANES]

    nt_dims = (((1,), (1,)), ((), ()))
    hn_hi = hn.astype(BF16)
    hn_lo = (hn - hn_hi.astype(F32)).astype(BF16)
    rwt = rwt_ref[...]
    rwt_hi = rwt.astype(BF16)
    rwt_lo = (rwt - rwt_hi.astype(F32)).astype(BF16)
    both = lax.dot_general(jnp.concatenate([rwt_hi, rwt_lo], axis=0), hn_hi, nt_dims, preferred_element_type=F32)
    logits = (both[:N_EXPERTS] + both[N_EXPERTS:]
              + lax.dot_general(rwt_hi, hn_lo, nt_dims, preferred_element_type=F32) + rbt_ref[...])
    row = lax.broadcasted_iota(jnp.int32, logits.shape, 0)
    work = logits
    picks = []
    for _ in range(TOP_K):
        v = jnp.max(work, axis=0, keepdims=True)
        idx = jnp.min(jnp.where(work == v, row, N_EXPERTS), axis=0, keepdims=True)
        hit = row == idx
        picks.append((v, hit))
        work = jnp.where(hit, -3e38, work)
    v0 = picks[0][0]
    es = [jnp.exp(v - v0) for v, _ in picks]
    den = es[0] + es[1] + es[2] + es[3]
    comb = jnp.zeros_like(logits)
    slot = jnp.zeros(logits.shape, jnp.int32)
    for k, (e, (_, hit)) in enumerate(zip(es, picks)):
        comb = comb + jnp.where(hit, e / den, 0.0)
        slot = jnp.where(hit, k + 1, slot)
    comb_ref[...] = comb
    slot_ref[...] = slot


def _out_proj(x2, o_nsa, o_ssd, w_out, ffn_norm, router_w, router_b, w_cast, tm=512):
    t = x2.shape[0]
    wr, wc = w_cast.shape
    assert wr % (t // tm) == 0
    wblk = pl.BlockSpec((wr // (t // tm), wc), lambda i: (i, 0))
    row = lambda w: pl.BlockSpec((tm, w), lambda i: (i, 0))
    full = lambda a: pl.BlockSpec(a.shape, lambda i: (0,) * a.ndim)
    wo = w_out.astype(BF16)
    fg = ffn_norm.reshape(1, D_MODEL)
    rwt = router_w.T
    rbt = router_b.reshape(N_EXPERTS, 1)
    tok = pl.BlockSpec((N_EXPERTS, tm), lambda i: (0, i))
    return pl.pallas_call(
        _out_proj_kernel,
        out_shape=[jax.ShapeDtypeStruct((t * H_SLAB, LANES), F32),
                   jax.ShapeDtypeStruct((t * H_SLAB, LANES), F32),
                   jax.ShapeDtypeStruct((N_EXPERTS, t), F32),
                   jax.ShapeDtypeStruct((N_EXPERTS, t), jnp.int32),
                   jax.ShapeDtypeStruct((wr, wc), BF16)],
        grid=(t // tm,),
        in_specs=[row(D_MODEL), row(NSA_WIDTH), row(SSD_WIDTH), full(wo), full(fg), full(rwt), full(rbt), wblk],
        out_specs=[pl.BlockSpec((tm * H_SLAB, LANES), lambda i: (i, 0)),
                   pl.BlockSpec((tm * H_SLAB, LANES), lambda i: (i, 0)), tok, tok, wblk],
        compiler_params=_cparams("parallel"),
        name="out_proj_router",
    )(x2, o_nsa, o_ssd, wo, fg, rwt, rbt, w_cast)


MOE_CHUNK = 4096
MOE_TILE = 192
MOE_GROUP = 256
ROUTE_TILE = 256


def _route_kernel(comb_ref, slot_ref, dest_ref, wrow_ref, starts_ref, pos_ref):
    ne, tc = comb_ref.shape
    ri = lax.broadcasted_iota(jnp.int32, (ROUTE_TILE, ROUTE_TILE), 0)
    ci = lax.broadcasted_iota(jnp.int32, (ROUTE_TILE, ROUTE_TILE), 1)
    earlier = (ri < ci).astype(BF16)
    carry = jnp.zeros((ne, 1), F32)
    for j in range(tc // ROUTE_TILE):
        cols = slice(j * ROUTE_TILE, (j + 1) * ROUTE_TILE)
        sel = jnp.where(slot_ref[:, cols] > 0, 1.0, 0.0)
        pos_ref[:, cols] = jnp.dot(sel.astype(BF16), earlier, preferred_element_type=F32) + carry
        carry = carry + jnp.sum(sel, axis=1, keepdims=True)
    ei = lax.broadcasted_iota(jnp.int32, (ne, LANES), 0)
    li = lax.broadcasted_iota(jnp.int32, (ne, LANES), 1)
    starts_row = jnp.sum(jnp.where(ei < li, carry, 0.0), axis=0, keepdims=True)
    starts_col = jnp.sum(jnp.where(ei == li, starts_row, 0.0), axis=1, keepdims=True)
    starts_ref[0] = jnp.broadcast_to(starts_row, (SUBLANES, LANES)).astype(jnp.int32)
    dest = pos_ref[...] + starts_col
    slot = slot_ref[...]
    comb = comb_ref[...]
    for k in range(TOP_K):
        hit = slot == k + 1
        dest_ref[0, :, k * tc:(k + 1) * tc] = jnp.sum(jnp.where(hit, dest, 0.0), axis=0,
                                                      keepdims=True).astype(jnp.int32)
        wrow_ref[0, :, k * tc:(k + 1) * tc] = jnp.sum(jnp.where(hit, comb, 0.0), axis=0, keepdims=True)


def _route(comb, slot, tc):
    ne, t = comb.shape
    nch = t // tc
    blk = pl.BlockSpec((ne, tc), lambda c: (0, c))
    return pl.pallas_call(
        _route_kernel,
        out_shape=[jax.ShapeDtypeStruct((nch, 1, TOP_K * tc), jnp.int32),
                   jax.ShapeDtypeStruct((nch, 1, TOP_K * tc), F32),
                   jax.ShapeDtypeStruct((nch, SUBLANES, LANES), jnp.int32)],
        grid=(nch,),
        in_specs=[blk, blk],
        out_specs=[pl.BlockSpec((1, 1, TOP_K * tc), lambda c: (c, 0, 0)),
                   pl.BlockSpec((1, 1, TOP_K * tc), lambda c: (c, 0, 0)),
                   pl.BlockSpec((1, SUBLANES, LANES), lambda c: (c, 0, 0))],
        scratch_shapes=[pltpu.VMEM((ne, tc), F32)],
        compiler_params=_cparams("parallel"),
        name="moe_route",
    )(comb, slot)


def _moe_kernel(starts_ref, dest_hbm, wrow_hbm, hn_hbm, h_hbm, wgu_ref, bgu_ref, wd_ref, bd_ref, fn_ref, o_hbm,
                x_ref, acc_ref, xs0_ref, xs1_ref, y0_ref, y1_ref, ob_ref, dest_s, wrow_s, tok_s, wt_s, st_s,
                sem, osem):
    c = pl.program_id(0)
    e = pl.program_id(1)
    tc = x_ref.shape[0] // H_SLAB
    n_rows = TOP_K * tc

    def slab(i, n, width):
        return pl.ds(pl.multiple_of(i * width, width), n * width)

    half_len = n_rows + MOE_TILE
    cur = (c % 2) * half_len
    nxt = half_len - cur

    def invert(j0, u, half):
        r = half + dest_s[j0 + u]
        tok_s[r] = (j0 & (tc - 1)) + u
        wt_s[r] = wrow_s[j0 + u]

    def gather(r0, xs_ref):
        i0 = cur + r0
        for r in range(MOE_TILE):
            xs_ref[slab(r, 1, H_SLAB), :] = x_ref[slab(tok_s[i0 + r], 1, H_SLAB), :]

    def ffn(xs_ref, y_ref):
        x = jnp.concatenate([xs_ref[pl.ds(j, MOE_TILE, stride=H_SLAB), :].astype(BF16) for j in range(H_SLAB)],
                            axis=1)
        gu = jnp.dot(x, wgu_ref[0], preferred_element_type=F32) + bgu_ref[0]
        gate = jnp.minimum(gu[:, :D_FF], SWIGLU_LIMIT)
        up = jnp.clip(gu[:, D_FF:], -SWIGLU_LIMIT, SWIGLU_LIMIT)
        act = (up + 1.0) * gate / (1.0 + jnp.exp(-SWIGLU_ALPHA * gate))
        y = jnp.dot(act.astype(BF16), wd_ref[0], preferred_element_type=F32) + bd_ref[0]
        for j in range(H_SLAB):
            y_ref[pl.ds(j, MOE_TILE, stride=H_SLAB), :] = y[:, j * LANES:(j + 1) * LANES]

    def scatter(r0, n_valid, y_ref):
        i0 = cur + r0
        for g in range(MOE_TILE // SUBLANES):
            rows = []
            for u in range(SUBLANES):
                r = g * SUBLANES + u
                tok = jnp.where(r < n_valid, tok_s[i0 + r], tc)
                rows.append((tok, acc_ref[slab(tok, 1, H_SLAB), :] + wt_s[i0 + r] * y_ref[slab(r, 1, H_SLAB), :]))
            for tok, val in rows:
                acc_ref[slab(tok, 1, H_SLAB), :] = val

    chunk_copies = lambda: (pltpu.make_async_copy(hn_hbm.at[slab(c * tc, tc, H_SLAB), :], x_ref, sem.at[0]),
                            pltpu.make_async_copy(h_hbm.at[slab(c * tc, tc, H_SLAB), :],
                                                  acc_ref.at[pl.ds(0, tc * H_SLAB), :], sem.at[1]))
    map_copies = lambda cc: (pltpu.make_async_copy(dest_hbm.at[cc, 0], dest_s, sem.at[2]),
                             pltpu.make_async_copy(wrow_hbm.at[cc, 0], wrow_s, sem.at[3]))

    @pl.when(e == 0)
    def _():
        for cp in chunk_copies():
            cp.start()

        @pl.when(c == 0)
        def _():
            for cp in map_copies(0):
                cp.start()
            for cp in map_copies(0):
                cp.wait()

            def invert_slice(g, carry):
                for u in range(MOE_GROUP):
                    invert(g * MOE_GROUP, u, cur)
                return carry

            lax.fori_loop(0, n_rows // MOE_GROUP, invert_slice, 0)
            for half in (0, half_len):
                for u in range(MOE_TILE):
                    tok_s[half + n_rows + u] = 0
                    wt_s[half + n_rows + u] = 0.0

        @pl.when(c + 1 < pl.num_programs(0))
        def _():
            for cp in map_copies(c + 1):
                cp.start()
            for cp in map_copies(c + 1):
                cp.wait()

        for cp in chunk_copies():
            cp.wait()
        acc_ref[pl.ds(tc * H_SLAB, H_SLAB), :] = jnp.zeros((H_SLAB, LANES), F32)
        st_s[0] = 0
        st_s[1] = 0
        st_s[2] = 0
        for y_ref in (y0_ref, y1_ref):
            y_ref[...] = jnp.zeros(y_ref.shape, F32)
        gather(0, xs0_ref)

    base = starts_ref[c * LANES + e]
    n_e = starts_ref[c * LANES + e + 1] - base
    xs_bufs = (xs0_ref, xs1_ref)
    y_bufs = (y0_ref, y1_ref)

    def tile_step(p, k, r0_next, prev_r0, prev_nv):
        first = jnp.minimum(k, n_rows // MOE_GROUP - 1) * MOE_GROUP
        for u in range(MOE_GROUP):
            invert(first, u, nxt)
        gather(r0_next, xs_bufs[1 - p])
        scatter(prev_r0, prev_nv, y_bufs[1 - p])
        ffn(xs_bufs[p], y_bufs[p])

    def tile(j, carry):
        r0 = base + j * MOE_TILE
        r0_next = jnp.minimum(r0 + MOE_TILE, base + n_e)
        k = st_s[0]
        prev_r0 = st_s[1]
        prev_nv = st_s[2]
        for p in range(2):
            pl.when(k % 2 == p)(functools.partial(tile_step, p, k, r0_next, prev_r0, prev_nv))
        st_s[0] = k + 1
        st_s[1] = r0
        st_s[2] = jnp.minimum(n_e - j * MOE_TILE, MOE_TILE)
        return carry

    lax.fori_loop(0, (n_e + MOE_TILE - 1) // MOE_TILE, tile, 0)

    @pl.when(e == pl.num_programs(1) - 1)
    def _():
        for p in range(2):
            pl.when(st_s[0] % 2 == p)(functools.partial(scatter, st_s[1], st_s[2], y_bufs[1 - p]))
        n_groups = tc // MOE_GROUP
        out_copy = lambda g, buf: pltpu.make_async_copy(
            ob_ref.at[buf], o_hbm.at[pl.ds(pl.multiple_of(c * tc + g * MOE_GROUP, MOE_GROUP), MOE_GROUP), :],
            osem.at[buf])

        def norm(g, carry):
            buf = g % 2

            @pl.when(g >= 2)
            def _():
                out_copy(g - 2, buf).wait()

            first = pl.multiple_of(g * MOE_GROUP * H_SLAB, MOE_GROUP * H_SLAB)
            hs = [acc_ref[pl.ds(first + j, MOE_GROUP, stride=H_SLAB), :] for j in range(H_SLAB)]
            ss = hs[0] * hs[0]
            for hj in hs[1:]:
                ss = ss + hj * hj
            inv = lax.rsqrt(jnp.sum(ss, axis=-1, keepdims=True) / D_MODEL + EPS)
            for j, hj in enumerate(hs):
                ob_ref[buf, :, j * LANES:(j + 1) * LANES] = hj * inv * fn_ref[:, j * LANES:(j + 1) * LANES]
            out_copy(g, buf).start()
            return carry

        lax.fori_loop(0, n_groups, norm, 0)
        for g in range(max(n_groups - 2, 0), n_groups):
            out_copy(g, g % 2).wait()


def _moe(hnp, comb, slot, h1, w_gate_up, b_gate_up, w_down, b_down, final_norm, tc=MOE_CHUNK):
    t = comb.shape[1]
    tc = min(tc, t)
    assert tc & (tc - 1) == 0 and tc % MOE_GROUP == 0 and MOE_TILE <= MOE_GROUP
    nch = t // tc
    dest, wrow, starts = _route(comb, slot, tc)
    starts = starts[:, 0, :].reshape(nch * LANES)
    wgu = w_gate_up.astype(BF16)
    wd = w_down.astype(BF16)
    bgu = b_gate_up.reshape(N_EXPERTS, 1, 2 * D_FF)
    bd = b_down.reshape(N_EXPERTS, 1, D_MODEL)
    fn = final_norm.reshape(1, D_MODEL)
    anyspace = pl.BlockSpec(memory_space=pl.ANY)
    exp = lambda a: pl.BlockSpec((1,) + a.shape[1:], lambda c, e, st: (e, 0, 0))
    return pl.pallas_call(
        _moe_kernel,
        out_shape=jax.ShapeDtypeStruct((t, D_MODEL), F32),
        grid_spec=pltpu.PrefetchScalarGridSpec(
            num_scalar_prefetch=1,
            grid=(nch, N_EXPERTS),
            in_specs=[anyspace, anyspace, anyspace, anyspace, exp(wgu), exp(bgu), exp(wd), exp(bd),
                      pl.BlockSpec(fn.shape, lambda c, e, st: (0, 0))],
            out_specs=anyspace,
            scratch_shapes=[pltpu.VMEM((tc * H_SLAB, LANES), F32),
                            pltpu.VMEM(((tc + 1) * H_SLAB, LANES), F32),
                            pltpu.VMEM((MOE_TILE * H_SLAB, LANES), F32),
                            pltpu.VMEM((MOE_TILE * H_SLAB, LANES), F32),
                            pltpu.VMEM((MOE_TILE * H_SLAB, LANES), F32),
                            pltpu.VMEM((MOE_TILE * H_SLAB, LANES), F32),
                            pltpu.VMEM((2, MOE_GROUP, D_MODEL), F32),
                            pltpu.SMEM((TOP_K * tc,), jnp.int32),
                            pltpu.SMEM((TOP_K * tc,), F32),
                            pltpu.SMEM((2 * (TOP_K * tc + MOE_TILE),), jnp.int32),
                            pltpu.SMEM((2 * (TOP_K * tc + MOE_TILE),), F32),
                            pltpu.SMEM((4,), jnp.int32),
                            pltpu.SemaphoreType.DMA((4,)),
                            pltpu.SemaphoreType.DMA((2,))]),
        compiler_params=_cparams("arbitrary", "arbitrary", vmem=MOE_VMEM_LIMIT),
        name="moe_experts",
    )(starts, dest, wrow, hnp, h1, wgu, bgu, wd, bd, fn)


def kernel(x, attn_norm, w_in, rel_bias, cmp_pos, cmp_w1, cmp_w2, attn_out_norm, conv_w, conv_b,
           dt_bias, a_log, d_skip, ssm_out_norm, w_out, ffn_norm, router_w, router_b,
           w_gate_up, b_gate_up, w_down, b_down, final_norm):
    b, s, d = x.shape
    t = b * s
    depth = w_in.shape[0]
    tables = tuple(_bias_tables(rel_bias, s)) + _sel_tables(s)
    h = x.reshape(t, d)
    for l in range(depth):
        (q, kc_raw, vc_raw, ks, vs, kw, vw, gates, z, xbc, dt) = _in_proj(h, attn_norm[l], _pad_w_in(w_in[l]))
        grp = CMP_STRIDE * HKV * DH
        kc, vc = _compress(kc_raw.reshape(b, s // CMP_STRIDE, grp), vc_raw.reshape(b, s // CMP_STRIDE, grp),
                           _compress_weights(cmp_pos[l], cmp_w1[l], cmp_w2[l]))
        seq = lambda a: a.reshape(a.shape[:-2] + (b, s, a.shape[-1]))
        front = lambda a: jnp.pad(a, [(0, 0)] * (a.ndim - 2) + [(WINDOW, 0), (0, 0)])
        o_nsa = _nsa(seq(q), kc, vc, seq(ks), seq(vs), front(seq(kw)), front(seq(vw)), seq(gates), tables,
                     attn_out_norm[l])
        o_ssd, wgu = _ssd(seq(xbc), seq(z), seq(dt), conv_w[l], conv_b[l], dt_bias[l], a_log[l], d_skip[l],
                          ssm_out_norm[l], w_gate_up[l].reshape(N_EXPERTS * D_MODEL, 2 * D_FF))
        h1, hnp, comb, slot, wd = _out_proj(h, o_nsa.reshape(t, NSA_WIDTH), o_ssd.reshape(t, SSD_WIDTH), w_out[l],
                                            ffn_norm[l], router_w[l], router_b[l],
                                            w_down[l].reshape(N_EXPERTS * D_FF, D_MODEL))
        assert depth == 1
        h = _moe(hnp, comb, slot, h1, wgu.reshape(N_EXPERTS, D_MODEL, 2 * D_FF), b_gate_up[l],
                 wd.reshape(N_EXPERTS, D_FF, D_MODEL), b_down[l], final_norm)
    return h.reshape(b, s, d)
```
